```python
import math
import jax
import jax.numpy as jnp
from jax import lax
import numpy as np


D_MODEL = 1024
BATCH = 16
SEQ = 2048
DEPTH = 2
DEC_BATCH = 32
DEC_SEQ = 1
PAST_LEN = 16384
PAGE_SIZE = 128

SSM_GROUP_CH = 16
SSM_GROUPS = 24
SSM_DIM = SSM_GROUPS * SSM_GROUP_CH
SSM_STATE = 64
N_HEADS = 8
HEAD_DIM = 64
KV_HEADS = 2
ATTN_DIM = N_HEADS * HEAD_DIM
KV_DIM = KV_HEADS * HEAD_DIM
CMP_BLOCK = 32
CMP_HIDDEN = 64
SEL_BLOCK = 64
N_SELECT = 8
WINDOW = 512
QUERY_BLOCK = 128
FORCE_BONUS = 1e4
CONV_DIM = 384
CONV_WIDTH = 3
N_EXPERT_GROUPS = 4
EXPERTS_PER_GROUP = 4
N_EXPERTS = N_EXPERT_GROUPS * EXPERTS_PER_GROUP
TOP_K = 2
D_FF_EXPERT = 256
RMS_EPS = 1e-6
N_IN = SSM_DIM + ATTN_DIM + 6 * KV_DIM + 3 * N_HEADS + 3 * CONV_DIM + 3 * D_MODEL

kernel_name = "hybrid_s5_nsa_shortconv_hiermoe_step"


def rms_norm(x, g):
    x32 = x.astype(jnp.float32)
    y = x32 * lax.rsqrt(jnp.mean(x32 * x32, axis=-1, keepdims=True) + RMS_EPS)
    return (y * g.astype(jnp.float32)).astype(x.dtype)


def masked_softmax(s, mask):
    s = jnp.where(mask, s, -jnp.inf)
    m = jnp.max(s, axis=-1, keepdims=True)
    m = jnp.where(jnp.isfinite(m), m, 0.0)
    e = jnp.where(mask, jnp.exp(s - m), 0.0)
    return e / jnp.maximum(jnp.sum(e, axis=-1, keepdims=True), 1.0)


def alibi_slopes():
    return 2.0 ** (-8.0 * jnp.arange(1, N_HEADS + 1, dtype=jnp.float32) / N_HEADS)


def complex_affine_combine(e1, e2):
    ar1, ai1, br1, bi1 = e1
    ar2, ai2, br2, bi2 = e2
    return (ar1 * ar2 - ai1 * ai2,
            ar1 * ai2 + ai1 * ar2,
            ar2 * br1 - ai2 * bi1 + br2,
            ar2 * bi1 + ai2 * br1 + bi2)


def s5_branch(u, h0, a_re, a_im, log_dt, b_re, b_im, c_re, c_im, d_skip, w_glu, b_glu):
    f32 = jnp.float32
    bsz, seq = u.shape[:2]
    u32 = u.astype(f32)
    ug = u32.reshape(bsz, seq, SSM_GROUPS, SSM_GROUP_CH)
    ar = a_re.astype(f32)
    ai = a_im.astype(f32)
    dt = jnp.exp(log_dt.astype(f32))[:, None]
    mag = jnp.exp(ar * dt)
    abar_re = mag * jnp.cos(ai * dt)
    abar_im = mag * jnp.sin(ai * dt)
    den = ar * ar + ai * ai
    coef_re = ((abar_re - 1.0) * ar + abar_im * ai) / den
    coef_im = (abar_im * ar - (abar_re - 1.0) * ai) / den
    br = b_re.astype(f32)
    bi = b_im.astype(f32)
    bbar_re = coef_re[..., None] * br - coef_im[..., None] * bi
    bbar_im = coef_re[..., None] * bi + coef_im[..., None] * br
    bu_re = jnp.einsum('blgc,gpc->blgp', ug, bbar_re)
    bu_im = jnp.einsum('blgc,gpc->blgp', ug, bbar_im)
    h0r = h0[..., 0].astype(f32)
    h0i = h0[..., 1].astype(f32)
    bu_re = bu_re.at[:, 0].add(abar_re * h0r - abar_im * h0i)
    bu_im = bu_im.at[:, 0].add(abar_re * h0i + abar_im * h0r)
    a_r = jnp.broadcast_to(abar_re, bu_re.shape)
    a_i = jnp.broadcast_to(abar_im, bu_im.shape)
    _, _, hr, hi = lax.associative_scan(complex_affine_combine, (a_r, a_i, bu_re, bu_im), axis=1)
    y = (jnp.einsum('blgp,gcp->blgc', hr, c_re.astype(f32))
         - jnp.einsum('blgp,gcp->blgc', hi, c_im.astype(f32)))
    y = y.reshape(bsz, seq, SSM_DIM) + d_skip.astype(f32) * u32
    g = jax.nn.gelu(y)
    out = g * jax.nn.sigmoid(g @ w_glu.astype(f32) + b_glu.astype(f32))
    h_last = jnp.stack([hr[:, -1], hi[:, -1]], axis=-1)
    return out.astype(u.dtype), h_last.astype(h0.dtype)


def compress_blocks(k, pe, w1, w2):
    bsz, lk, kvh, hd = k.shape
    nc = lk // CMP_BLOCK
    blocks = k.reshape(bsz, nc, CMP_BLOCK, kvh, hd) + pe[None, None, :, None, :]
    flat = blocks.transpose(0, 1, 3, 2, 4).reshape(bsz, nc, kvh, CMP_BLOCK * hd)
    return jax.nn.gelu(flat @ w1) @ w2


def nsa_branch(q, gate_logits, kv_all, win_all, pos0, p0_win, cmp_pe, cmp_w1, cmp_w2):
    f32 = jnp.float32
    bsz, seq = q.shape[:2]
    grp = N_HEADS // KV_HEADS
    scale = HEAD_DIM ** -0.5
    qpos = pos0 + jnp.arange(seq, dtype=jnp.int32)
    slopes = alibi_slopes().reshape(KV_HEADS, grp)
    q5 = q.reshape(bsz, seq, KV_HEADS, grp, HEAD_DIM)

    pad = (-kv_all.shape[1]) % SEL_BLOCK
    kv = jnp.pad(kv_all, ((0, 0), (0, pad), (0, 0), (0, 0), (0, 0)))
    k_cmp, v_cmp, k_slc, v_slc = kv[:, :, 0], kv[:, :, 1], kv[:, :, 2], kv[:, :, 3]

    kc = compress_blocks(k_cmp, cmp_pe[0], cmp_w1[0], cmp_w2[0])
    vc = compress_blocks(v_cmp, cmp_pe[1], cmp_w1[1], cmp_w2[1])
    n_cmp = kc.shape[1]
    n_sel_blocks = n_cmp // (SEL_BLOCK // CMP_BLOCK)
    cend = (jnp.arange(n_cmp, dtype=jnp.int32) + 1) * CMP_BLOCK - 1
    dist_c = qpos[:, None] - cend[None, :]
    s = (jnp.einsum('blgmd,bcgd->blgmc', q5, kc).astype(f32) * scale
         - slopes[None, None, :, :, None] * dist_c.astype(f32)[None, :, None, None, :])
    p_cmp = masked_softmax(s, (dist_c >= 0)[None, :, None, None, :])
    o_cmp = jnp.einsum('blgmc,bcgd->blgmd', p_cmp.astype(vc.dtype), vc)

    imp = p_cmp.sum(axis=3).reshape(bsz, seq, KV_HEADS, n_sel_blocks, -1).sum(-1)
    blk = jnp.arange(n_sel_blocks, dtype=jnp.int32)[None, :]
    cur = (qpos // SEL_BLOCK)[:, None]
    forced = ((blk == 0) | (blk == cur) | (blk == cur - 1)).astype(f32)
    causal_blk = blk * SEL_BLOCK <= qpos[:, None]
    score = jnp.where(causal_blk[None, :, None, :], imp + FORCE_BONUS * forced[None, :, None, :], -jnp.inf)
    n_sel = min(N_SELECT, n_sel_blocks)
    _, sel_idx = lax.top_k(score, n_sel)

    ks_t = k_slc.reshape(bsz, n_sel_blocks, SEL_BLOCK, KV_HEADS, HEAD_DIM).transpose(0, 3, 1, 2, 4)
    vs_t = v_slc.reshape(bsz, n_sel_blocks, SEL_BLOCK, KV_HEADS, HEAD_DIM).transpose(0, 3, 1, 2, 4)
    win_pad = jnp.pad(win_all, ((0, 0), (WINDOW, 0), (0, 0), (0, 0), (0, 0)))
    kw, vw = win_pad[:, :, 0], win_pad[:, :, 1]

    qb = QUERY_BLOCK if seq % QUERY_BLOCK == 0 else seq
    nb = seq // qb
    q_blocks = q5.reshape(bsz, nb, qb, KV_HEADS, grp, HEAD_DIM).swapaxes(0, 1)
    idx_blocks = sel_idx.reshape(bsz, nb, qb, KV_HEADS, n_sel).swapaxes(0, 1)
    pos_blocks = qpos.reshape(nb, qb)
    b_ix = jnp.arange(bsz)[:, None, None, None]
    g_ix = jnp.arange(KV_HEADS)[None, None, :, None]
    n_keys_sel = n_sel * SEL_BLOCK

    def block_fn(args):
        qblk, iblk, pblk = args
        kg = ks_t[b_ix, g_ix, iblk].reshape(bsz, qb, KV_HEADS, n_keys_sel, HEAD_DIM)
        vg = vs_t[b_ix, g_ix, iblk].reshape(bsz, qb, KV_HEADS, n_keys_sel, HEAD_DIM)
        kpos = (iblk[..., None] * SEL_BLOCK + jnp.arange(SEL_BLOCK, dtype=jnp.int32)).reshape(bsz, qb, KV_HEADS, n_keys_sel)
        d_s = pblk[None, :, None, None] - kpos
        s1 = (jnp.einsum('bqgmd,bqgkd->bqgmk', qblk, kg).astype(f32) * scale
              - slopes[None, None, :, :, None] * d_s.astype(f32)[:, :, :, None, :])
        p1 = masked_softmax(s1, (d_s >= 0)[:, :, :, None, :])
        o_slc = jnp.einsum('bqgmk,bqgkd->bqgmd', p1.astype(vg.dtype), vg)
        start = pblk[0] - p0_win
        kwb = lax.dynamic_slice_in_dim(kw, start, qb + WINDOW, axis=1)
        vwb = lax.dynamic_slice_in_dim(vw, start, qb + WINDOW, axis=1)
        kpos_w = pblk[0] - WINDOW + jnp.arange(qb + WINDOW, dtype=jnp.int32)
        d_w = pblk[:, None] - kpos_w[None, :]
        mask_w = (kpos_w[None, :] >= p0_win) & (d_w >= 0) & (d_w < WINDOW)
        s2 = (jnp.einsum('bqgmd,bkgd->bqgmk', qblk, kwb).astype(f32) * scale
              - slopes[None, None, :, :, None] * d_w.astype(f32)[None, :, None, None, :])
        p2 = masked_softmax(s2, mask_w[None, :, None, None, :])
        o_win = jnp.einsum('bqgmk,bkgd->bqgmd', p2.astype(vwb.dtype), vwb)
        return o_slc, o_win

    o_slc, o_win = lax.map(block_fn, (q_blocks, idx_blocks, pos_blocks))
    o_slc = o_slc.swapaxes(0, 1).reshape(bsz, seq, KV_HEADS, grp, HEAD_DIM)
    o_win = o_win.swapaxes(0, 1).reshape(bsz, seq, KV_HEADS, grp, HEAD_DIM)
    gates = jax.nn.sigmoid(gate_logits.astype(f32)).reshape(bsz, seq, 3, KV_HEADS, grp)[..., None]
    o = gates[:, :, 0] * o_cmp + gates[:, :, 1] * o_slc + gates[:, :, 2] * o_win
    return o.reshape(bsz, seq, ATTN_DIM).astype(q.dtype)


def short_conv_branch(gate_b, gate_c, x_in, conv_prev, conv_w):
    seq = x_in.shape[1]
    u = gate_c * x_in
    up = jnp.concatenate([conv_prev.astype(u.dtype), u], axis=1)
    y = sum(conv_w[j] * up[:, j:j + seq] for j in range(CONV_WIDTH))
    return gate_b * y, up[:, -(CONV_WIDTH - 1):]


def token_mixers(h, pos0, kv_past, win_past, ssm_h0, conv_prev, w_in, a_re, a_im, log_dt,
                 b_re, b_im, c_re, c_im, d_skip, w_glu, b_glu, cmp_pe, cmp_w1, cmp_w2, conv_w,
                 w_br_ssm, w_br_attn, w_br_conv, w_out):
    bsz, seq, _ = h.shape
    sizes = (SSM_DIM, ATTN_DIM, 4 * KV_DIM, 2 * KV_DIM, 3 * N_HEADS, CONV_DIM, CONV_DIM, CONV_DIM, 3 * D_MODEL)
    cuts = [int(c) for c in np.cumsum(sizes)[:-1]]
    z = h @ w_in
    u_ssm, q, kv_paged, kv_win, nsa_gates, conv_b, conv_c, conv_x, merge = jnp.split(z, cuts, axis=-1)

    y_ssm, h_last = s5_branch(u_ssm, ssm_h0, a_re, a_im, log_dt, b_re, b_im, c_re, c_im, d_skip, w_glu, b_glu)

    kv_new = kv_paged.reshape(bsz, seq, 4, KV_HEADS, HEAD_DIM)
    win_new = kv_win.reshape(bsz, seq, 2, KV_HEADS, HEAD_DIM)
    kv_all = jnp.concatenate([kv_past.astype(h.dtype), kv_new], axis=1)
    win_all = jnp.concatenate([win_past.astype(h.dtype), win_new], axis=1)
    y_attn = nsa_branch(q.reshape(bsz, seq, N_HEADS, HEAD_DIM), nsa_gates, kv_all, win_all,
                        pos0, pos0 - win_past.shape[1], cmp_pe, cmp_w1, cmp_w2)

    y_conv, conv_last = short_conv_branch(conv_b, conv_c, conv_x, conv_prev, conv_w)

    g = jax.nn.sigmoid(merge.astype(jnp.float32)).astype(h.dtype)
    g_ssm, g_attn, g_conv = jnp.split(g, 3, axis=-1)
    merged = g_ssm * (y_ssm @ w_br_ssm) + g_attn * (y_attn @ w_br_attn) + g_conv * (y_conv @ w_br_conv)
    win_state = win_all[:, -min(WINDOW, win_all.shape[1]):]
    return merged @ w_out, kv_new, win_state, h_last, conv_last


def hier_moe(h, w_rg, b_rg, w_re, b_re, w_gate, w_up, w_down):
    f32 = jnp.float32
    bsz, seq, dm = h.shape
    t = h.reshape(-1, dm)
    n_tok = t.shape[0]
    lg = (t @ w_rg).astype(f32) + b_rg.astype(f32)
    pg = jax.nn.softmax(lg, axis=-1)
    gsel = jnp.argmax(lg, axis=-1)
    gw = jnp.max(pg, axis=-1)
    le = ((t @ w_re).astype(f32) + b_re.astype(f32)).reshape(n_tok, N_EXPERT_GROUPS, EXPERTS_PER_GROUP)
    le_sel = le[jnp.arange(n_tok), gsel]
    pe = jax.nn.softmax(le_sel, axis=-1)
    top_v, top_i = lax.top_k(pe, TOP_K)
    wts = top_v / jnp.sum(top_v, axis=-1, keepdims=True) * gw[:, None]
    eid = gsel[:, None] * EXPERTS_PER_GROUP + top_i
    comb = jnp.sum(jax.nn.one_hot(eid, N_EXPERTS, dtype=f32) * wts[..., None], axis=1)
    hg = jnp.einsum('nd,edf->nef', t, w_gate)
    hu = jnp.einsum('nd,edf->nef', t, w_up)
    act = jax.nn.silu(hg) * hu * comb[:, :, None].astype(t.dtype)
    y = jnp.einsum('nef,efd->nd', act, w_down)
    return y.reshape(bsz, seq, dm)


def setup_inputs(seed: int = 0) -> dict:
    key = jax.random.key(seed)
    ks = iter(jax.random.split(key, 48))
    f32 = jnp.float32

    def nrm(shape, scale):
        return scale * jax.random.normal(next(ks), shape, f32)

    n_pages = PAST_LEN // PAGE_SIZE
    n_used = DEC_BATCH * n_pages
    n_pool = n_used + max(1, n_used // 4)
    win_buf = min(WINDOW, PAST_LEN)
    n_idx = jnp.arange(SSM_STATE, dtype=f32)
    return {
        'x_prompt': nrm((BATCH, SEQ, D_MODEL), 1.0),
        'x_sample': nrm((DEC_BATCH, DEC_SEQ, D_MODEL), 1.0),
        'cache_kv': nrm((DEPTH, n_pool, PAGE_SIZE, 4, KV_HEADS, HEAD_DIM), 1.0),
        'page_table': jax.random.permutation(next(ks), n_pool)[:n_used].reshape(DEC_BATCH, n_pages).astype(jnp.int32),
        'cache_win': nrm((DEPTH, DEC_BATCH, win_buf, 2, KV_HEADS, HEAD_DIM), 1.0),
        'state_ssm': nrm((DEPTH, DEC_BATCH, SSM_GROUPS, SSM_STATE, 2), 0.5),
        'state_conv': nrm((DEPTH, DEC_BATCH, CONV_WIDTH - 1, CONV_DIM), 1.0),
        'norm_attn_g': 1.0 + nrm((DEPTH, D_MODEL), 0.01),
        'w_in': nrm((DEPTH, D_MODEL, N_IN), D_MODEL ** -0.5),
        'ssm_a_re': -0.5 + nrm((DEPTH, SSM_GROUPS, SSM_STATE), 0.01),
        'ssm_a_im': math.pi * n_idx + nrm((DEPTH, SSM_GROUPS, SSM_STATE), 0.01),
        'ssm_log_dt': jax.random.uniform(next(ks), (DEPTH, SSM_GROUPS), f32, math.log(1e-3), math.log(1e-1)),
        'ssm_b_re': nrm((DEPTH, SSM_GROUPS, SSM_STATE, SSM_GROUP_CH), (2 * SSM_GROUP_CH) ** -0.5),
        'ssm_b_im': nrm((DEPTH, SSM_GROUPS, SSM_STATE, SSM_GROUP_CH), (2 * SSM_GROUP_CH) ** -0.5),
        'ssm_c_re': nrm((DEPTH, SSM_GROUPS, SSM_GROUP_CH, SSM_STATE), SSM_STATE ** -0.5),
        'ssm_c_im': nrm((DEPTH, SSM_GROUPS, SSM_GROUP_CH, SSM_STATE), SSM_STATE ** -0.5),
        'ssm_d': nrm((DEPTH, SSM_DIM), 1.0),
        'ssm_w_glu': nrm((DEPTH, SSM_DIM, SSM_DIM), SSM_DIM ** -0.5),
        'ssm_b_glu': nrm((DEPTH, SSM_DIM), 0.01),
        'cmp_pe': nrm((DEPTH, 2, CMP_BLOCK, HEAD_DIM), 0.02),
        'cmp_w1': nrm((DEPTH, 2, CMP_BLOCK * HEAD_DIM, CMP_HIDDEN), (CMP_BLOCK * HEAD_DIM) ** -0.5),
        'cmp_w2': nrm((DEPTH, 2, CMP_HIDDEN, HEAD_DIM), CMP_HIDDEN ** -0.5),
        'conv_w': nrm((DEPTH, CONV_WIDTH, CONV_DIM), CONV_WIDTH ** -0.5),
        'w_br_ssm': nrm((DEPTH, SSM_DIM, D_MODEL), SSM_DIM ** -0.5),
        'w_br_attn': nrm((DEPTH, ATTN_DIM, D_MODEL), ATTN_DIM ** -0.5),
        'w_br_conv': nrm((DEPTH, CONV_DIM, D_MODEL), CONV_DIM ** -0.5),
        'w_out': nrm((DEPTH, D_MODEL, D_MODEL), D_MODEL ** -0.5),
        'norm_ffn_g': 1.0 + nrm((DEPTH, D_MODEL), 0.01),
        'w_router_group': nrm((DEPTH, D_MODEL, N_EXPERT_GROUPS), D_MODEL ** -0.5),
        'b_router_group': nrm((DEPTH, N_EXPERT_GROUPS), 0.01),
        'w_router_expert': nrm((DEPTH, D_MODEL, N_EXPERTS), D_MODEL ** -0.5),
        'b_router_expert': nrm((DEPTH, N_EXPERTS), 0.01),
        'moe_w_gate': nrm((DEPTH, N_EXPERTS, D_MODEL, D_FF_EXPERT), D_MODEL ** -0.5),
        'moe_w_up': nrm((DEPTH, N_EXPERTS, D_MODEL, D_FF_EXPERT), D_MODEL ** -0.5),
        'moe_w_down': nrm((DEPTH, N_EXPERTS, D_FF_EXPERT, D_MODEL), D_FF_EXPERT ** -0.5),
        'norm_final_g': 1.0 + nrm((D_MODEL,), 0.01),
    }


def reference(x_prompt, x_sample, cache_kv, page_table, cache_win, state_ssm, state_conv,
              norm_attn_g, w_in, ssm_a_re, ssm_a_im, ssm_log_dt, ssm_b_re, ssm_b_im, ssm_c_re, ssm_c_im,
              ssm_d, ssm_w_glu, ssm_b_glu, cmp_pe, cmp_w1, cmp_w2, conv_w, w_br_ssm, w_br_attn, w_br_conv,
              w_out, norm_ffn_g, w_router_group, b_router_group, w_router_expert, b_router_expert,
              moe_w_gate, moe_w_up, moe_w_down, norm_final_g):

    def run(x, pos0, kv_past_of, win_past, ssm_h0, conv_prev):
        kv_rows, win_rows, ssm_rows, conv_rows = [], [], [], []
        for l in range(DEPTH):
            h = rms_norm(x, norm_attn_g[l])
            mix, kv_new, win_state, h_last, conv_last = token_mixers(
                h, pos0, kv_past_of(l), win_past[l], ssm_h0[l], conv_prev[l], w_in[l],
                ssm_a_re[l], ssm_a_im[l], ssm_log_dt[l], ssm_b_re[l], ssm_b_im[l], ssm_c_re[l], ssm_c_im[l],
                ssm_d[l], ssm_w_glu[l], ssm_b_glu[l], cmp_pe[l], cmp_w1[l], cmp_w2[l], conv_w[l],
                w_br_ssm[l], w_br_attn[l], w_br_conv[l], w_out[l])
            x = x + mix
            x = x + hier_moe(rms_norm(x, norm_ffn_g[l]), w_router_group[l], b_router_group[l],
                             w_router_expert[l], b_router_expert[l], moe_w_gate[l], moe_w_up[l], moe_w_down[l])
            kv_rows.append(kv_new)
            win_rows.append(win_state)
            ssm_rows.append(h_last)
            conv_rows.append(conv_last)
        return (rms_norm(x, norm_final_g), jnp.stack(kv_rows), jnp.stack(win_rows),
                jnp.stack(ssm_rows), jnp.stack(conv_rows))

    bp = x_prompt.shape[0]
    dt = x_prompt.dtype
    y_prompt, kv_prompt, win_prompt, ssm_prompt, conv_prompt = run(
        x_prompt, 0,
        lambda l: jnp.zeros((bp, 0, 4, KV_HEADS, HEAD_DIM), dt),
        jnp.zeros((DEPTH, bp, 0, 2, KV_HEADS, HEAD_DIM), dt),
        jnp.zeros((DEPTH, bp, SSM_GROUPS, SSM_STATE, 2), dt),
        jnp.zeros((DEPTH, bp, CONV_WIDTH - 1, CONV_DIM), dt))

    bs = x_sample.shape[0]
    past_len = page_table.shape[1] * PAGE_SIZE
    y_sample, kv_sample, win_sample, ssm_sample, conv_sample = run(
        x_sample, past_len,
        lambda l: cache_kv[l][page_table].reshape(bs, past_len, 4, KV_HEADS, HEAD_DIM),
        cache_win, state_ssm, state_conv)

    return (y_prompt, y_sample, kv_prompt, kv_sample, win_prompt, win_sample,
            ssm_prompt, ssm_sample, conv_prompt, conv_sample)
```

```python
import functools
import math

import numpy as np
import jax
import jax.numpy as jnp
from jax import lax
from jax.experimental import pallas as pl
from jax.experimental.pallas import tpu as pltpu

F32 = jnp.float32
BF16 = jnp.bfloat16

D_MODEL = 1024
DEPTH = 2
PAGE = 128
SSM_GROUPS = 24
SSM_CH = 16
SSM_DIM = SSM_GROUPS * SSM_CH
SSM_STATE = 64
SSM_N = SSM_GROUPS * SSM_STATE
N_HEADS = 8
HEAD_DIM = 64
KV_HEADS = 2
GRP = N_HEADS // KV_HEADS
ATTN_DIM = N_HEADS * HEAD_DIM
KV_DIM = KV_HEADS * HEAD_DIM
CMP_BLOCK = 32
CMP_HIDDEN = 64
SEL_BLOCK = 64
N_SELECT = 8
WINDOW = 512
FORCE_BONUS = 1e4
CONV_DIM = 384
N_GROUPS = 4
EPG = 4
N_EXPERTS = 16
D_FF = 256
RMS_EPS = 1e-6

C_U = 0
C_Q = C_U + SSM_DIM
C_KVP = C_Q + ATTN_DIM
C_KVW = C_KVP + 4 * KV_DIM
C_GATE = C_KVW + 2 * KV_DIM
C_CONV = C_GATE + 3 * N_HEADS
C_MERGE = C_CONV + 3 * CONV_DIM
N_IN = C_MERGE + 3 * D_MODEL
W2_COLS = N_IN - C_CONV
GATE_PAD = 128
W1_COLS = C_GATE + GATE_PAD

LANES = 128
SUBLANES = 8
NEG = -1e30
M_INIT = -1e29
VMEM_LIMIT = 56 * 1024 * 1024


def _cparams(sem):
    return pltpu.CompilerParams(dimension_semantics=sem, vmem_limit_bytes=VMEM_LIMIT)


def _rms(x, g):
    ms = jnp.mean(x * x, axis=-1, keepdims=True)
    return x * lax.rsqrt(ms + RMS_EPS) * g


def _gelu_tanh(x):
    return 0.5 * x * (1.0 + jnp.tanh(math.sqrt(2.0 / math.pi) * (x + 0.044715 * (x * x * x))))


def _sigmoid(x):
    return 1.0 / (1.0 + jnp.exp(-x))


def _dot(a, b):
    return jnp.dot(a, b, preferred_element_type=F32)


def _dot_nt(a, b):
    return lax.dot_general(a, b, (((1,), (1,)), ((), ())), preferred_element_type=F32)


def _split3(x):
    hi = x.astype(BF16)
    r1 = x - hi.astype(F32)
    mid = r1.astype(BF16)
    lo = (r1 - mid.astype(F32)).astype(BF16)
    return hi, mid, lo


def _mm(a, w_ref, wlo_ref=None):
    ah = a.astype(BF16)
    if wlo_ref is None:
        return _dot(ah, w_ref[...])
    al = (a - ah.astype(F32)).astype(BF16)
    return _dot(ah, w_ref[...]) + (_dot(al, w_ref[...]) + _dot(ah, wlo_ref[...]))


def _hilo(w):
    hi = w.astype(BF16)
    return hi, (w - hi.astype(F32)).astype(BF16)


def _full(shape):
    nd = len(shape)
    return pl.BlockSpec(shape, lambda *_: (0,) * nd)


def _full1(shape):
    nd = len(shape)
    return pl.BlockSpec(shape, lambda *_: (0,) * nd, pipeline_mode=pl.Buffered(1))


def _inproj_body(x_ref, g_ref, w_ref, *refs, precise):
    wlo_ref = refs[0] if precise else None
    u_ref, q_ref, kvp_ref, kvw_ref, gt_ref = refs[1:] if precise else refs
    h = _rms(x_ref[...], g_ref[...])

    def proj(a, b):
        return _mm(h, w_ref.at[:, a:b], wlo_ref.at[:, a:b] if precise else None)

    u_ref[...] = proj(C_U, C_Q)
    q_ref[...] = proj(C_Q, C_KVP)
    kvp_ref[...] = proj(C_KVP, C_KVW)
    kvw_ref[...] = proj(C_KVW, C_GATE)
    gt_ref[...] = proj(C_GATE, W1_COLS)


def _inproj(x, g, w1, tm, w1_lo=None):
    n = x.shape[0]
    widths = (SSM_DIM, ATTN_DIM, 4 * KV_DIM, 2 * KV_DIM, GATE_PAD)
    ws = [w1] if w1_lo is None else [w1, w1_lo]
    return pl.pallas_call(
        functools.partial(_inproj_body, precise=w1_lo is not None),
        grid=(n // tm,),
        in_specs=[
            pl.BlockSpec((tm, D_MODEL), lambda i: (i, 0)),
            _full((1, D_MODEL)),
        ] + [_full((D_MODEL, W1_COLS))] * len(ws),
        out_specs=[pl.BlockSpec((tm, w), lambda i: (i, 0)) for w in widths],
        out_shape=[jax.ShapeDtypeStruct((n, w), F32) for w in widths],
        compiler_params=_cparams(("parallel",)),
        name="inproj",
    )(x, g, *ws)


S5_CHUNK = 128
S5_J = S5_CHUNK // SUBLANES


def _s5_tables_body(ar_r, ai_r, ldt_r, ar_c, ai_c, ldt_c, bre_ref, bim_ref,
                    a8re, a8im, apre, apim, a16re, a16im, bbre, bbim):
    dt = jnp.exp(ldt_r[...])
    mag = jnp.exp(ar_r[...] * dt)
    ang = ai_r[...] * dt
    are = mag * jnp.cos(ang)
    aim = mag * jnp.sin(ang)
    a8re[...] = jnp.broadcast_to(are, (SUBLANES, SSM_N))
    a8im[...] = jnp.broadcast_to(aim, (SUBLANES, SSM_N))
    pr, pi = are, aim
    for j in range(S5_J):
        apre[j * SUBLANES:(j + 1) * SUBLANES, :] = jnp.broadcast_to(pr, (SUBLANES, SSM_N))
        apim[j * SUBLANES:(j + 1) * SUBLANES, :] = jnp.broadcast_to(pi, (SUBLANES, SSM_N))
        if j + 1 < S5_J:
            pr, pi = pr * are - pi * aim, pr * aim + pi * are
    sre, sim = pr, pi
    qr, qi = sre, sim
    for s in range(SUBLANES):
        a16re[s:s + 1, :] = qr
        a16im[s:s + 1, :] = qi
        if s + 1 < SUBLANES:
            qr, qi = qr * sre - qi * sim, qr * sim + qi * sre
    dtc = jnp.exp(ldt_c[...])
    arc, aic = ar_c[...], ai_c[...]
    magc = jnp.exp(arc * dtc)
    angc = aic * dtc
    arec = magc * jnp.cos(angc)
    aimc = magc * jnp.sin(angc)
    den = arc * arc + aic * aic
    cre = ((arec - 1.0) * arc + aimc * aic) / den
    cim = (aimc * arc - (arec - 1.0) * aic) / den
    br, bi = bre_ref[...], bim_ref[...]
    bbre[...] = cre * br - cim * bi
    bbim[...] = cre * bi + cim * br


def _s5_tables(a_re, a_im, log_dt, b_re, b_im):
    ar_r = a_re.reshape(1, SSM_N)
    ai_r = a_im.reshape(1, SSM_N)
    ldt_r = jnp.repeat(log_dt, SSM_STATE).reshape(1, SSM_N)
    row = jax.ShapeDtypeStruct((SUBLANES, SSM_N), F32)
    tab = jax.ShapeDtypeStruct((S5_CHUNK, SSM_N), F32)
    col = jax.ShapeDtypeStruct((SSM_N, SSM_CH), F32)
    return pl.pallas_call(
        _s5_tables_body,
        out_shape=[row, row, tab, tab, row, row, col, col],
        name="s5_tables",
    )(ar_r, ai_r, ldt_r, ar_r.reshape(SSM_N, 1), ai_r.reshape(SSM_N, 1), ldt_r.reshape(SSM_N, 1),
      b_re.reshape(SSM_N, SSM_CH), b_im.reshape(SSM_N, SSM_CH))


def _block_diag_b(bb):
    t = bb.reshape(SSM_GROUPS, SSM_STATE, SSM_CH).transpose(0, 2, 1)
    eye = jnp.eye(SSM_GROUPS, dtype=bb.dtype)
    return (t[:, :, None, :] * eye[:, None, :, None]).reshape(SSM_DIM, SSM_N)


def _block_diag_c(c):
    t = c.transpose(0, 2, 1)
    eye = jnp.eye(SSM_GROUPS, dtype=c.dtype)
    return (t[:, :, None, :] * eye[:, None, :, None]).reshape(SSM_N, SSM_DIM)


def _s5_perm():
    p = np.zeros((S5_CHUNK, S5_CHUNK), np.float32)
    for j in range(S5_J):
        for s in range(SUBLANES):
            p[j * SUBLANES + s, S5_J * s + j] = 1.0
    return p


def _s5_epilogue(y, u, d_ref, wglu_ref, wglu_lo, bglu_ref):
    y = y + d_ref[...] * u
    g = _gelu_tanh(y)
    return g * _sigmoid(_mm(g, wglu_ref, wglu_lo) + bglu_ref[...])


def _s5_seq_body(*refs, tl, precise, carry_in):
    it = iter(refs)
    u_ref = next(it)
    h0r_ref, h0i_ref = (next(it), next(it)) if carry_in else (None, None)
    (perm_ref, permt_ref, bmat_ref, cmat_ref, a8re_ref, a8im_ref, apre_ref, apim_ref, a16re_ref, a16im_ref,
     d_ref, wglu_ref, bglu_ref) = (next(it) for _ in range(13))
    bmat_lo, cmat_lo, wglu_lo = (next(it), next(it), next(it)) if precise else (None, None, None)
    y_ref, hre_ref, him_ref, pre_r_ref, pre_i_ref, cre_scr, cim_scr = it
    li = pl.program_id(1)

    @pl.when(li == 0)
    def _():
        if carry_in:
            cre_scr[...] = h0r_ref[...]
            cim_scr[...] = h0i_ref[...]
        else:
            cre_scr[...] = jnp.zeros_like(cre_scr)
            cim_scr[...] = jnp.zeros_like(cim_scr)

    are = a8re_ref[...]
    aim = a8im_ref[...]
    row8 = lax.broadcasted_iota(jnp.int32, (SUBLANES, SSM_N), 0)
    nchunk = tl // S5_CHUNK

    for c in range(nchunk):
        if c == nchunk - 1:
            pre_r_ref[...] = cre_scr[...]
            pre_i_ref[...] = cim_scr[...]
        u = u_ref[c * S5_CHUNK:(c + 1) * S5_CHUNK, :]
        u_hi = u.astype(BF16)
        up = _dot(perm_ref[...], u_hi).astype(BF16)
        bu = _dot(up, bmat_ref[...])
        if precise:
            up_lo = _dot(perm_ref[...], (u - u_hi.astype(F32)).astype(BF16)).astype(BF16)
            bu = bu + (_dot(up_lo, bmat_ref[...]) + _dot(up, bmat_lo[...]))
        hr = [bu[0:SUBLANES, :SSM_N]]
        hi = [bu[0:SUBLANES, SSM_N:]]
        for j in range(1, S5_J):
            br = bu[j * SUBLANES:(j + 1) * SUBLANES, :SSM_N]
            bi = bu[j * SUBLANES:(j + 1) * SUBLANES, SSM_N:]
            hr.append(are * hr[-1] - aim * hi[-1] + br)
            hi.append(are * hi[-1] + aim * hr[-2] + bi)
        er, ei = hr[-1], hi[-1]
        for k, d in enumerate((1, 2, 4)):
            mr = jnp.broadcast_to(a16re_ref[d - 1:d, :], (SUBLANES, SSM_N))
            mi = jnp.broadcast_to(a16im_ref[d - 1:d, :], (SUBLANES, SSM_N))
            sr = jnp.where(row8 >= d, pltpu.roll(er, d, 0), 0.0)
            si = jnp.where(row8 >= d, pltpu.roll(ei, d, 0), 0.0)
            er, ei = er + mr * sr - mi * si, ei + mr * si + mi * sr
        h0r = jnp.broadcast_to(cre_scr[0:1, :], (SUBLANES, SSM_N))
        h0i = jnp.broadcast_to(cim_scr[0:1, :], (SUBLANES, SSM_N))
        p16r, p16i = a16re_ref[...], a16im_ref[...]
        er, ei = er + p16r * h0r - p16i * h0i, ei + p16r * h0i + p16i * h0r
        cinr = jnp.where(row8 == 0, h0r, pltpu.roll(er, 1, 0))
        cini = jnp.where(row8 == 0, h0i, pltpu.roll(ei, 1, 0))
        cre_scr[...] = jnp.broadcast_to(er[SUBLANES - 1:SUBLANES, :], (SUBLANES, SSM_N))
        cim_scr[...] = jnp.broadcast_to(ei[SUBLANES - 1:SUBLANES, :], (SUBLANES, SSM_N))
        fr, fi = [], []
        for j in range(S5_J):
            pr = apre_ref[j * SUBLANES:(j + 1) * SUBLANES, :]
            pi = apim_ref[j * SUBLANES:(j + 1) * SUBLANES, :]
            fr.append(hr[j] + pr * cinr - pi * cini)
            fi.append(hi[j] + pr * cini + pi * cinr)
        hfull = jnp.concatenate([jnp.concatenate(fr, axis=0), jnp.concatenate(fi, axis=0)], axis=1)
        yp = _mm(hfull, cmat_ref, cmat_lo)
        y_hi, y_mid, y_lo = _split3(yp)
        pt = permt_ref[...]
        y = _dot(pt, y_hi) + _dot(pt, y_mid) + _dot(pt, y_lo)
        y_ref[c * S5_CHUNK:(c + 1) * S5_CHUNK, :] = _s5_epilogue(y, u, d_ref, wglu_ref, wglu_lo, bglu_ref)

    hre_ref[...] = cre_scr[...]
    him_ref[...] = cim_scr[...]


def _s5_seq(u, s5w, bsz, seq, tl=512, precise=False, h0=None):
    nt = seq // tl
    consts = list(s5w["tabs"]) + [s5w["d"], s5w["wglu"], s5w["bglu"]]
    if precise:
        consts += [s5w["bmat_lo"], s5w["cmat_lo"], s5w["wglu_lo"]]
    state_spec = pl.BlockSpec((None, SUBLANES, SSM_N), lambda b, i: (b, 0, 0))
    state_shape = jax.ShapeDtypeStruct((bsz, SUBLANES, SSM_N), F32)
    carry = [] if h0 is None else list(h0)
    return pl.pallas_call(
        functools.partial(_s5_seq_body, tl=tl, precise=precise, carry_in=h0 is not None),
        grid=(bsz, nt),
        in_specs=[pl.BlockSpec((tl, SSM_DIM), lambda b, i: (b * nt + i, 0))]
        + [state_spec] * len(carry) + [_full(c.shape) for c in consts],
        out_specs=[pl.BlockSpec((tl, SSM_DIM), lambda b, i: (b * nt + i, 0))] + [state_spec] * 4,
        out_shape=[jax.ShapeDtypeStruct((bsz * seq, SSM_DIM), F32)] + [state_shape] * 4,
        scratch_shapes=[pltpu.VMEM((SUBLANES, SSM_N), F32), pltpu.VMEM((SUBLANES, SSM_N), F32)],
        compiler_params=_cparams(("parallel", "arbitrary")),
        name="s5_seq",
    )(u, *carry, *consts)


def _s5_step_body(u_ref, h0r_ref, h0i_ref, bmat_ref, cmat_ref, a8re_ref, a8im_ref,
                  d_ref, wglu_ref, bglu_ref, bmat_lo, cmat_lo, wglu_lo, y_ref, hre_ref, him_ref):
    u = u_ref[...]
    bu = _mm(u, bmat_ref, bmat_lo)
    are = a8re_ref[0:1, :]
    aim = a8im_ref[0:1, :]
    h0r, h0i = h0r_ref[...], h0i_ref[...]
    hr = are * h0r - aim * h0i + bu[:, :SSM_N]
    hi = are * h0i + aim * h0r + bu[:, SSM_N:]
    hre_ref[...] = hr
    him_ref[...] = hi
    y = _mm(jnp.concatenate([hr, hi], axis=1), cmat_ref, cmat_lo)
    y_ref[...] = _s5_epilogue(y, u, d_ref, wglu_ref, wglu_lo, bglu_ref)


def _s5_step(u, h0r, h0i, s5w):
    _, _, bmat, cmat, a8re, a8im = s5w["tabs"][:6]
    n = u.shape[0]
    return pl.pallas_call(
        _s5_step_body,
        out_shape=[
            jax.ShapeDtypeStruct((n, SSM_DIM), F32),
            jax.ShapeDtypeStruct((n, SSM_N), F32),
            jax.ShapeDtypeStruct((n, SSM_N), F32),
        ],
        compiler_params=pltpu.CompilerParams(vmem_limit_bytes=VMEM_LIMIT),
        name="s5_step",
    )(u, h0r, h0i, bmat, cmat, a8re, a8im, s5w["d"], s5w["wglu"], s5w["bglu"],
      s5w["bmat_lo"], s5w["cmat_lo"], s5w["wglu_lo"])


def _s5_prepare(a_re, a_im, log_dt, b_re, b_im, c_re, c_im, d_skip, w_glu, b_glu):
    a8re, a8im, apre, apim, a16re, a16im, bbre, bbim = _s5_tables(a_re, a_im, log_dt, b_re, b_im)
    bmat, bmat_lo = _hilo(jnp.concatenate([_block_diag_b(bbre), _block_diag_b(bbim)], axis=1))
    cmat, cmat_lo = _hilo(jnp.concatenate([_block_diag_c(c_re), -_block_diag_c(c_im)], axis=0))
    wglu, wglu_lo = _hilo(w_glu)
    perm = _s5_perm()
    tabs = (jnp.asarray(perm, BF16), jnp.asarray(perm.T, BF16), bmat, cmat, a8re, a8im, apre, apim, a16re, a16im)
    return dict(tabs=tabs, d=d_skip.reshape(1, SSM_DIM), wglu=wglu, bglu=b_glu.reshape(1, SSM_DIM),
                bmat_lo=bmat_lo, cmat_lo=cmat_lo, wglu_lo=wglu_lo)


TAIL = 128

MERGE_W = ("w2", "wbs", "wba", "wbc", "wo")


def _merge_weights(w2, w_br_ssm, w_br_attn, w_br_conv, w_out, conv_w):
    mw = {"cw": conv_w}
    for name, w in zip(MERGE_W, (w2, w_br_ssm, w_br_attn, w_br_conv, w_out)):
        mw[name], mw[name + "_lo"] = _hilo(w)
    return mw


def _merge_wlist(mw, precise):
    return [mw["cw"]] + [mw[n] for n in MERGE_W] + ([mw[n + "_lo"] for n in MERGE_W] if precise else [])


def _merge_core(x, g_ref, wrefs, precise, up0, up1, ys_ref, ya_ref):
    cw_ref, w2_ref, wbs_ref, wba_ref, wbc_ref, wo_ref = wrefs[:6]
    w2_lo, wbs_lo, wba_lo, wbc_lo, wo_lo = wrefs[6:] if precise else (None,) * 5
    zc = _mm(_rms(x, g_ref[...]), w2_ref, w2_lo)
    conv_b = zc[:, 0:CONV_DIM]
    uc = zc[:, CONV_DIM:2 * CONV_DIM] * zc[:, 2 * CONV_DIM:3 * CONV_DIM]
    yc = conv_b * (cw_ref[0:1, :] * up0(uc) + cw_ref[1:2, :] * up1(uc) + cw_ref[2:3, :] * uc)
    g0 = 3 * CONV_DIM
    g_ssm = _sigmoid(zc[:, g0:g0 + D_MODEL])
    g_attn = _sigmoid(zc[:, g0 + D_MODEL:g0 + 2 * D_MODEL])
    g_conv = _sigmoid(zc[:, g0 + 2 * D_MODEL:g0 + 3 * D_MODEL])
    merged = (g_ssm * _mm(ys_ref[...], wbs_ref, wbs_lo)
              + g_attn * _mm(ya_ref[...], wba_ref, wba_lo)
              + g_conv * _mm(yc, wbc_ref, wbc_lo))
    return x + _mm(merged, wo_ref, wo_lo), uc


def _merge_seq_body(*refs, tm, precise, carry_in):
    it = iter(refs)
    x_ref, g_ref, ys_ref, ya_ref = (next(it) for _ in range(4))
    cin_ref = next(it) if carry_in else None
    wrefs = [next(it) for _ in range(11 if precise else 6)]
    o_ref, cl_ref, pre_ref, stage = it
    ti = pl.program_id(1)

    @pl.when(ti == 0)
    def _():
        stage[0:SUBLANES, :] = cin_ref[...] if carry_in else jnp.zeros((SUBLANES, CONV_DIM), F32)

    def up0(uc):
        stage[SUBLANES:SUBLANES + tm, :] = uc
        return stage[SUBLANES - 2:SUBLANES - 2 + tm, :]

    def up1(uc):
        return stage[SUBLANES - 1:SUBLANES - 1 + tm, :]

    out, uc = _merge_core(x_ref[...], g_ref, wrefs, precise, up0, up1, ys_ref, ya_ref)
    o_ref[...] = out
    last = uc[tm - SUBLANES:tm, :]
    stage[0:SUBLANES, :] = last
    cl_ref[...] = last
    pre_ref[...] = uc[max(tm - TAIL, SUBLANES) - SUBLANES:max(tm - TAIL, SUBLANES), :]


def _merge_seq(x, g, ys, ya, mw, bsz, seq, tm=256, precise=False, cin=None):
    nt = seq // tm
    ws = _merge_wlist(mw, precise)
    row = lambda w: pl.BlockSpec((tm, w), lambda b, i: (b * nt + i, 0))
    cspec = pl.BlockSpec((None, SUBLANES, CONV_DIM), lambda b, i: (b, 0, 0))
    carry = [] if cin is None else [cin]
    return pl.pallas_call(
        functools.partial(_merge_seq_body, tm=tm, precise=precise, carry_in=cin is not None),
        grid=(bsz, nt),
        in_specs=[row(D_MODEL), _full(g.shape), row(SSM_DIM), row(ATTN_DIM)] + [cspec] * len(carry)
        + [_full1(w.shape) for w in ws],
        out_specs=[row(D_MODEL), cspec, cspec],
        out_shape=[jax.ShapeDtypeStruct((bsz * seq, D_MODEL), F32)]
        + [jax.ShapeDtypeStruct((bsz, SUBLANES, CONV_DIM), F32)] * 2,
        scratch_shapes=[pltpu.VMEM((tm + SUBLANES, CONV_DIM), F32)],
        compiler_params=_cparams(("parallel", "arbitrary")),
        name="merge_seq",
    )(x, g, ys, ya, *carry, *ws)


def _merge_step_body(x_ref, g_ref, ys_ref, ya_ref, p0_ref, p1_ref, *refs):
    wrefs, (o_ref, uc_ref) = refs[:11], refs[11:]
    out, uc = _merge_core(x_ref[...], g_ref, wrefs, True, lambda _: p0_ref[...], lambda _: p1_ref[...],
                          ys_ref, ya_ref)
    o_ref[...] = out
    uc_ref[...] = uc


def _merge_step(x, g, ys, ya, prev0, prev1, mw):
    n = x.shape[0]
    return pl.pallas_call(
        _merge_step_body,
        out_shape=[jax.ShapeDtypeStruct((n, D_MODEL), F32), jax.ShapeDtypeStruct((n, CONV_DIM), F32)],
        compiler_params=pltpu.CompilerParams(vmem_limit_bytes=VMEM_LIMIT),
        name="merge_step",
    )(x, g, ys, ya, prev0, prev1, *_merge_wlist(mw, True))


def _moe_route(h, wr_ref, br_ref):
    h_hi, h_mid, h_lo = _split3(h)
    w_hi, w_mid, w_lo = wr_ref[0], wr_ref[1], wr_ref[2]
    logits = (_dot(h_hi, w_hi) + (_dot(h_hi, w_mid) + _dot(h_mid, w_hi))
              + (_dot(h_hi, w_lo) + _dot(h_mid, w_mid) + _dot(h_lo, w_hi))) + br_ref[...]
    lg = [logits[:, k:k + 1] for k in range(N_GROUPS)]
    le = [logits[:, N_GROUPS + e:N_GROUPS + e + 1] for e in range(N_EXPERTS)]
    gmax = functools.reduce(jnp.maximum, lg)
    gsum = functools.reduce(lambda a, b: a + b, [jnp.exp(v - gmax) for v in lg])
    gw = 1.0 / gsum
    gsel = jnp.full_like(gmax, N_GROUPS - 1).astype(jnp.int32)
    for k in range(N_GROUPS - 2, -1, -1):
        gsel = jnp.where(lg[k] == gmax, k, gsel)
    ls = []
    for j in range(EPG):
        v = le[j]
        for k in range(1, N_GROUPS):
            v = jnp.where(gsel == k, le[k * EPG + j], v)
        ls.append(v)
    emax = functools.reduce(jnp.maximum, ls)
    ex = [jnp.exp(v - emax) for v in ls]
    esum = functools.reduce(lambda a, b: a + b, ex)
    pe = [v / esum for v in ex]
    v1 = functools.reduce(jnp.maximum, pe)
    i1 = jnp.full_like(gsel, EPG - 1)
    for j in range(EPG - 2, -1, -1):
        i1 = jnp.where(pe[j] == v1, j, i1)
    pe2 = [jnp.where(i1 == j, -1.0, pe[j]) for j in range(EPG)]
    v2 = functools.reduce(jnp.maximum, pe2)
    i2 = jnp.full_like(gsel, EPG - 1)
    for j in range(EPG - 2, -1, -1):
        i2 = jnp.where(pe2[j] == v2, j, i2)
    tot = v1 + v2
    w1 = v1 / tot * gw
    w2 = v2 / tot * gw
    e1 = gsel * EPG + i1
    e2 = gsel * EPG + i2
    lane = lax.broadcasted_iota(jnp.int32, (h.shape[0], LANES), 1)
    return jnp.where(lane == e1, w1, 0.0) + jnp.where(lane == e2, w2, 0.0)


def _moe_body(x_ref, g_ref, wr_ref, br_ref, wgu_ref, wd_ref, gf_ref, o_ref, hb_scr, comb_scr, acc_scr, *, final):
    e = pl.program_id(1)

    @pl.when(e == 0)
    def _():
        h = _rms(x_ref[...], g_ref[...])
        hb_scr[...] = h.astype(BF16)
        comb_scr[...] = _moe_route(h, wr_ref, br_ref)
        acc_scr[...] = jnp.zeros_like(acc_scr)

    hgu = _dot(hb_scr[...], wgu_ref[0])
    hg = hgu[:, :D_FF]
    lane = lax.broadcasted_iota(jnp.int32, comb_scr.shape, 1)
    ce = jnp.sum(jnp.where(lane == e, comb_scr[...], 0.0), axis=1, keepdims=True)
    act = hg * _sigmoid(hg) * hgu[:, D_FF:] * ce
    acc_scr[...] += _dot(act.astype(BF16), wd_ref[0])

    @pl.when(e == N_EXPERTS - 1)
    def _():
        y = x_ref[...] + acc_scr[...]
        o_ref[...] = _rms(y, gf_ref[...]) if final else y


def _moe(x, g, wr, br, wgu, wd, gf, tm, final=False):
    n = x.shape[0]
    return pl.pallas_call(
        functools.partial(_moe_body, final=final),
        grid=(n // tm, N_EXPERTS),
        in_specs=[
            pl.BlockSpec((tm, D_MODEL), lambda i, e: (i, 0)),
            _full(g.shape), _full(wr.shape), _full(br.shape),
            pl.BlockSpec((1, D_MODEL, 2 * D_FF), lambda i, e: (e, 0, 0)),
            pl.BlockSpec((1, D_FF, D_MODEL), lambda i, e: (e, 0, 0)),
            _full(gf.shape),
        ],
        out_specs=pl.BlockSpec((tm, D_MODEL), lambda i, e: (i, 0)),
        out_shape=jax.ShapeDtypeStruct((n, D_MODEL), F32),
        scratch_shapes=[pltpu.VMEM((tm, D_MODEL), BF16), pltpu.VMEM((tm, LANES), F32),
                        pltpu.VMEM((tm, D_MODEL), F32)],
        compiler_params=_cparams(("parallel", "arbitrary")),
        name="moe",
    )(x, g, wr, br, wgu, wd, gf)


def _moe_prepare(w_rg, b_rg, w_re, b_re, w_gate, w_up, w_down):
    wr = jnp.zeros((D_MODEL, LANES), F32)
    wr = wr.at[:, :N_GROUPS].set(w_rg).at[:, N_GROUPS:N_GROUPS + N_EXPERTS].set(w_re)
    hi = wr.astype(BF16)
    r1 = wr - hi.astype(F32)
    mid = r1.astype(BF16)
    lo = (r1 - mid.astype(F32)).astype(BF16)
    br = jnp.zeros((1, LANES), F32)
    br = br.at[0, :N_GROUPS].set(b_rg).at[0, N_GROUPS:N_GROUPS + N_EXPERTS].set(b_re)
    wgu = jnp.concatenate([w_gate, w_up], axis=-1).astype(BF16)
    return jnp.stack([hi, mid, lo]), br, wgu, w_down.astype(BF16)


CMP_SUB = 256
CMP_W = 4 * HEAD_DIM
PAGES_PER_STEP = 32


def _cmp_perm():
    p = np.zeros((CMP_SUB, CMP_SUB), np.float32)
    for i in range(CMP_SUB):
        p[i, CMP_BLOCK * (i % SUBLANES) + i // SUBLANES] = 1.0
    return p


def _cmp_prepare(pe, w1, w2):
    eye4 = jnp.eye(4, dtype=F32)
    sel = jnp.array([0, 0, 1, 1])
    w1r = w1.reshape(2, CMP_BLOCK, HEAD_DIM, CMP_HIDDEN)[sel]
    w1bd = (w1r.transpose(1, 0, 2, 3)[:, :, :, None, :] * eye4[None, :, None, :, None])
    w1bd = w1bd.reshape(CMP_BLOCK, CMP_W, 4 * CMP_HIDDEN).astype(BF16)
    w2r = w2[sel]
    w2bd = (w2r[:, :, None, :] * eye4[:, None, :, None]).reshape(4 * CMP_HIDDEN, CMP_W).astype(BF16)
    pe4 = pe[sel].transpose(1, 0, 2).reshape(CMP_BLOCK, CMP_W)
    pe_exp = jnp.repeat(pe4, SUBLANES, axis=0)
    return jnp.asarray(_cmp_perm(), BF16), pe_exp, w1bd, w2bd


def _compress_rows(get_sub, nsub, perm_ref, pe_ref, w1_ref, w2_ref, stage):
    for t in range(nsub):
        xp = _dot(perm_ref[...], get_sub(t).astype(BF16))
        stage[t] = xp + pe_ref[...]
    acc = jnp.zeros((nsub * SUBLANES, 4 * CMP_HIDDEN), F32)
    for r in range(CMP_BLOCK):
        a = stage[:, r * SUBLANES:(r + 1) * SUBLANES, :].reshape(nsub * SUBLANES, CMP_W)
        acc = acc + _dot(a.astype(BF16), w1_ref[r])
    return _dot(_gelu_tanh(acc).astype(BF16), w2_ref[...])


def _compress_seq_body(x_ref, perm_ref, pe_ref, w1_ref, w2_ref, o_ref, stage, *, nsub):
    o_ref[...] = _compress_rows(lambda t: x_ref[t * CMP_SUB:(t + 1) * CMP_SUB, :], nsub,
                                perm_ref, pe_ref, w1_ref, w2_ref, stage)


def _compress_seq(kvp, cw, bsz, seq):
    nsub = seq // CMP_SUB
    nblk = seq // CMP_BLOCK
    return pl.pallas_call(
        functools.partial(_compress_seq_body, nsub=nsub),
        grid=(bsz,),
        in_specs=[pl.BlockSpec((seq, CMP_W), lambda b: (b, 0))] + [_full(c.shape) for c in cw],
        out_specs=pl.BlockSpec((None, nblk, CMP_W), lambda b: (b, 0, 0)),
        out_shape=jax.ShapeDtypeStruct((bsz, nblk, CMP_W), F32),
        scratch_shapes=[pltpu.VMEM((nsub, CMP_SUB, CMP_W), F32)],
        compiler_params=_cparams(("parallel",)),
        name="compress_seq",
    )(kvp, *cw)


def _compress_pages_body(pt_ref, *refs):
    pages = refs[:PAGES_PER_STEP]
    perm_ref, pe_ref, w1_ref, w2_ref, o_ref, stage = refs[PAGES_PER_STEP:]
    per = CMP_SUB // PAGE

    def get_sub(t):
        return jnp.concatenate([pages[per * t + k][...] for k in range(per)], axis=0)

    o_ref[...] = _compress_rows(get_sub, PAGES_PER_STEP // per, perm_ref, pe_ref, w1_ref, w2_ref, stage)


def _compress_pages(cache, page_table, layer, cw):
    bsz, n_pages = page_table.shape
    steps = n_pages // PAGES_PER_STEP
    nsub = PAGES_PER_STEP * PAGE // CMP_SUB
    blk_per_step = PAGES_PER_STEP * PAGE // CMP_BLOCK

    def page_spec(k):
        return pl.BlockSpec((None, None, PAGE, CMP_W),
                            lambda b, i, pt: (layer, pt[b * n_pages + i * PAGES_PER_STEP + k], 0, 0))

    grid_spec = pltpu.PrefetchScalarGridSpec(
        num_scalar_prefetch=1,
        grid=(bsz, steps),
        in_specs=[page_spec(k) for k in range(PAGES_PER_STEP)]
        + [pl.BlockSpec(c.shape, lambda b, i, pt, nd=c.ndim: (0,) * nd) for c in cw],
        out_specs=pl.BlockSpec((None, blk_per_step, CMP_W), lambda b, i, pt: (b, i, 0)),
        scratch_shapes=[pltpu.VMEM((nsub, CMP_SUB, CMP_W), F32)],
    )
    return pl.pallas_call(
        _compress_pages_body,
        grid_spec=grid_spec,
        out_shape=jax.ShapeDtypeStruct((bsz, n_pages * PAGE // CMP_BLOCK, CMP_W), F32),
        compiler_params=_cparams(("parallel", "arbitrary")),
        name="compress_pages",
    )(page_table.reshape(-1), *([cache] * PAGES_PER_STEP), *cw)


TQ = 128
QL = GRP * TQ
N_SELBLK_SEQ = 32
AUG_MASK = HEAD_DIM
AUG_POS = HEAD_DIM + 32
POS_SPLIT = 128


def _alibi_slopes():
    return [2.0 ** (-8.0 * (h + 1) / N_HEADS) for h in range(N_HEADS)]


def _nsa_tables(seq):
    pos = np.arange(seq)
    tbl = np.zeros((seq, HEAD_DIM), np.float32)
    tbl[pos, pos // SEL_BLOCK] = 1.0
    tbl[:, 32] = (pos // POS_SPLIT) * POS_SPLIT
    tbl[:, 33] = pos % POS_SPLIT
    ncmp = seq // CMP_BLOCK
    order = np.concatenate([np.arange(0, ncmp, 2), np.arange(1, ncmp, 2)])
    cend = (order + 1) * CMP_BLOCK - 1
    ctbl = np.zeros((ncmp, HEAD_DIM), np.float32)
    ctbl[:, 32] = (cend // POS_SPLIT) * POS_SPLIT
    ctbl[:, 33] = cend % POS_SPLIT
    pm = np.zeros((ncmp, ncmp), np.float32)
    pm[np.arange(ncmp), order] = 1.0
    k = np.arange(TQ)[:, None]
    l = np.arange(TQ)[None, :]
    causal = np.where(l >= k, 0.0, NEG).astype(np.float32)
    far = np.where(l < k, 0.0, NEG).astype(np.float32)
    bias = np.stack([np.tile(causal, (1, GRP)), np.tile(far, (1, GRP))])
    slope = np.zeros((KV_HEADS, 16, QL), np.float32)
    sl = _alibi_slopes()
    for g in range(KV_HEADS):
        for m in range(GRP):
            slope[g, 0:2, m * TQ:(m + 1) * TQ] = sl[g * GRP + m]
    return (jnp.asarray(tbl), jnp.asarray(ctbl), jnp.asarray(cend.reshape(ncmp, 1).astype(np.int32)),
            jnp.asarray(pm, BF16), jnp.asarray(bias), jnp.asarray(slope, BF16))


def _flash_chunk(kaug, vt, qa, bias, state):
    m, l, acc = state
    s = _dot(kaug, qa)
    if bias is not None:
        s = s + bias
    mn = jnp.maximum(m, jnp.max(s, axis=0, keepdims=True))
    alpha = jnp.exp(m - mn)
    p = jnp.exp(s - mn)
    l = alpha * l + jnp.sum(p, axis=0, keepdims=True)
    acc = alpha * acc + _dot(vt, p.astype(BF16))
    return mn, l, acc


def _flash_init():
    return (jnp.full((1, QL), M_INIT, F32), jnp.zeros((1, QL), F32), jnp.zeros((HEAD_DIM, QL), F32))


def _nsa_seq_body(q_ref, kvp_ref, kvw_ref, gt_ref, kc_ref, tbl_ref, ctbl_ref, cend_ref, pm_ref, bias_ref,
                  slope_ref, y_ref, kslc, kwin, vslc, vwin, qaug):
    i = pl.program_id(1)
    l0 = i * TQ
    q = q_ref[...]
    kvp = kvp_ref[...]
    kvw = kvw_ref[...]
    tbl = tbl_ref[...]
    lane64 = lax.broadcasted_iota(jnp.int32, (TQ, HEAD_DIM), 1)
    tblw = jnp.where(lane64 < 32, 0.0, tbl)
    for g in range(KV_HEADS):
        ks = kvp[:, 2 * KV_DIM + g * HEAD_DIM:2 * KV_DIM + (g + 1) * HEAD_DIM]
        kslc[g, i] = jnp.concatenate([ks, tbl], axis=1).astype(BF16)
        kw = kvw[:, g * HEAD_DIM:(g + 1) * HEAD_DIM]
        kwin[g, i] = jnp.concatenate([kw, tblw], axis=1).astype(BF16)
    vslc[i] = kvp[:, 3 * KV_DIM:4 * KV_DIM].T.astype(BF16)
    vwin[i] = kvw[:, KV_DIM:2 * KV_DIM].T.astype(BF16)

    scale = HEAD_DIM ** -0.5
    qt = [(q[:, j * LANES:(j + 1) * LANES] * scale).T for j in range(ATTN_DIM // LANES)]
    for g in range(KV_HEADS):
        heads = []
        for m in range(GRP):
            h = g * GRP + m
            heads.append(qt[h // 2][(h % 2) * HEAD_DIM:(h % 2 + 1) * HEAD_DIM, :])
        qaug[g, 0:HEAD_DIM, :] = jnp.concatenate(heads, axis=1).astype(BF16)
        qaug[g, AUG_MASK:AUG_POS, :] = jnp.zeros((AUG_POS - AUG_MASK, QL), BF16)
        qaug[g, AUG_POS:AUG_POS + 16, :] = slope_ref[g]
        qaug[g, AUG_POS + 16:, :] = jnp.zeros((2 * HEAD_DIM - AUG_POS - 16, QL), BF16)

    ncmp = kc_ref.shape[0]
    nsel = ncmp // 2
    kcp = _dot(pm_ref[...], kc_ref[...].astype(BF16))
    vct = jnp.concatenate([kcp[:, 2 * KV_DIM - KV_DIM:2 * KV_DIM],
                           jnp.zeros((LANES - ncmp, KV_DIM), F32)], axis=0).T
    lpos = l0 + lax.broadcasted_iota(jnp.int32, (1, QL), 1) % TQ
    valid_c = cend_ref[...] <= lpos
    lq = l0 + lax.broadcasted_iota(jnp.int32, (nsel, TQ), 1)
    blk = lax.broadcasted_iota(jnp.int32, (nsel, TQ), 0)
    cur = lq // SEL_BLOCK
    forced = (blk == 0) | (blk == cur) | (blk == cur - 1)
    causal_blk = blk * SEL_BLOCK <= lq
    o_cmp = []
    for g in range(KV_HEADS):
        kca = jnp.concatenate([kcp[:, g * HEAD_DIM:(g + 1) * HEAD_DIM], ctbl_ref[...]], axis=1).astype(BF16)
        s = jnp.where(valid_c, _dot(kca, qaug[g]), NEG)
        mx = jnp.max(s, axis=0, keepdims=True)
        e = jnp.where(valid_c, jnp.exp(s - mx), 0.0)
        p = e * (1.0 / jnp.maximum(jnp.sum(e, axis=0, keepdims=True), 1.0))
        vt_g = vct[g * HEAD_DIM:(g + 1) * HEAD_DIM, 0:ncmp].astype(BF16)
        o_cmp.append(_dot(vt_g, p.astype(BF16)))
        psum = p[:, 0:TQ]
        for m in range(1, GRP):
            psum = psum + p[:, m * TQ:(m + 1) * TQ]
        imp = psum[0:nsel, :] + psum[nsel:, :]
        score = jnp.where(causal_blk, imp + jnp.where(forced, FORCE_BONUS, 0.0), NEG)
        sel = jnp.zeros((nsel, TQ), jnp.bool_)
        for _ in range(N_SELECT):
            best = jnp.max(score, axis=0, keepdims=True)
            idx = jnp.min(jnp.where(score == best, blk, nsel), axis=0, keepdims=True)
            hit = (blk == idx) & (best > 0.5 * NEG)
            sel = sel | hit
            score = jnp.where(hit, 2.0 * NEG, score)
        mb = jnp.where(sel, 0.0, NEG)
        qaug[g, AUG_MASK:AUG_POS, :] = jnp.concatenate([mb] * GRP, axis=1).astype(BF16)

    def slc_step(c, st):
        return tuple(_flash_chunk(kslc[g, c], vslc[c, g * HEAD_DIM:(g + 1) * HEAD_DIM, :], qaug[g], None, st[g])
                     for g in range(KV_HEADS))

    st = lax.fori_loop(0, i, slc_step, tuple(_flash_init() for _ in range(KV_HEADS)))
    cbias = bias_ref[0]
    o_slc = []
    for g in range(KV_HEADS):
        m, l, acc = _flash_chunk(kslc[g, i], vslc[i, g * HEAD_DIM:(g + 1) * HEAD_DIM, :], qaug[g], cbias, st[g])
        o_slc.append(acc * (1.0 / l))

    def win_chunk(c, st, bias):
        return tuple(_flash_chunk(kwin[g, c], vwin[c, g * HEAD_DIM:(g + 1) * HEAD_DIM, :], qaug[g], bias, st[g])
                     for g in range(KV_HEADS))

    nfull = WINDOW // TQ - 1
    st = tuple(_flash_init() for _ in range(KV_HEADS))
    st = lax.cond(i > nfull, lambda s: win_chunk(i - nfull - 1, s, bias_ref[1]), lambda s: s, st)
    st = lax.fori_loop(jnp.maximum(i - nfull, 0), i, lambda c, s: win_chunk(c, s, None), st)
    o_win = []
    for g in range(KV_HEADS):
        m, l, acc = _flash_chunk(kwin[g, i], vwin[i, g * HEAD_DIM:(g + 1) * HEAD_DIM, :], qaug[g], cbias, st[g])
        o_win.append(acc * (1.0 / l))

    sg = _sigmoid(gt_ref[...].T[0:3 * N_HEADS, :])
    outs = []
    for g in range(KV_HEADS):
        def gate_row(br):
            return jnp.concatenate([sg[br * N_HEADS + g * GRP + m:br * N_HEADS + g * GRP + m + 1, :]
                                    for m in range(GRP)], axis=1)
        outs.append(gate_row(0) * o_cmp[g] + gate_row(1) * o_slc[g] + gate_row(2) * o_win[g])
    for j in range(ATTN_DIM // LANES):
        rows = []
        for h in (2 * j, 2 * j + 1):
            g, m = divmod(h, GRP)
            rows.append(outs[g][:, m * TQ:(m + 1) * TQ])
        y_ref[:, j * LANES:(j + 1) * LANES] = jnp.concatenate(rows, axis=0).T


def _nsa_seq(q, kvp, kvw, gt, kc, bsz, seq):
    nt = seq // TQ
    tbl, ctbl, cend, pm, bias, slope = _nsa_tables(seq)
    ncmp = seq // CMP_BLOCK
    row = lambda w: pl.BlockSpec((TQ, w), lambda b, i: (b * nt + i, 0))
    return pl.pallas_call(
        _nsa_seq_body,
        grid=(bsz, nt),
        in_specs=[row(ATTN_DIM), row(4 * KV_DIM), row(2 * KV_DIM), row(GATE_PAD),
                  pl.BlockSpec((None, ncmp, CMP_W), lambda b, i: (b, 0, 0)),
                  pl.BlockSpec((TQ, HEAD_DIM), lambda b, i: (i, 0)),
                  _full(ctbl.shape), _full(cend.shape), _full(pm.shape), _full(bias.shape), _full(slope.shape)],
        out_specs=row(ATTN_DIM),
        out_shape=jax.ShapeDtypeStruct((bsz * seq, ATTN_DIM), F32),
        scratch_shapes=[pltpu.VMEM((KV_HEADS, nt, TQ, 2 * HEAD_DIM), BF16),
                        pltpu.VMEM((KV_HEADS, nt, TQ, 2 * HEAD_DIM), BF16),
                        pltpu.VMEM((nt, KV_DIM, TQ), BF16),
                        pltpu.VMEM((nt, KV_DIM, TQ), BF16),
                        pltpu.VMEM((KV_HEADS, 2 * HEAD_DIM, QL), BF16)],
        compiler_params=_cparams(("parallel", "arbitrary")),
        name="nsa_seq",
    )(q, kvp, kvw, gt, kc, tbl, ctbl, cend, pm, bias, slope)


N_PICK = N_SELECT - 1


def _q_rows(q, g, scale):
    rows = [q[:, (g * GRP + m) * HEAD_DIM:(g * GRP + m + 1) * HEAD_DIM] for m in range(GRP)]
    return jnp.concatenate(rows + [jnp.zeros((SUBLANES - GRP, HEAD_DIM), F32)], axis=0) * scale


def _slope_col(g):
    sl = _alibi_slopes()
    row = lax.broadcasted_iota(jnp.int32, (SUBLANES, 1), 0)
    col = jnp.zeros((SUBLANES, 1), F32)
    for m in range(GRP):
        col = jnp.where(row == m, sl[g * GRP + m], col)
    return col


def _nsa_step_cmp_body(q_ref, kc_ref, pair_ref, o_ref, idx_ref, *, past_len):
    q = q_ref[...]
    kc = kc_ref[...]
    ncmp = kc.shape[0]
    nsel = ncmp // 2
    cur = past_len // SEL_BLOCK
    scale = HEAD_DIM ** -0.5
    cend = (lax.broadcasted_iota(jnp.int32, (1, ncmp), 1) + 1) * CMP_BLOCK - 1
    dist = (past_len - cend).astype(F32)
    lane = lax.broadcasted_iota(jnp.int32, (1, nsel), 1)
    forced = (lane == 0) | (lane == cur) | (lane == cur - 1)
    lane_o = lax.broadcasted_iota(jnp.int32, (1, LANES), 1)
    idx_out = jnp.zeros((1, LANES), jnp.int32)
    for g in range(KV_HEADS):
        q8 = _q_rows(q, g, scale).astype(BF16)
        s = _dot_nt(q8, kc[:, g * HEAD_DIM:(g + 1) * HEAD_DIM].astype(BF16)) - _slope_col(g) * dist
        mx = jnp.max(s, axis=1, keepdims=True)
        e = jnp.exp(s - mx)
        p = e / jnp.maximum(jnp.sum(e, axis=1, keepdims=True), 1.0)
        vg = kc[:, KV_DIM + g * HEAD_DIM:KV_DIM + (g + 1) * HEAD_DIM]
        o_ref[g] = _dot(p.astype(BF16), vg.astype(BF16))
        psum = jnp.sum(p[0:GRP, :], axis=0, keepdims=True)
        p_hi, p_mid, p_lo = _split3(jnp.broadcast_to(psum, (SUBLANES, ncmp)))
        pair = pair_ref[...]
        imp = (_dot(p_hi, pair) + _dot(p_mid, pair) + _dot(p_lo, pair))[0:1, :]
        score = imp + jnp.where(forced, FORCE_BONUS, 0.0)
        for t in range(N_PICK):
            best = jnp.max(score, axis=1, keepdims=True)
            idx = jnp.min(jnp.where(score == best, lane, nsel), axis=1, keepdims=True)
            score = jnp.where(lane == idx, NEG, score)
            idx_out = jnp.where(lane_o == g * N_SELECT + t, idx, idx_out)
    idx_ref[...] = idx_out


def _nsa_step_attn_body(sel_ref, pt_ref, *refs, past_len):
    nb = KV_HEADS * N_PICK
    blocks = refs[:nb]
    win_ref, q_ref, gt_ref, kvp_ref, kvw_ref, ocmp_ref, y_ref = refs[nb:]
    b = pl.program_id(0)
    q = q_ref[...]
    kvp = kvp_ref[...]
    kvw = kvw_ref[...]
    sg = _sigmoid(gt_ref[...])
    w = win_ref[...]
    nwin = w.shape[0]
    scale = HEAD_DIM ** -0.5
    iot = lax.broadcasted_iota(jnp.int32, (1, SEL_BLOCK), 1)
    wl = lax.broadcasted_iota(jnp.int32, (1, nwin), 1)
    for g in range(KV_HEADS):
        q8 = _q_rows(q, g, scale)
        q8b = q8.astype(BF16)
        sc = _slope_col(g)
        hs = slice(g * HEAD_DIM, (g + 1) * HEAD_DIM)
        k_new = kvp[:, 2 * KV_DIM + g * HEAD_DIM:2 * KV_DIM + (g + 1) * HEAD_DIM]
        v_new = kvp[:, 3 * KV_DIM + g * HEAD_DIM:3 * KV_DIM + (g + 1) * HEAD_DIM]
        s_cur = jnp.sum(q8 * k_new, axis=1, keepdims=True)
        ss, vs = [], []
        for t in range(N_PICK):
            blk = blocks[g * N_PICK + t][...]
            kpos = sel_ref[b * KV_HEADS * N_SELECT + g * N_SELECT + t] * SEL_BLOCK + iot
            ss.append(_dot_nt(q8b, blk[:, hs].astype(BF16)) - sc * (past_len - kpos).astype(F32))
            vs.append(blk[:, KV_DIM + g * HEAD_DIM:KV_DIM + (g + 1) * HEAD_DIM].astype(BF16))
        mx = functools.reduce(jnp.maximum, [jnp.max(s, axis=1, keepdims=True) for s in ss] + [s_cur])
        es = [jnp.exp(s - mx) for s in ss]
        e_cur = jnp.exp(s_cur - mx)
        den = functools.reduce(lambda a, c: a + c, [jnp.sum(e, axis=1, keepdims=True) for e in es] + [e_cur])
        acc = e_cur * v_new
        for e, v in zip(es, vs):
            acc = acc + _dot(e.astype(BF16), v)
        o_slc = acc / den
        kw_new = kvw[:, g * HEAD_DIM:(g + 1) * HEAD_DIM]
        vw_new = kvw[:, KV_DIM + g * HEAD_DIM:KV_DIM + (g + 1) * HEAD_DIM]
        dw = (nwin - wl).astype(F32)
        s_w = jnp.where(nwin - wl < WINDOW, _dot_nt(q8b, w[:, hs].astype(BF16)) - sc * dw, NEG)
        s_wc = jnp.sum(q8 * kw_new, axis=1, keepdims=True)
        mw = jnp.maximum(jnp.max(s_w, axis=1, keepdims=True), s_wc)
        e_w = jnp.exp(s_w - mw)
        e_wc = jnp.exp(s_wc - mw)
        o_win = (_dot(e_w.astype(BF16), w[:, KV_DIM + g * HEAD_DIM:KV_DIM + (g + 1) * HEAD_DIM].astype(BF16))
                 + e_wc * vw_new) / (jnp.sum(e_w, axis=1, keepdims=True) + e_wc)

        def gate_col(br):
            cols = [sg[:, br * N_HEADS + g * GRP + m:br * N_HEADS + g * GRP + m + 1] for m in range(GRP)]
            return jnp.concatenate(cols + [jnp.zeros((SUBLANES - GRP, 1), F32)], axis=0)

        y_ref[g] = gate_col(0) * ocmp_ref[g] + gate_col(1) * o_slc + gate_col(2) * o_win


def _nsa_step(q, kvp, kvw, gt, kc, cache, page_table, cache_win, layer):
    n, n_pages = page_table.shape
    past_len = n_pages * PAGE
    ncmp = kc.shape[1]
    nsel = ncmp // 2
    pair = np.zeros((ncmp, nsel), np.float32)
    pair[np.arange(ncmp), np.arange(ncmp) // 2] = 1.0
    r3 = lambda a: a.reshape(n, 1, a.shape[-1])
    row3 = lambda w: pl.BlockSpec((None, 1, w), lambda b, *_: (b, 0, 0))
    o_cmp, idx = pl.pallas_call(
        functools.partial(_nsa_step_cmp_body, past_len=past_len),
        grid=(n,),
        in_specs=[row3(ATTN_DIM), pl.BlockSpec((None, ncmp, CMP_W), lambda b: (b, 0, 0)), _full(pair.shape)],
        out_specs=[pl.BlockSpec((None, KV_HEADS, SUBLANES, HEAD_DIM), lambda b: (b, 0, 0, 0)), row3(LANES)],
        out_shape=[jax.ShapeDtypeStruct((n, KV_HEADS, SUBLANES, HEAD_DIM), F32),
                   jax.ShapeDtypeStruct((n, 1, LANES), jnp.int32)],
        compiler_params=_cparams(("parallel",)),
        name="nsa_step_cmp",
    )(r3(q), kc, jnp.asarray(pair, BF16))
    sel = idx[:, 0, :KV_HEADS * N_SELECT].reshape(-1)

    def blk_spec(j):
        g, t = divmod(j, N_PICK)

        def imap(b, sel_ref, pt_ref):
            s = sel_ref[b * KV_HEADS * N_SELECT + g * N_SELECT + t]
            return (layer, pt_ref[b * n_pages + s // (PAGE // SEL_BLOCK)], s % (PAGE // SEL_BLOCK), 1)

        return pl.BlockSpec((None, None, SEL_BLOCK, 2 * KV_DIM), imap)

    nb = KV_HEADS * N_PICK
    grid_spec = pltpu.PrefetchScalarGridSpec(
        num_scalar_prefetch=2,
        grid=(n,),
        in_specs=[blk_spec(j) for j in range(nb)]
        + [pl.BlockSpec((None, None, cache_win.shape[2], CMP_W), lambda b, *_: (layer, b, 0, 0)),
           row3(ATTN_DIM), row3(GATE_PAD), row3(4 * KV_DIM), row3(2 * KV_DIM),
           pl.BlockSpec((None, KV_HEADS, SUBLANES, HEAD_DIM), lambda b, *_: (b, 0, 0, 0))],
        out_specs=pl.BlockSpec((None, KV_HEADS, SUBLANES, HEAD_DIM), lambda b, *_: (b, 0, 0, 0)),
    )
    y = pl.pallas_call(
        functools.partial(_nsa_step_attn_body, past_len=past_len),
        grid_spec=grid_spec,
        out_shape=jax.ShapeDtypeStruct((n, KV_HEADS, SUBLANES, HEAD_DIM), F32),
        compiler_params=_cparams(("arbitrary",)),
        name="nsa_step_attn",
    )(sel, page_table.reshape(-1), *([cache] * nb), cache_win, r3(q), r3(gt), r3(kvp), r3(kvw), o_cmp)
    return y[:, :, :GRP, :].reshape(n, ATTN_DIM)


def kernel(x_prompt, x_sample, cache_kv, page_table, cache_win, state_ssm, state_conv, norm_attn_g, w_in,
           ssm_a_re, ssm_a_im, ssm_log_dt, ssm_b_re, ssm_b_im, ssm_c_re, ssm_c_im, ssm_d, ssm_w_glu, ssm_b_glu,
           cmp_pe, cmp_w1, cmp_w2, conv_w, w_br_ssm, w_br_attn, w_br_conv, w_out, norm_ffn_g,
           w_router_group, b_router_group, w_router_expert, b_router_expert, moe_w_gate, moe_w_up, moe_w_down,
           norm_final_g):
    bp, lp, _ = x_prompt.shape
    bs = x_sample.shape[0]
    depth = w_in.shape[0]
    n_pool = cache_kv.shape[1]
    nwin = cache_win.shape[2]
    xp = x_prompt.reshape(bp * lp, D_MODEL)
    xs = x_sample.reshape(bs, D_MODEL)
    cache = cache_kv.reshape(depth, n_pool, PAGE, 4 * KV_DIM)
    cwin = cache_win.reshape(depth, bs, nwin, 2 * KV_DIM)
    gf = norm_final_g.reshape(1, D_MODEL)
    kv_p, kv_s, win_p, win_s, ssm_p, ssm_s, conv_p, conv_s = ([] for _ in range(8))

    def ssm_state(hre, him, n):
        return jnp.stack([hre.reshape(n, SSM_GROUPS, SSM_STATE), him.reshape(n, SSM_GROUPS, SSM_STATE)], axis=-1)

    for l in range(depth):
        w = w_in[l]
        w1, w1_lo = _hilo(jnp.concatenate(
            [w[:, :C_GATE], jnp.pad(w[:, C_GATE:C_CONV], ((0, 0), (0, GATE_PAD - 3 * N_HEADS)))], axis=1))
        g_attn = norm_attn_g[l].reshape(1, D_MODEL)
        g_ffn = norm_ffn_g[l].reshape(1, D_MODEL)
        s5w = _s5_prepare(ssm_a_re[l], ssm_a_im[l], ssm_log_dt[l], ssm_b_re[l], ssm_b_im[l],
                          ssm_c_re[l], ssm_c_im[l], ssm_d[l], ssm_w_glu[l], ssm_b_glu[l])
        cw = _cmp_prepare(cmp_pe[l], cmp_w1[l], cmp_w2[l])
        mw = _merge_weights(w[:, C_CONV:], w_br_ssm[l], w_br_attn[l], w_br_conv[l], w_out[l], conv_w[l])
        wr, br, wgu, wd = _moe_prepare(w_router_group[l], b_router_group[l], w_router_expert[l], b_router_expert[l],
                                       moe_w_gate[l], moe_w_up[l], moe_w_down[l])
        final = l == depth - 1

        u, q, kvp, kvw, gt = _inproj(xp, g_attn, w1, 512)
        y_ssm, hre, him, pre_r, pre_i = _s5_seq(u, s5w, bp, lp)
        kc = _compress_seq(kvp, cw, bp, lp)
        y_attn = _nsa_seq(q, kvp, kvw, gt, kc, bp, lp)
        x1, cl, cpre = _merge_seq(xp, g_attn, y_ssm, y_attn, mw, bp, lp)
        x2 = _moe(x1, g_ffn, wr, br, wgu, wd, gf, 1024, final=final)
        if not final and lp >= 2 * TAIL:
            tail = lambda a: a.reshape(bp, lp, a.shape[-1])[:, lp - TAIL:].reshape(bp * TAIL, a.shape[-1])
            xt = tail(xp)
            ut = _inproj(xt, g_attn, w1, TAIL, w1_lo)[0]
            yst = _s5_seq(ut, s5w, bp, TAIL, tl=TAIL, precise=True, h0=(pre_r, pre_i))[0]
            x1t = _merge_seq(xt, g_attn, yst, tail(y_attn), mw, bp, TAIL, tm=TAIL, precise=True, cin=cpre)[0]
            x2t = _moe(x1t, g_ffn, wr, br, wgu, wd, gf, min(1024, bp * TAIL), final=False)
            x2 = x2.reshape(bp, lp, D_MODEL).at[:, lp - TAIL:].set(x2t.reshape(bp, TAIL, D_MODEL))
            x2 = x2.reshape(bp * lp, D_MODEL)
        xp = x2
        kv_p.append(kvp.reshape(bp, lp, 4, KV_HEADS, HEAD_DIM))
        win_p.append(kvw.reshape(bp, lp, 2, KV_HEADS, HEAD_DIM)[:, lp - min(WINDOW, lp):])
        ssm_p.append(ssm_state(hre[:, 0], him[:, 0], bp))
        conv_p.append(cl[:, SUBLANES - 2:])

        u, q, kvp, kvw, gt = _inproj(xs, g_attn, w1, bs, w1_lo)
        st = state_ssm[l]
        y_ssm, hre, him = _s5_step(u, st[..., 0].reshape(bs, SSM_N), st[..., 1].reshape(bs, SSM_N), s5w)
        kc = _compress_pages(cache, page_table, l, cw)
        y_attn = _nsa_step(q, kvp, kvw, gt, kc, cache, page_table, cwin, l)
        prev = state_conv[l]
        x1, uc = _merge_step(xs, g_attn, y_ssm, y_attn, prev[:, 0], prev[:, 1], mw)
        xs = _moe(x1, g_ffn, wr, br, wgu, wd, gf, bs, final=final)
        kv_s.append(kvp.reshape(bs, 1, 4, KV_HEADS, HEAD_DIM))
        win_s.append(jnp.concatenate([cache_win[l][:, 1:], kvw.reshape(bs, 1, 2, KV_HEADS, HEAD_DIM)], axis=1))
        ssm_s.append(ssm_state(hre, him, bs))
        conv_s.append(jnp.stack([prev[:, 1], uc], axis=1))

    return (xp.reshape(bp, lp, D_MODEL), xs.reshape(bs, 1, D_MODEL),
            jnp.stack(kv_p), jnp.stack(kv_s), jnp.stack(win_p), jnp.stack(win_s),
            jnp.stack(ssm_p), jnp.stack(ssm_s), jnp.stack(conv_p), jnp.stack(conv_s))
```

```python
import functools
import math

import numpy as np
import jax
import jax.numpy as jnp
from jax import lax
from jax.experimental import pallas as pl
from jax.experimental.pallas import tpu as pltpu

F32 = jnp.float32
BF16 = jnp.bfloat16

D_MODEL = 1024
DEPTH = 2
PAGE = 128
SSM_GROUPS = 24
SSM_CH = 16
SSM_DIM = SSM_GROUPS * SSM_CH
SSM_STATE = 64
SSM_N = SSM_GROUPS * SSM_STATE
N_HEADS = 8
HEAD_DIM = 64
KV_HEADS = 2
GRP = N_HEADS // KV_HEADS
ATTN_DIM = N_HEADS * HEAD_DIM
KV_DIM = KV_HEADS * HEAD_DIM
CMP_BLOCK = 32
CMP_HIDDEN = 64
SEL_BLOCK = 64
N_SELECT = 8
WINDOW = 512
FORCE_BONUS = 1e4
CONV_DIM = 384
N_GROUPS = 4
EPG = 4
N_EXPERTS = 16
D_FF = 256
RMS_EPS = 1e-6

C_U = 0
C_Q = C_U + SSM_DIM
C_KVP = C_Q + ATTN_DIM
C_KVW = C_KVP + 4 * KV_DIM
C_GATE = C_KVW + 2 * KV_DIM
C_CONV = C_GATE + 3 * N_HEADS
C_MERGE = C_CONV + 3 * CONV_DIM
N_IN = C_MERGE + 3 * D_MODEL
W2_COLS = N_IN - C_CONV
GATE_PAD = 128
W1_COLS = C_GATE + GATE_PAD

LANES = 128
SUBLANES = 8
NEG = -1e30
M_INIT = -1e29
VMEM_LIMIT = 56 * 1024 * 1024


def _cparams(sem):
    return pltpu.CompilerParams(dimension_semantics=sem, vmem_limit_bytes=VMEM_LIMIT)


def _rms(x, g):
    ms = jnp.mean(x * x, axis=-1, keepdims=True)
    return x * lax.rsqrt(ms + RMS_EPS) * g


def _gelu_tanh(x):
    return 0.5 * x * (1.0 + jnp.tanh(math.sqrt(2.0 / math.pi) * (x + 0.044715 * (x * x * x))))


def _sigmoid(x):
    return 1.0 / (1.0 + jnp.exp(-x))


def _dot(a, b):
    return jnp.dot(a, b, preferred_element_type=F32)


def _dot_nt(a, b):
    return lax.dot_general(a, b, (((1,), (1,)), ((), ())), preferred_element_type=F32)


def _split3(x):
    hi = x.astype(BF16)
    r1 = x - hi.astype(F32)
    mid = r1.astype(BF16)
    lo = (r1 - mid.astype(F32)).astype(BF16)
    return hi, mid, lo


def _mm(a, w_ref, wlo_ref=None):
    ah = a.astype(BF16)
    if wlo_ref is None:
        return _dot(ah, w_ref[...])
    al = (a - ah.astype(F32)).astype(BF16)
    return _dot(ah, w_ref[...]) + (_dot(al, w_ref[...]) + _dot(ah, wlo_ref[...]))


def _hilo(w):
    hi = w.astype(BF16)
    return hi, (w - hi.astype(F32)).astype(BF16)


def _full(shape):
    nd = len(shape)
    return pl.BlockSpec(shape, lambda *_: (0,) * nd)


def _full1(shape):
    nd = len(shape)
    return pl.BlockSpec(shape, lambda *_: (0,) * nd, pipeline_mode=pl.Buffered(1))


def _inproj_body(x_ref, g_ref, w_ref, *refs, precise):
    wlo_ref = refs[0] if precise else None
    u_ref, q_ref, kvp_ref, kvw_ref, gt_ref = refs[1:] if precise else refs
    h = _rms(x_ref[...], g_ref[...])

    def proj(a, b):
        return _mm(h, w_ref.at[:, a:b], wlo_ref.at[:, a:b] if precise else None)

    u_ref[...] = proj(C_U, C_Q)
    q_ref[...] = proj(C_Q, C_KVP)
    kvp_ref[...] = proj(C_KVP, C_KVW)
    kvw_ref[...] = proj(C_KVW, C_GATE)
    gt_ref[...] = proj(C_GATE, W1_COLS)


def _inproj(x, g, w1, tm, w1_lo=None):
    n = x.shape[0]
    widths = (SSM_DIM, ATTN_DIM, 4 * KV_DIM, 2 * KV_DIM, GATE_PAD)
    ws = [w1] if w1_lo is None else [w1, w1_lo]
    return pl.pallas_call(
        functools.partial(_inproj_body, precise=w1_lo is not None),
        grid=(n // tm,),
        in_specs=[
            pl.BlockSpec((tm, D_MODEL), lambda i: (i, 0)),
            _full((1, D_MODEL)),
        ] + [_full((D_MODEL, W1_COLS))] * len(ws),
        out_specs=[pl.BlockSpec((tm, w), lambda i: (i, 0)) for w in widths],
        out_shape=[jax.ShapeDtypeStruct((n, w), F32) for w in widths],
        compiler_params=_cparams(("parallel",)),
        name="inproj",
    )(x, g, *ws)


S5_CHUNK = 128
S5_J = S5_CHUNK // SUBLANES


def _s5_tables_body(ar_r, ai_r, ldt_r, ar_c, ai_c, ldt_c, bre_ref, bim_ref,
                    a8re, a8im, apre, apim, a16re, a16im, bbre, bbim):
    dt = jnp.exp(ldt_r[...])
    mag = jnp.exp(ar_r[...] * dt)
    ang = ai_r[...] * dt
    are = mag * jnp.cos(ang)
    aim = mag * jnp.sin(ang)
    a8re[...] = jnp.broadcast_to(are, (SUBLANES, SSM_N))
    a8im[...] = jnp.broadcast_to(aim, (SUBLANES, SSM_N))
    pr, pi = are, aim
    for j in range(S5_J):
        apre[j * SUBLANES:(j + 1) * SUBLANES, :] = jnp.broadcast_to(pr, (SUBLANES, SSM_N))
        apim[j * SUBLANES:(j + 1) * SUBLANES, :] = jnp.broadcast_to(pi, (SUBLANES, SSM_N))
        if j + 1 < S5_J:
            pr, pi = pr * are - pi * aim, pr * aim + pi * are
    sre, sim = pr, pi
    qr, qi = sre, sim
    for s in range(SUBLANES):
        a16re[s:s + 1, :] = qr
        a16im[s:s + 1, :] = qi
        if s + 1 < SUBLANES:
            qr, qi = qr * sre - qi * sim, qr * sim + qi * sre
    dtc = jnp.exp(ldt_c[...])
    arc, aic = ar_c[...], ai_c[...]
    magc = jnp.exp(arc * dtc)
    angc = aic * dtc
    arec = magc * jnp.cos(angc)
    aimc = magc * jnp.sin(angc)
    den = arc * arc + aic * aic
    cre = ((arec - 1.0) * arc + aimc * aic) / den
    cim = (aimc * arc - (arec - 1.0) * aic) / den
    br, bi = bre_ref[...], bim_ref[...]
    bbre[...] = cre * br - cim * bi
    bbim[...] = cre * bi + cim * br


def _s5_tables(a_re, a_im, log_dt, b_re, b_im):
    ar_r = a_re.reshape(1, SSM_N)
    ai_r = a_im.reshape(1, SSM_N)
    ldt_r = jnp.repeat(log_dt, SSM_STATE).reshape(1, SSM_N)
    row = jax.ShapeDtypeStruct((SUBLANES, SSM_N), F32)
    tab = jax.ShapeDtypeStruct((S5_CHUNK, SSM_N), F32)
    col = jax.ShapeDtypeStruct((SSM_N, SSM_CH), F32)
    return pl.pallas_call(
        _s5_tables_body,
        out_shape=[row, row, tab, tab, row, row, col, col],
        name="s5_tables",
    )(ar_r, ai_r, ldt_r, ar_r.reshape(SSM_N, 1), ai_r.reshape(SSM_N, 1), ldt_r.reshape(SSM_N, 1),
      b_re.reshape(SSM_N, SSM_CH), b_im.reshape(SSM_N, SSM_CH))


def _block_diag_b(bb):
    t = bb.reshape(SSM_GROUPS, SSM_STATE, SSM_CH).transpose(0, 2, 1)
    eye = jnp.eye(SSM_GROUPS, dtype=bb.dtype)
    return (t[:, :, None, :] * eye[:, None, :, None]).reshape(SSM_DIM, SSM_N)


def _block_diag_c(c):
    t = c.transpose(0, 2, 1)
    eye = jnp.eye(SSM_GROUPS, dtype=c.dtype)
    return (t[:, :, None, :] * eye[:, None, :, None]).reshape(SSM_N, SSM_DIM)


def _s5_perm():
    p = np.zeros((S5_CHUNK, S5_CHUNK), np.float32)
    for j in range(S5_J):
        for s in range(SUBLANES):
            p[j * SUBLANES + s, S5_J * s + j] = 1.0
    return p


def _s5_epilogue(y, u, d_ref, wglu_ref, wglu_lo, bglu_ref):
    y = y + d_ref[...] * u
    g = _gelu_tanh(y)
    return g * _sigmoid(_mm(g, wglu_ref, wglu_lo) + bglu_ref[...])


def _s5_seq_body(*refs, tl, precise, carry_in):
    it = iter(refs)
    u_ref = next(it)
    h0r_ref, h0i_ref = (next(it), next(it)) if carry_in else (None, None)
    (perm_ref, permt_ref, bmat_ref, cmat_ref, a8re_ref, a8im_ref, apre_ref, apim_ref, a16re_ref, a16im_ref,
     d_ref, wglu_ref, bglu_ref) = (next(it) for _ in range(13))
    bmat_lo, cmat_lo, wglu_lo = (next(it), next(it), next(it)) if precise else (None, None, None)
    y_ref, hre_ref, him_ref, pre_r_ref, pre_i_ref, cre_scr, cim_scr = it
    li = pl.program_id(1)

    @pl.when(li == 0)
    def _():
        if carry_in:
            cre_scr[...] = h0r_ref[...]
            cim_scr[...] = h0i_ref[...]
        else:
            cre_scr[...] = jnp.zeros_like(cre_scr)
            cim_scr[...] = jnp.zeros_like(cim_scr)

    are = a8re_ref[...]
    aim = a8im_ref[...]
    row8 = lax.broadcasted_iota(jnp.int32, (SUBLANES, SSM_N), 0)
    nchunk = tl // S5_CHUNK

    for c in range(nchunk):
        if c == nchunk - 1:
            pre_r_ref[...] = cre_scr[...]
            pre_i_ref[...] = cim_scr[...]
        u = u_ref[c * S5_CHUNK:(c + 1) * S5_CHUNK, :]
        u_hi = u.astype(BF16)
        up = _dot(perm_ref[...], u_hi).astype(BF16)
        bu = _dot(up, bmat_ref[...])
        if precise:
            up_lo = _dot(perm_ref[...], (u - u_hi.astype(F32)).astype(BF16)).astype(BF16)
            bu = bu + (_dot(up_lo, bmat_ref[...]) + _dot(up, bmat_lo[...]))
        hr = [bu[0:SUBLANES, :SSM_N]]
        hi = [bu[0:SUBLANES, SSM_N:]]
        for j in range(1, S5_J):
            br = bu[j * SUBLANES:(j + 1) * SUBLANES, :SSM_N]
            bi = bu[j * SUBLANES:(j + 1) * SUBLANES, SSM_N:]
            hr.append(are * hr[-1] - aim * hi[-1] + br)
            hi.append(are * hi[-1] + aim * hr[-2] + bi)
        er, ei = hr[-1], hi[-1]
        for k, d in enumerate((1, 2, 4)):
            mr = jnp.broadcast_to(a16re_ref[d - 1:d, :], (SUBLANES, SSM_N))
            mi = jnp.broadcast_to(a16im_ref[d - 1:d, :], (SUBLANES, SSM_N))
            sr = jnp.where(row8 >= d, pltpu.roll(er, d, 0), 0.0)
            si = jnp.where(row8 >= d, pltpu.roll(ei, d, 0), 0.0)
            er, ei = er + mr * sr - mi * si, ei + mr * si + mi * sr
        h0r = jnp.broadcast_to(cre_scr[0:1, :], (SUBLANES, SSM_N))
        h0i = jnp.broadcast_to(cim_scr[0:1, :], (SUBLANES, SSM_N))
        p16r, p16i = a16re_ref[...], a16im_ref[...]
        er, ei = er + p16r * h0r - p16i * h0i, ei + p16r * h0i + p16i * h0r
        cinr = jnp.where(row8 == 0, h0r, pltpu.roll(er, 1, 0))
        cini = jnp.where(row8 == 0, h0i, pltpu.roll(ei, 1, 0))
        cre_scr[...] = jnp.broadcast_to(er[SUBLANES - 1:SUBLANES, :], (SUBLANES, SSM_N))
        cim_scr[...] = jnp.broadcast_to(ei[SUBLANES - 1:SUBLANES, :], (SUBLANES, SSM_N))
        fr, fi = [], []
        for j in range(S5_J):
            pr = apre_ref[j * SUBLANES:(j + 1) * SUBLANES, :]
            pi = apim_ref[j * SUBLANES:(j + 1) * SUBLANES, :]
            fr.append(hr[j] + pr * cinr - pi * cini)
            fi.append(hi[j] + pr * cini + pi * cinr)
        hfull = jnp.concatenate([jnp.concatenate(fr, axis=0), jnp.concatenate(fi, axis=0)], axis=1)
        yp = _mm(hfull, cmat_ref, cmat_lo)
        y_hi, y_mid, y_lo = _split3(yp)
        pt = permt_ref[...]
        y = _dot(pt, y_hi) + _dot(pt, y_mid) + _dot(pt, y_lo)
        y_ref[c * S5_CHUNK:(c + 1) * S5_CHUNK, :] = _s5_epilogue(y, u, d_ref, wglu_ref, wglu_lo, bglu_ref)

    hre_ref[...] = cre_scr[...]
    him_ref[...] = cim_scr[...]


def _s5_seq(u, s5w, bsz, seq, tl=512, precise=False, h0=None):
    nt = seq // tl
    consts = list(s5w["tabs"]) + [s5w["d"], s5w["wglu"], s5w["bglu"]]
    if precise:
        consts += [s5w["bmat_lo"], s5w["cmat_lo"], s5w["wglu_lo"]]
    state_spec = pl.BlockSpec((None, SUBLANES, SSM_N), lambda b, i: (b, 0, 0))
    state_shape = jax.ShapeDtypeStruct((bsz, SUBLANES, SSM_N), F32)
    carry = [] if h0 is None else list(h0)
    return pl.pallas_call(
        functools.partial(_s5_seq_body, tl=tl, precise=precise, carry_in=h0 is not None),
        grid=(bsz, nt),
        in_specs=[pl.BlockSpec((tl, SSM_DIM), lambda b, i: (b * nt + i, 0))]
        + [state_spec] * len(carry) + [_full(c.shape) for c in consts],
        out_specs=[pl.BlockSpec((tl, SSM_DIM), lambda b, i: (b * nt + i, 0))] + [state_spec] * 4,
        out_shape=[jax.ShapeDtypeStruct((bsz * seq, SSM_DIM), F32)] + [state_shape] * 4,
        scratch_shapes=[pltpu.VMEM((SUBLANES, SSM_N), F32), pltpu.VMEM((SUBLANES, SSM_N), F32)],
        compiler_params=_cparams(("parallel", "arbitrary")),
        name="s5_seq",
    )(u, *carry, *consts)


def _s5_step_body(u_ref, h0r_ref, h0i_ref, bmat_ref, cmat_ref, a8re_ref, a8im_ref,
                  d_ref, wglu_ref, bglu_ref, bmat_lo, cmat_lo, wglu_lo, y_ref, hre_ref, him_ref):
    u = u_ref[...]
    bu = _mm(u, bmat_ref, bmat_lo)
    are = a8re_ref[0:1, :]
    aim = a8im_ref[0:1, :]
    h0r, h0i = h0r_ref[...], h0i_ref[...]
    hr = are * h0r - aim * h0i + bu[:, :SSM_N]
    hi = are * h0i + aim * h0r + bu[:, SSM_N:]
    hre_ref[...] = hr
    him_ref[...] = hi
    y = _mm(jnp.concatenate([hr, hi], axis=1), cmat_ref, cmat_lo)
    y_ref[...] = _s5_epilogue(y, u, d_ref, wglu_ref, wglu_lo, bglu_ref)


def _s5_step(u, h0r, h0i, s5w):
    _, _, bmat, cmat, a8re, a8im = s5w["tabs"][:6]
    n = u.shape[0]
    return pl.pallas_call(
        _s5_step_body,
        out_shape=[
            jax.ShapeDtypeStruct((n, SSM_DIM), F32),
            jax.ShapeDtypeStruct((n, SSM_N), F32),
            jax.ShapeDtypeStruct((n, SSM_N), F32),
        ],
        compiler_params=pltpu.CompilerParams(vmem_limit_bytes=VMEM_LIMIT),
        name="s5_step",
    )(u, h0r, h0i, bmat, cmat, a8re, a8im, s5w["d"], s5w["wglu"], s5w["bglu"],
      s5w["bmat_lo"], s5w["cmat_lo"], s5w["wglu_lo"])


def _s5_prepare(a_re, a_im, log_dt, b_re, b_im, c_re, c_im, d_skip, w_glu, b_glu):
    a8re, a8im, apre, apim, a16re, a16im, bbre, bbim = _s5_tables(a_re, a_im, log_dt, b_re, b_im)
    bmat, bmat_lo = _hilo(jnp.concatenate([_block_diag_b(bbre), _block_diag_b(bbim)], axis=1))
    cmat, cmat_lo = _hilo(jnp.concatenate([_block_diag_c(c_re), -_block_diag_c(c_im)], axis=0))
    wglu, wglu_lo = _hilo(w_glu)
    perm = _s5_perm()
    tabs = (jnp.asarray(perm, BF16), jnp.asarray(perm.T, BF16), bmat, cmat, a8re, a8im, apre, apim, a16re, a16im)
    return dict(tabs=tabs, d=d_skip.reshape(1, SSM_DIM), wglu=wglu, bglu=b_glu.reshape(1, SSM_DIM),
                bmat_lo=bmat_lo, cmat_lo=cmat_lo, wglu_lo=wglu_lo)


TAIL = 128

MERGE_W = ("w2", "wbs", "wba", "wbc", "wo")


def _merge_weights(w2, w_br_ssm, w_br_attn, w_br_conv, w_out, conv_w):
    mw = {"cw": conv_w}
    for name, w in zip(MERGE_W, (w2, w_br_ssm, w_br_attn, w_br_conv, w_out)):
        mw[name], mw[name + "_lo"] = _hilo(w)
    return mw


def _merge_wlist(mw, precise):
    return [mw["cw"]] + [mw[n] for n in MERGE_W] + ([mw[n + "_lo"] for n in MERGE_W] if precise else [])


def _merge_core(x, g_ref, wrefs, precise, up0, up1, ys_ref, ya_ref):
    cw_ref, w2_ref, wbs_ref, wba_ref, wbc_ref, wo_ref = wrefs[:6]
    w2_lo, wbs_lo, wba_lo, wbc_lo, wo_lo = wrefs[6:] if precise else (None,) * 5
    zc = _mm(_rms(x, g_ref[...]), w2_ref, w2_lo)
    conv_b = zc[:, 0:CONV_DIM]
    uc = zc[:, CONV_DIM:2 * CONV_DIM] * zc[:, 2 * CONV_DIM:3 * CONV_DIM]
    yc = conv_b * (cw_ref[0:1, :] * up0(uc) + cw_ref[1:2, :] * up1(uc) + cw_ref[2:3, :] * uc)
    g0 = 3 * CONV_DIM
    g_ssm = _sigmoid(zc[:, g0:g0 + D_MODEL])
    g_attn = _sigmoid(zc[:, g0 + D_MODEL:g0 + 2 * D_MODEL])
    g_conv = _sigmoid(zc[:, g0 + 2 * D_MODEL:g0 + 3 * D_MODEL])
    merged = (g_ssm * _mm(ys_ref[...], wbs_ref, wbs_lo)
              + g_attn * _mm(ya_ref[...], wba_ref, wba_lo)
              + g_conv * _mm(yc, wbc_ref, wbc_lo))
    return x + _mm(merged, wo_ref, wo_lo), uc


def _merge_seq_body(*refs, tm, precise, carry_in):
    it = iter(refs)
    x_ref, g_ref, ys_ref, ya_ref = (next(it) for _ in range(4))
    cin_ref = next(it) if carry_in else None
    wrefs = [next(it) for _ in range(11 if precise else 6)]
    o_ref, cl_ref, pre_ref, stage = it
    ti = pl.program_id(1)

    @pl.when(ti == 0)
    def _():
        stage[0:SUBLANES, :] = cin_ref[...] if carry_in else jnp.zeros((SUBLANES, CONV_DIM), F32)

    def up0(uc):
        stage[SUBLANES:SUBLANES + tm, :] = uc
        return stage[SUBLANES - 2:SUBLANES - 2 + tm, :]

    def up1(uc):
        return stage[SUBLANES - 1:SUBLANES - 1 + tm, :]

    out, uc = _merge_core(x_ref[...], g_ref, wrefs, precise, up0, up1, ys_ref, ya_ref)
    o_ref[...] = out
    last = uc[tm - SUBLANES:tm, :]
    stage[0:SUBLANES, :] = last
    cl_ref[...] = last
    pre_ref[...] = uc[max(tm - TAIL, SUBLANES) - SUBLANES:max(tm - TAIL, SUBLANES), :]


def _merge_seq(x, g, ys, ya, mw, bsz, seq, tm=256, precise=False, cin=None):
    nt = seq // tm
    ws = _merge_wlist(mw, precise)
    row = lambda w: pl.BlockSpec((tm, w), lambda b, i: (b * nt + i, 0))
    cspec = pl.BlockSpec((None, SUBLANES, CONV_DIM), lambda b, i: (b, 0, 0))
    carry = [] if cin is None else [cin]
    return pl.pallas_call(
        functools.partial(_merge_seq_body, tm=tm, precise=precise, carry_in=cin is not None),
        grid=(bsz, nt),
        in_specs=[row(D_MODEL), _full(g.shape), row(SSM_DIM), row(ATTN_DIM)] + [cspec] * len(carry)
        + [_full1(w.shape) for w in ws],
        out_specs=[row(D_MODEL), cspec, cspec],
        out_shape=[jax.ShapeDtypeStruct((bsz * seq, D_MODEL), F32)]
        + [jax.ShapeDtypeStruct((bsz, SUBLANES, CONV_DIM), F32)] * 2,
        scratch_shapes=[pltpu.VMEM((tm + SUBLANES, CONV_DIM), F32)],
        compiler_params=_cparams(("parallel", "arbitrary")),
        name="merge_seq",
    )(x, g, ys, ya, *carry, *ws)


def _merge_step_body(x_ref, g_ref, ys_ref, ya_ref, p0_ref, p1_ref, *refs):
    wrefs, (o_ref, uc_ref) = refs[:11], refs[11:]
    out, uc = _merge_core(x_ref[...], g_ref, wrefs, True, lambda _: p0_ref[...], lambda _: p1_ref[...],
                          ys_ref, ya_ref)
    o_ref[...] = out
    uc_ref[...] = uc


def _merge_step(x, g, ys, ya, prev0, prev1, mw):
    n = x.shape[0]
    return pl.pallas_call(
        _merge_step_body,
        out_shape=[jax.ShapeDtypeStruct((n, D_MODEL), F32), jax.ShapeDtypeStruct((n, CONV_DIM), F32)],
        compiler_params=pltpu.CompilerParams(vmem_limit_bytes=VMEM_LIMIT),
        name="merge_step",
    )(x, g, ys, ya, prev0, prev1, *_merge_wlist(mw, True))


ROUTE_E0 = SUBLANES


def _route_math(lg, le):
    gmax = functools.reduce(jnp.maximum, lg)
    gsum = functools.reduce(lambda a, b: a + b, [jnp.exp(v - gmax) for v in lg])
    gw = 1.0 / gsum
    gsel = jnp.full_like(gmax, N_GROUPS - 1).astype(jnp.int32)
    for k in range(N_GROUPS - 2, -1, -1):
        gsel = jnp.where(lg[k] == gmax, k, gsel)
    ls = []
    for j in range(EPG):
        v = le[j]
        for k in range(1, N_GROUPS):
            v = jnp.where(gsel == k, le[k * EPG + j], v)
        ls.append(v)
    emax = functools.reduce(jnp.maximum, ls)
    ex = [jnp.exp(v - emax) for v in ls]
    esum = functools.reduce(lambda a, b: a + b, ex)
    pe = [v / esum for v in ex]
    v1 = functools.reduce(jnp.maximum, pe)
    i1 = jnp.full_like(gsel, EPG - 1)
    for j in range(EPG - 2, -1, -1):
        i1 = jnp.where(pe[j] == v1, j, i1)
    pe2 = [jnp.where(i1 == j, -1.0, pe[j]) for j in range(EPG)]
    v2 = functools.reduce(jnp.maximum, pe2)
    i2 = jnp.full_like(gsel, EPG - 1)
    for j in range(EPG - 2, -1, -1):
        i2 = jnp.where(pe2[j] == v2, j, i2)
    tot = v1 + v2
    w1 = v1 / tot * gw
    w2 = v2 / tot * gw
    return gsel * EPG + i1, gsel * EPG + i2, w1, w2


def _moe_route(h, wr_ref, wrt_ref, br_ref, brt_ref):
    tm = h.shape[0]
    h_hi = h.astype(BF16)
    h_lo = (h - h_hi.astype(F32)).astype(BF16)
    if tm % LANES:
        logits = _dot(h_hi, wr_ref[0]) + (_dot(h_lo, wr_ref[0]) + _dot(h_hi, wr_ref[1])) + br_ref[...]
        lg = [logits[:, k:k + 1] for k in range(N_GROUPS)]
        le = [logits[:, ROUTE_E0 + e:ROUTE_E0 + e + 1] for e in range(N_EXPERTS)]
        e1, e2, w1, w2 = _route_math(lg, le)
        lane = lax.broadcasted_iota(jnp.int32, (tm, LANES), 1)
        return jnp.where(lane == e1, w1, 0.0) + jnp.where(lane == e2, w2, 0.0)
    lt = _dot_nt(wrt_ref[0], h_hi) + (_dot_nt(wrt_ref[0], h_lo) + _dot_nt(wrt_ref[1], h_hi)) + brt_ref[...]
    lg = [lt[k:k + 1, :] for k in range(N_GROUPS)]
    le = [lt[ROUTE_E0 + e:ROUTE_E0 + e + 1, :] for e in range(N_EXPERTS)]
    e1, e2, w1, w2 = _route_math(lg, le)
    rows = [jnp.where(e1 == e, w1, 0.0) + jnp.where(e2 == e, w2, 0.0) for e in range(N_EXPERTS)]
    comb_t = jnp.concatenate(rows + [jnp.zeros((LANES - N_EXPERTS, tm), F32)], axis=0)
    return comb_t.T


def _moe_body(x_ref, g_ref, wr_ref, wrt_ref, br_ref, brt_ref, wgu_ref, wd_ref, gf_ref, o_ref,
              hb_scr, comb_scr, acc_scr, *, final):
    e = pl.program_id(1)

    @pl.when(e == 0)
    def _():
        h = _rms(x_ref[...], g_ref[...])
        hb_scr[...] = h.astype(BF16)
        comb_scr[...] = _moe_route(h, wr_ref, wrt_ref, br_ref, brt_ref)
        acc_scr[...] = jnp.zeros_like(acc_scr)

    hgu = _dot(hb_scr[...], wgu_ref[0])
    hg = hgu[:, :D_FF]
    lane = lax.broadcasted_iota(jnp.int32, comb_scr.shape, 1)
    ce = jnp.sum(jnp.where(lane == e, comb_scr[...], 0.0), axis=1, keepdims=True)
    act = hg * _sigmoid(hg) * hgu[:, D_FF:] * ce
    acc_scr[...] += _dot(act.astype(BF16), wd_ref[0])

    @pl.when(e == N_EXPERTS - 1)
    def _():
        y = x_ref[...] + acc_scr[...]
        o_ref[...] = _rms(y, gf_ref[...]) if final else y


def _moe(x, g, mo, gf, tm, final=False):
    n = x.shape[0]
    wr, wrt, br, brt, wgu, wd = (mo[k] for k in ("wr", "wrt", "br", "brt", "wgu", "wd"))
    return pl.pallas_call(
        functools.partial(_moe_body, final=final),
        grid=(n // tm, N_EXPERTS),
        in_specs=[
            pl.BlockSpec((tm, D_MODEL), lambda i, e: (i, 0)),
            _full(g.shape), _full(wr.shape), _full(wrt.shape), _full(br.shape), _full(brt.shape),
            pl.BlockSpec((1, D_MODEL, 2 * D_FF), lambda i, e: (e, 0, 0)),
            pl.BlockSpec((1, D_FF, D_MODEL), lambda i, e: (e, 0, 0)),
            _full(gf.shape),
        ],
        out_specs=pl.BlockSpec((tm, D_MODEL), lambda i, e: (i, 0)),
        out_shape=jax.ShapeDtypeStruct((n, D_MODEL), F32),
        scratch_shapes=[pltpu.VMEM((tm, D_MODEL), BF16), pltpu.VMEM((tm, LANES), F32),
                        pltpu.VMEM((tm, D_MODEL), F32)],
        compiler_params=_cparams(("parallel", "arbitrary")),
        name="moe",
    )(x, g, wr, wrt, br, brt, wgu, wd, gf)


def _moe_prepare(w_rg, b_rg, w_re, b_re, w_gate, w_up, w_down):
    wr = jnp.zeros((D_MODEL, LANES), F32)
    wr = wr.at[:, :N_GROUPS].set(w_rg).at[:, ROUTE_E0:ROUTE_E0 + N_EXPERTS].set(w_re)
    br = jnp.zeros((1, LANES), F32)
    br = br.at[0, :N_GROUPS].set(b_rg).at[0, ROUTE_E0:ROUTE_E0 + N_EXPERTS].set(b_re)
    wr2 = jnp.stack(_hilo(wr))
    return dict(wr=wr2, wrt=wr2.transpose(0, 2, 1), br=br, brt=br.reshape(LANES, 1),
                wgu=jnp.concatenate([w_gate, w_up], axis=-1).astype(BF16),
                wd=w_down.astype(BF16))


CMP_SUB = 256
CMP_W = 4 * HEAD_DIM
PAGES_PER_STEP = 32


def _cmp_perm():
    p = np.zeros((CMP_SUB, CMP_SUB), np.float32)
    for i in range(CMP_SUB):
        p[i, CMP_BLOCK * (i % SUBLANES) + i // SUBLANES] = 1.0
    return p


def _cmp_prepare(pe, w1, w2):
    eye4 = jnp.eye(4, dtype=F32)
    sel = jnp.array([0, 0, 1, 1])
    w1r = w1.reshape(2, CMP_BLOCK, HEAD_DIM, CMP_HIDDEN)[sel]
    w1bd = (w1r.transpose(1, 0, 2, 3)[:, :, :, None, :] * eye4[None, :, None, :, None])
    w1bd = w1bd.reshape(CMP_BLOCK, CMP_W, 4 * CMP_HIDDEN).astype(BF16)
    w2r = w2[sel]
    w2bd = (w2r[:, :, None, :] * eye4[:, None, :, None]).reshape(4 * CMP_HIDDEN, CMP_W).astype(BF16)
    pe4 = pe[sel].transpose(1, 0, 2).reshape(CMP_BLOCK, CMP_W)
    pe_exp = jnp.repeat(pe4, SUBLANES, axis=0)
    return jnp.asarray(_cmp_perm(), BF16), pe_exp, w1bd, w2bd


def _compress_rows(get_sub, nsub, perm_ref, pe_ref, w1_ref, w2_ref, stage, transposed=False):
    mm = _dot_nt if transposed else _dot
    for t in range(nsub):
        xp = mm(perm_ref[...], get_sub(t).astype(BF16))
        stage[t] = xp + pe_ref[...]
    acc = jnp.zeros((nsub * SUBLANES, 4 * CMP_HIDDEN), F32)
    for r in range(CMP_BLOCK):
        a = stage[:, r * SUBLANES:(r + 1) * SUBLANES, :].reshape(nsub * SUBLANES, CMP_W)
        acc = acc + _dot(a.astype(BF16), w1_ref[r])
    return _dot(_gelu_tanh(acc).astype(BF16), w2_ref[...])


def _compress_seq_body(x_ref, perm_ref, pe_ref, w1_ref, w2_ref, o_ref, stage, *, nsub):
    o_ref[...] = _compress_rows(lambda t: x_ref[t * CMP_SUB:(t + 1) * CMP_SUB, :], nsub,
                                perm_ref, pe_ref, w1_ref, w2_ref, stage)


def _compress_seq(kvp, cw, bsz, seq):
    nsub = seq // CMP_SUB
    nblk = seq // CMP_BLOCK
    return pl.pallas_call(
        functools.partial(_compress_seq_body, nsub=nsub),
        grid=(bsz,),
        in_specs=[pl.BlockSpec((seq, CMP_W), lambda b: (b, 0))] + [_full(c.shape) for c in cw],
        out_specs=pl.BlockSpec((None, nblk, CMP_W), lambda b: (b, 0, 0)),
        out_shape=jax.ShapeDtypeStruct((bsz, nblk, CMP_W), F32),
        scratch_shapes=[pltpu.VMEM((nsub, CMP_SUB, CMP_W), F32)],
        compiler_params=_cparams(("parallel",)),
        name="compress_seq",
    )(kvp, *cw)


def _compress_pages_body(pt_ref, *refs):
    pages = refs[:PAGES_PER_STEP]
    perm_ref, pe_ref, w1_ref, w2_ref, o_ref, stage = refs[PAGES_PER_STEP:]
    per = CMP_SUB // PAGE

    def get_sub(t):
        return jnp.concatenate([pages[per * t + k][...] for k in range(per)], axis=1)

    o_ref[...] = _compress_rows(get_sub, PAGES_PER_STEP // per, perm_ref, pe_ref, w1_ref, w2_ref, stage,
                                transposed=True)


def _compress_pages(cache_t, page_table, layer, cw):
    bsz, n_pages = page_table.shape
    steps = n_pages // PAGES_PER_STEP
    nsub = PAGES_PER_STEP * PAGE // CMP_SUB
    blk_per_step = PAGES_PER_STEP * PAGE // CMP_BLOCK

    def page_spec(k):
        return pl.BlockSpec((None, None, CMP_W, PAGE),
                            lambda b, i, pt: (layer, pt[b * n_pages + i * PAGES_PER_STEP + k], 0, 0))

    grid_spec = pltpu.PrefetchScalarGridSpec(
        num_scalar_prefetch=1,
        grid=(bsz, steps),
        in_specs=[page_spec(k) for k in range(PAGES_PER_STEP)]
        + [pl.BlockSpec(c.shape, lambda b, i, pt, nd=c.ndim: (0,) * nd) for c in cw],
        out_specs=pl.BlockSpec((None, blk_per_step, CMP_W), lambda b, i, pt: (b, i, 0)),
        scratch_shapes=[pltpu.VMEM((nsub, CMP_SUB, CMP_W), F32)],
    )
    return pl.pallas_call(
        _compress_pages_body,
        grid_spec=grid_spec,
        out_shape=jax.ShapeDtypeStruct((bsz, n_pages * PAGE // CMP_BLOCK, CMP_W), F32),
        compiler_params=_cparams(("parallel", "arbitrary")),
        name="compress_pages",
    )(page_table.reshape(-1), *([cache_t] * PAGES_PER_STEP), *cw)


TQ = 128
SLC_CHUNK = 512
QL = GRP * TQ
N_SELBLK_SEQ = 32
AUG_MASK = HEAD_DIM
AUG_POS = HEAD_DIM + 32
POS_SPLIT = 128


def _alibi_slopes():
    return [2.0 ** (-8.0 * (h + 1) / N_HEADS) for h in range(N_HEADS)]


def _nsa_tables(seq):
    pos = np.arange(seq)
    tbl = np.zeros((seq, HEAD_DIM), np.float32)
    tbl[pos, pos // SEL_BLOCK] = 1.0
    tbl[:, 32] = (pos // POS_SPLIT) * POS_SPLIT
    tbl[:, 33] = pos % POS_SPLIT
    ncmp = seq // CMP_BLOCK
    order = np.concatenate([np.arange(0, ncmp, 2), np.arange(1, ncmp, 2)])
    cend = (order + 1) * CMP_BLOCK - 1
    ctbl = np.zeros((ncmp, HEAD_DIM), np.float32)
    ctbl[:, 32] = (cend // POS_SPLIT) * POS_SPLIT
    ctbl[:, 33] = cend % POS_SPLIT
    pm = np.zeros((ncmp, ncmp), np.float32)
    pm[np.arange(ncmp), order] = 1.0
    slope = np.zeros((KV_HEADS, 16, QL), np.float32)
    sl = _alibi_slopes()
    for g in range(KV_HEADS):
        for m in range(GRP):
            slope[g, 0:2, m * TQ:(m + 1) * TQ] = sl[g * GRP + m]
    return (jnp.asarray(tbl), jnp.asarray(ctbl), jnp.asarray(cend.reshape(ncmp, 1).astype(np.int32)),
            jnp.asarray(pm, BF16), jnp.asarray(slope, BF16))


def _flash_chunk(kaug, vt, qa, valid, state):
    m, l, acc = state
    s = _dot(kaug, qa)
    if valid is not None:
        s = jnp.where(valid, s, NEG)
    mn = jnp.maximum(m, jnp.max(s, axis=0, keepdims=True))
    alpha = jnp.exp(m - mn)
    p = jnp.exp(s - mn)
    l = alpha * l + jnp.sum(p, axis=0, keepdims=True)
    acc = alpha * acc + _dot(vt, p.astype(BF16))
    return mn, l, acc


def _flash_init():
    return (jnp.full((1, QL), M_INIT, F32), jnp.zeros((1, QL), F32), jnp.zeros((HEAD_DIM, QL), F32))


def _nsa_seq_body(q_ref, kvp_ref, kvw_ref, gt_ref, kc_ref, tbl_ref, ctbl_ref, cend_ref, pm_ref,
                  slope_ref, y_ref, kslc, kwin, vslc, vwin, qaug):
    i = pl.program_id(1)
    l0 = i * TQ

    @pl.when(i == 0)
    def _():
        kslc[...] = jnp.zeros_like(kslc)
        kwin[...] = jnp.zeros_like(kwin)
        vslc[...] = jnp.zeros_like(vslc)
        vwin[...] = jnp.zeros_like(vwin)

    q = q_ref[...]
    kvp = kvp_ref[...]
    kvw = kvw_ref[...]
    tbl = tbl_ref[...]
    lane64 = lax.broadcasted_iota(jnp.int32, (TQ, HEAD_DIM), 1)
    tblw = jnp.where(lane64 < 32, 0.0, tbl)
    rows = pl.ds(pl.multiple_of(l0, TQ), TQ)
    for g in range(KV_HEADS):
        ks = kvp[:, 2 * KV_DIM + g * HEAD_DIM:2 * KV_DIM + (g + 1) * HEAD_DIM]
        kslc[g, rows, :] = jnp.concatenate([ks, tbl], axis=1).astype(BF16)
        kw = kvw[:, g * HEAD_DIM:(g + 1) * HEAD_DIM]
        kwin[g, rows, :] = jnp.concatenate([kw, tblw], axis=1).astype(BF16)
    vslc[i] = kvp[:, 3 * KV_DIM:4 * KV_DIM].T.astype(BF16)
    vwin[i] = kvw[:, KV_DIM:2 * KV_DIM].T.astype(BF16)

    scale = HEAD_DIM ** -0.5
    qt = [(q[:, j * LANES:(j + 1) * LANES] * scale).T for j in range(ATTN_DIM // LANES)]
    for g in range(KV_HEADS):
        heads = []
        for m in range(GRP):
            h = g * GRP + m
            heads.append(qt[h // 2][(h % 2) * HEAD_DIM:(h % 2 + 1) * HEAD_DIM, :])
        qaug[g, 0:HEAD_DIM, :] = jnp.concatenate(heads, axis=1).astype(BF16)
        qaug[g, AUG_MASK:AUG_POS, :] = jnp.zeros((AUG_POS - AUG_MASK, QL), BF16)
        qaug[g, AUG_POS:AUG_POS + 16, :] = slope_ref[g]
        qaug[g, AUG_POS + 16:, :] = jnp.zeros((2 * HEAD_DIM - AUG_POS - 16, QL), BF16)

    ncmp = kc_ref.shape[0]
    nsel = ncmp // 2
    kcp = _dot(pm_ref[...], kc_ref[...].astype(BF16))
    vct = jnp.concatenate([kcp[:, 2 * KV_DIM - KV_DIM:2 * KV_DIM],
                           jnp.zeros((LANES - ncmp, KV_DIM), F32)], axis=0).T
    lpos = l0 + lax.broadcasted_iota(jnp.int32, (1, QL), 1) % TQ
    valid_c = cend_ref[...] <= lpos
    lq = l0 + lax.broadcasted_iota(jnp.int32, (nsel, TQ), 1)
    blk = lax.broadcasted_iota(jnp.int32, (nsel, TQ), 0)
    cur = lq // SEL_BLOCK
    forced = (blk == 0) | (blk == cur) | (blk == cur - 1)
    causal_blk = blk * SEL_BLOCK <= lq
    o_cmp = []
    for g in range(KV_HEADS):
        kca = jnp.concatenate([kcp[:, g * HEAD_DIM:(g + 1) * HEAD_DIM], ctbl_ref[...]], axis=1).astype(BF16)
        s = jnp.where(valid_c, _dot(kca, qaug[g]), NEG)
        mx = jnp.max(s, axis=0, keepdims=True)
        e = jnp.where(valid_c, jnp.exp(s - mx), 0.0)
        p = e * (1.0 / jnp.maximum(jnp.sum(e, axis=0, keepdims=True), 1.0))
        vt_g = vct[g * HEAD_DIM:(g + 1) * HEAD_DIM, 0:ncmp].astype(BF16)
        o_cmp.append(_dot(vt_g, p.astype(BF16)))
        psum = p[:, 0:TQ]
        for m in range(1, GRP):
            psum = psum + p[:, m * TQ:(m + 1) * TQ]
        imp = psum[0:nsel, :] + psum[nsel:, :]
        score = jnp.where(causal_blk, imp + jnp.where(forced, FORCE_BONUS, 0.0), NEG)
        sel = jnp.zeros((nsel, TQ), jnp.bool_)
        for _ in range(N_SELECT):
            best = jnp.max(score, axis=0, keepdims=True)
            idx = jnp.min(jnp.where(score == best, blk, nsel), axis=0, keepdims=True)
            hit = (blk == idx) & (best > 0.5 * NEG)
            sel = sel | hit
            score = jnp.where(hit, 2.0 * NEG, score)
        mb = jnp.where(sel, 0.0, NEG)
        qaug[g, AUG_MASK:AUG_POS, :] = jnp.concatenate([mb] * GRP, axis=1).astype(BF16)

    qpos = l0 + lax.broadcasted_iota(jnp.int32, (1, QL), 1) % TQ
    per = SLC_CHUNK // TQ

    def slc_chunk(cb, st, valid):
        k0 = pl.multiple_of(cb * SLC_CHUNK, SLC_CHUNK)
        out = []
        for g in range(KV_HEADS):
            vt = jnp.concatenate([vslc[cb * per + k, g * HEAD_DIM:(g + 1) * HEAD_DIM, :] for k in range(per)], axis=1)
            out.append(_flash_chunk(kslc[g, pl.ds(k0, SLC_CHUNK), :], vt, qaug[g], valid, st[g]))
        return tuple(out)

    cbd = i // per
    st = lax.fori_loop(0, cbd, lambda cb, s: slc_chunk(cb, s, None), tuple(_flash_init() for _ in range(KV_HEADS)))
    kpos = cbd * SLC_CHUNK + lax.broadcasted_iota(jnp.int32, (SLC_CHUNK, 1), 0)
    st = slc_chunk(cbd, st, kpos <= qpos)
    o_slc = [acc * (1.0 / l) for (m, l, acc) in st]

    nwc = WINDOW // TQ + 1
    cw0 = jnp.maximum(i - (nwc - 1), 0)
    kw0 = pl.multiple_of(cw0 * TQ, TQ)
    dist = qpos - (kw0 + lax.broadcasted_iota(jnp.int32, (nwc * TQ, 1), 0))
    vis = (dist >= 0) & (dist < WINDOW)
    o_win = []
    for g in range(KV_HEADS):
        vt = jnp.concatenate([vwin[cw0 + k, g * HEAD_DIM:(g + 1) * HEAD_DIM, :] for k in range(nwc)], axis=1)
        m, l, acc = _flash_chunk(kwin[g, pl.ds(kw0, nwc * TQ), :], vt, qaug[g], vis, _flash_init())
        o_win.append(acc * (1.0 / l))

    sg = _sigmoid(gt_ref[...].T[0:3 * N_HEADS, :])
    outs = []
    for g in range(KV_HEADS):
        def gate_row(br):
            return jnp.concatenate([sg[br * N_HEADS + g * GRP + m:br * N_HEADS + g * GRP + m + 1, :]
                                    for m in range(GRP)], axis=1)
        outs.append(gate_row(0) * o_cmp[g] + gate_row(1) * o_slc[g] + gate_row(2) * o_win[g])
    for j in range(ATTN_DIM // LANES):
        rows = []
        for h in (2 * j, 2 * j + 1):
            g, m = divmod(h, GRP)
            rows.append(outs[g][:, m * TQ:(m + 1) * TQ])
        y_ref[:, j * LANES:(j + 1) * LANES] = jnp.concatenate(rows, axis=0).T


def _nsa_seq(q, kvp, kvw, gt, kc, bsz, seq):
    assert seq % SLC_CHUNK == 0 and seq >= WINDOW + TQ
    nt = seq // TQ
    tbl, ctbl, cend, pm, slope = _nsa_tables(seq)
    ncmp = seq // CMP_BLOCK
    row = lambda w: pl.BlockSpec((TQ, w), lambda b, i: (b * nt + i, 0))
    return pl.pallas_call(
        _nsa_seq_body,
        grid=(bsz, nt),
        in_specs=[row(ATTN_DIM), row(4 * KV_DIM), row(2 * KV_DIM), row(GATE_PAD),
                  pl.BlockSpec((None, ncmp, CMP_W), lambda b, i: (b, 0, 0)),
                  pl.BlockSpec((TQ, HEAD_DIM), lambda b, i: (i, 0)),
                  _full(ctbl.shape), _full(cend.shape), _full(pm.shape), _full(slope.shape)],
        out_specs=row(ATTN_DIM),
        out_shape=jax.ShapeDtypeStruct((bsz * seq, ATTN_DIM), F32),
        scratch_shapes=[pltpu.VMEM((KV_HEADS, seq, 2 * HEAD_DIM), BF16),
                        pltpu.VMEM((KV_HEADS, seq, 2 * HEAD_DIM), BF16),
                        pltpu.VMEM((nt, KV_DIM, TQ), BF16),
                        pltpu.VMEM((nt, KV_DIM, TQ), BF16),
                        pltpu.VMEM((KV_HEADS, 2 * HEAD_DIM, QL), BF16)],
        compiler_params=_cparams(("parallel", "arbitrary")),
        name="nsa_seq",
    )(q, kvp, kvw, gt, kc, tbl, ctbl, cend, pm, slope)


N_PICK = N_SELECT - 1


def _q_rows(q, g, scale):
    rows = [q[:, (g * GRP + m) * HEAD_DIM:(g * GRP + m + 1) * HEAD_DIM] for m in range(GRP)]
    return jnp.concatenate(rows + [jnp.zeros((SUBLANES - GRP, HEAD_DIM), F32)], axis=0) * scale


def _slope_col(g):
    sl = _alibi_slopes()
    row = lax.broadcasted_iota(jnp.int32, (SUBLANES, 1), 0)
    col = jnp.zeros((SUBLANES, 1), F32)
    for m in range(GRP):
        col = jnp.where(row == m, sl[g * GRP + m], col)
    return col


def _nsa_step_cmp_body(q_ref, kc_ref, pair_ref, o_ref, idx_ref, *, past_len):
    q = q_ref[...]
    kc = kc_ref[...]
    ncmp = kc.shape[0]
    nsel = ncmp // 2
    cur = past_len // SEL_BLOCK
    scale = HEAD_DIM ** -0.5
    cend = (lax.broadcasted_iota(jnp.int32, (1, ncmp), 1) + 1) * CMP_BLOCK - 1
    dist = (past_len - cend).astype(F32)
    lane = lax.broadcasted_iota(jnp.int32, (1, nsel), 1)
    forced = (lane == 0) | (lane == cur) | (lane == cur - 1)
    lane_o = lax.broadcasted_iota(jnp.int32, (1, LANES), 1)
    idx_out = jnp.zeros((1, LANES), jnp.int32)
    for g in range(KV_HEADS):
        q8 = _q_rows(q, g, scale).astype(BF16)
        s = _dot_nt(q8, kc[:, g * HEAD_DIM:(g + 1) * HEAD_DIM].astype(BF16)) - _slope_col(g) * dist
        mx = jnp.max(s, axis=1, keepdims=True)
        e = jnp.exp(s - mx)
        p = e / jnp.maximum(jnp.sum(e, axis=1, keepdims=True), 1.0)
        vg = kc[:, KV_DIM + g * HEAD_DIM:KV_DIM + (g + 1) * HEAD_DIM]
        o_ref[g] = _dot(p.astype(BF16), vg.astype(BF16))
        psum = jnp.sum(p[0:GRP, :], axis=0, keepdims=True)
        p_hi, p_mid, p_lo = _split3(jnp.broadcast_to(psum, (SUBLANES, ncmp)))
        pair = pair_ref[...]
        imp = (_dot(p_hi, pair) + _dot(p_mid, pair) + _dot(p_lo, pair))[0:1, :]
        score = imp + jnp.where(forced, FORCE_BONUS, 0.0)
        for t in range(N_PICK):
            best = jnp.max(score, axis=1, keepdims=True)
            idx = jnp.min(jnp.where(score == best, lane, nsel), axis=1, keepdims=True)
            score = jnp.where(lane == idx, NEG, score)
            idx_out = jnp.where(lane_o == g * N_SELECT + t, idx, idx_out)
    idx_ref[...] = idx_out


def _nsa_step_attn_body(sel_ref, pt_ref, *refs, past_len):
    nb = KV_HEADS * N_PICK
    blocks = refs[:nb]
    win_ref, q_ref, gt_ref, kvp_ref, kvw_ref, ocmp_ref, y_ref = refs[nb:]
    b = pl.program_id(0)
    q = q_ref[...]
    kvp = kvp_ref[...]
    kvw = kvw_ref[...]
    sg = _sigmoid(gt_ref[...])
    w = win_ref[...]
    nwin = w.shape[1]
    scale = HEAD_DIM ** -0.5
    pl_ = lax.broadcasted_iota(jnp.int32, (1, PAGE), 1)
    wl = lax.broadcasted_iota(jnp.int32, (1, nwin), 1)
    per_page = PAGE // SEL_BLOCK
    for g in range(KV_HEADS):
        q8 = _q_rows(q, g, scale)
        q8b = q8.astype(BF16)
        sc = _slope_col(g)
        hs = slice(g * HEAD_DIM, (g + 1) * HEAD_DIM)
        vs_ = slice(KV_DIM + g * HEAD_DIM, KV_DIM + (g + 1) * HEAD_DIM)
        k_new = kvp[:, 2 * KV_DIM + g * HEAD_DIM:2 * KV_DIM + (g + 1) * HEAD_DIM]
        v_new = kvp[:, 3 * KV_DIM + g * HEAD_DIM:3 * KV_DIM + (g + 1) * HEAD_DIM]
        s_cur = jnp.sum(q8 * k_new, axis=1, keepdims=True)
        ss, vs = [], []
        for t in range(N_PICK):
            blk = blocks[g * N_PICK + t][...]
            s_idx = sel_ref[b * KV_HEADS * N_SELECT + g * N_SELECT + t]
            kpos = (s_idx // per_page) * PAGE + pl_
            inblk = pl_ // SEL_BLOCK == s_idx % per_page
            st = _dot(q8b, blk[hs, :].astype(BF16)) - sc * (past_len - kpos).astype(F32)
            ss.append(jnp.where(inblk, st, NEG))
            vs.append(blk[vs_, :].astype(BF16))
        mx = functools.reduce(jnp.maximum, [jnp.max(s, axis=1, keepdims=True) for s in ss] + [s_cur])
        es = [jnp.exp(s - mx) for s in ss]
        e_cur = jnp.exp(s_cur - mx)
        den = functools.reduce(lambda a, c: a + c, [jnp.sum(e, axis=1, keepdims=True) for e in es] + [e_cur])
        acc = e_cur * v_new
        for e, v in zip(es, vs):
            acc = acc + _dot_nt(e.astype(BF16), v)
        o_slc = acc / den
        kw_new = kvw[:, g * HEAD_DIM:(g + 1) * HEAD_DIM]
        vw_new = kvw[:, KV_DIM + g * HEAD_DIM:KV_DIM + (g + 1) * HEAD_DIM]
        dw = (nwin - wl).astype(F32)
        s_w = jnp.where(nwin - wl < WINDOW, _dot(q8b, w[hs, :].astype(BF16)) - sc * dw, NEG)
        s_wc = jnp.sum(q8 * kw_new, axis=1, keepdims=True)
        mw = jnp.maximum(jnp.max(s_w, axis=1, keepdims=True), s_wc)
        e_w = jnp.exp(s_w - mw)
        e_wc = jnp.exp(s_wc - mw)
        o_win = (_dot_nt(e_w.astype(BF16), w[vs_, :].astype(BF16))
                 + e_wc * vw_new) / (jnp.sum(e_w, axis=1, keepdims=True) + e_wc)

        def gate_col(br):
            cols = [sg[:, br * N_HEADS + g * GRP + m:br * N_HEADS + g * GRP + m + 1] for m in range(GRP)]
            return jnp.concatenate(cols + [jnp.zeros((SUBLANES - GRP, 1), F32)], axis=0)

        y_ref[g] = gate_col(0) * ocmp_ref[g] + gate_col(1) * o_slc + gate_col(2) * o_win


def _nsa_step(q, kvp, kvw, gt, kc, cache, page_table, cache_win, layer):
    n, n_pages = page_table.shape
    past_len = n_pages * PAGE
    ncmp = kc.shape[1]
    nsel = ncmp // 2
    pair = np.zeros((ncmp, nsel), np.float32)
    pair[np.arange(ncmp), np.arange(ncmp) // 2] = 1.0
    r3 = lambda a: a.reshape(n, 1, a.shape[-1])
    row3 = lambda w: pl.BlockSpec((None, 1, w), lambda b, *_: (b, 0, 0))
    o_cmp, idx = pl.pallas_call(
        functools.partial(_nsa_step_cmp_body, past_len=past_len),
        grid=(n,),
        in_specs=[row3(ATTN_DIM), pl.BlockSpec((None, ncmp, CMP_W), lambda b: (b, 0, 0)), _full(pair.shape)],
        out_specs=[pl.BlockSpec((None, KV_HEADS, SUBLANES, HEAD_DIM), lambda b: (b, 0, 0, 0)), row3(LANES)],
        out_shape=[jax.ShapeDtypeStruct((n, KV_HEADS, SUBLANES, HEAD_DIM), F32),
                   jax.ShapeDtypeStruct((n, 1, LANES), jnp.int32)],
        compiler_params=_cparams(("parallel",)),
        name="nsa_step_cmp",
    )(r3(q), kc, jnp.asarray(pair, BF16))
    sel = idx[:, 0, :KV_HEADS * N_SELECT].reshape(-1)

    def blk_spec(j):
        g, t = divmod(j, N_PICK)

        def imap(b, sel_ref, pt_ref):
            s = sel_ref[b * KV_HEADS * N_SELECT + g * N_SELECT + t]
            return (layer, pt_ref[b * n_pages + s // (PAGE // SEL_BLOCK)], 1, 0)

        return pl.BlockSpec((None, None, 2 * KV_DIM, PAGE), imap)

    nb = KV_HEADS * N_PICK
    grid_spec = pltpu.PrefetchScalarGridSpec(
        num_scalar_prefetch=2,
        grid=(n,),
        in_specs=[blk_spec(j) for j in range(nb)]
        + [pl.BlockSpec((None, None, CMP_W, cache_win.shape[3]), lambda b, *_: (layer, b, 0, 0)),
           row3(ATTN_DIM), row3(GATE_PAD), row3(4 * KV_DIM), row3(2 * KV_DIM),
           pl.BlockSpec((None, KV_HEADS, SUBLANES, HEAD_DIM), lambda b, *_: (b, 0, 0, 0))],
        out_specs=pl.BlockSpec((None, KV_HEADS, SUBLANES, HEAD_DIM), lambda b, *_: (b, 0, 0, 0)),
    )
    y = pl.pallas_call(
        functools.partial(_nsa_step_attn_body, past_len=past_len),
        grid_spec=grid_spec,
        out_shape=jax.ShapeDtypeStruct((n, KV_HEADS, SUBLANES, HEAD_DIM), F32),
        compiler_params=_cparams(("arbitrary",)),
        name="nsa_step_attn",
    )(sel, page_table.reshape(-1), *([cache] * nb), cache_win, r3(q), r3(gt), r3(kvp), r3(kvw), o_cmp)
    return y[:, :, :GRP, :].reshape(n, ATTN_DIM)


def kernel(x_prompt, x_sample, cache_kv, page_table, cache_win, state_ssm, state_conv, norm_attn_g, w_in,
           ssm_a_re, ssm_a_im, ssm_log_dt, ssm_b_re, ssm_b_im, ssm_c_re, ssm_c_im, ssm_d, ssm_w_glu, ssm_b_glu,
           cmp_pe, cmp_w1, cmp_w2, conv_w, w_br_ssm, w_br_attn, w_br_conv, w_out, norm_ffn_g,
           w_router_group, b_router_group, w_router_expert, b_router_expert, moe_w_gate, moe_w_up, moe_w_down,
           norm_final_g):
    bp, lp, _ = x_prompt.shape
    bs = x_sample.shape[0]
    depth = w_in.shape[0]
    n_pool = cache_kv.shape[1]
    nwin = cache_win.shape[2]
    xp = x_prompt.reshape(bp * lp, D_MODEL)
    xs = x_sample.reshape(bs, D_MODEL)
    cache = cache_kv.transpose(0, 1, 3, 4, 5, 2).reshape(depth, n_pool, 4 * KV_DIM, PAGE)
    cwin = cache_win.transpose(0, 1, 3, 4, 5, 2).reshape(depth, bs, 2 * KV_DIM, nwin)
    gf = norm_final_g.reshape(1, D_MODEL)
    kv_p, kv_s, win_p, win_s, ssm_p, ssm_s, conv_p, conv_s = ([] for _ in range(8))

    def ssm_state(hre, him, n):
        return jnp.stack([hre.reshape(n, SSM_GROUPS, SSM_STATE), him.reshape(n, SSM_GROUPS, SSM_STATE)], axis=-1)

    for l in range(depth):
        w = w_in[l]
        w1, w1_lo = _hilo(jnp.concatenate(
            [w[:, :C_GATE], jnp.pad(w[:, C_GATE:C_CONV], ((0, 0), (0, GATE_PAD - 3 * N_HEADS)))], axis=1))
        g_attn = norm_attn_g[l].reshape(1, D_MODEL)
        g_ffn = norm_ffn_g[l].reshape(1, D_MODEL)
        s5w = _s5_prepare(ssm_a_re[l], ssm_a_im[l], ssm_log_dt[l], ssm_b_re[l], ssm_b_im[l],
                          ssm_c_re[l], ssm_c_im[l], ssm_d[l], ssm_w_glu[l], ssm_b_glu[l])
        cw = _cmp_prepare(cmp_pe[l], cmp_w1[l], cmp_w2[l])
        mw = _merge_weights(w[:, C_CONV:], w_br_ssm[l], w_br_attn[l], w_br_conv[l], w_out[l], conv_w[l])
        mo = _moe_prepare(w_router_group[l], b_router_group[l], w_router_expert[l], b_router_expert[l],
                          moe_w_gate[l], moe_w_up[l], moe_w_down[l])
        final = l == depth - 1

        u, q, kvp, kvw, gt = _inproj(xp, g_attn, w1, 512)
        y_ssm, hre, him, pre_r, pre_i = _s5_seq(u, s5w, bp, lp)
        kc = _compress_seq(kvp, cw, bp, lp)
        y_attn = _nsa_seq(q, kvp, kvw, gt, kc, bp, lp)
        x1, cl, cpre = _merge_seq(xp, g_attn, y_ssm, y_attn, mw, bp, lp)
        x2 = _moe(x1, g_ffn, mo, gf, 1024, final=final)
        if not final and lp >= 2 * TAIL:
            tail = lambda a: a.reshape(bp, lp, a.shape[-1])[:, lp - TAIL:].reshape(bp * TAIL, a.shape[-1])
            xt = tail(xp)
            ut = _inproj(xt, g_attn, w1, TAIL, w1_lo)[0]
            yst = _s5_seq(ut, s5w, bp, TAIL, tl=TAIL, precise=True, h0=(pre_r, pre_i))[0]
            x1t = _merge_seq(xt, g_attn, yst, tail(y_attn), mw, bp, TAIL, tm=TAIL, precise=True, cin=cpre)[0]
            x2t = _moe(x1t, g_ffn, mo, gf, min(1024, bp * TAIL), final=False)
            x2 = x2.reshape(bp, lp, D_MODEL).at[:, lp - TAIL:].set(x2t.reshape(bp, TAIL, D_MODEL))
            x2 = x2.reshape(bp * lp, D_MODEL)
        xp = x2
        kv_p.append(kvp.reshape(bp, lp, 4, KV_HEADS, HEAD_DIM))
        win_p.append(kvw.reshape(bp, lp, 2, KV_HEADS, HEAD_DIM)[:, lp - min(WINDOW, lp):])
        ssm_p.append(ssm_state(hre[:, 0], him[:, 0], bp))
        conv_p.append(cl[:, SUBLANES - 2:])

        u, q, kvp, kvw, gt = _inproj(xs, g_attn, w1, bs, w1_lo)
        st = state_ssm[l]
        y_ssm, hre, him = _s5_step(u, st[..., 0].reshape(bs, SSM_N), st[..., 1].reshape(bs, SSM_N), s5w)
        kc = _compress_pages(cache, page_table, l, cw)
        y_attn = _nsa_step(q, kvp, kvw, gt, kc, cache, page_table, cwin, l)
        prev = state_conv[l]
        x1, uc = _merge_step(xs, g_attn, y_ssm, y_attn, prev[:, 0], prev[:, 1], mw)
        xs = _moe(x1, g_ffn, mo, gf, bs, final=final)
        kv_s.append(kvp.reshape(bs, 1, 4, KV_HEADS, HEAD_DIM))
        win_s.append(jnp.concatenate([cache_win[l][:, 1:], kvw.reshape(bs, 1, 2, KV_HEADS, HEAD_DIM)], axis=1))
        ssm_s.append(ssm_state(hre, him, bs))
        conv_s.append(jnp.stack([prev[:, 1], uc], axis=1))

    return (xp.reshape(bp, lp, D_MODEL), xs.reshape(bs, 1, D_MODEL),
            jnp.stack(kv_p), jnp.stack(kv_s), jnp.stack(win_p), jnp.stack(win_s),
            jnp.stack(ssm_p), jnp.stack(ssm_s), jnp.stack(conv_p), jnp.stack(conv_s))
```

```python
import functools
import math

import numpy as np
import jax
import jax.numpy as jnp
from jax import lax
from jax.experimental import pallas as pl
from jax.experimental.pallas import tpu as pltpu

F32 = jnp.float32
BF16 = jnp.bfloat16

D_MODEL = 1024
DEPTH = 2
PAGE = 128
SSM_GROUPS = 24
SSM_CH = 16
SSM_DIM = SSM_GROUPS * SSM_CH
SSM_STATE = 64
SSM_N = SSM_GROUPS * SSM_STATE
N_HEADS = 8
HEAD_DIM = 64
KV_HEADS = 2
GRP = N_HEADS // KV_HEADS
ATTN_DIM = N_HEADS * HEAD_DIM
KV_DIM = KV_HEADS * HEAD_DIM
CMP_BLOCK = 32
CMP_HIDDEN = 64
SEL_BLOCK = 64
N_SELECT = 8
WINDOW = 512
FORCE_BONUS = 1e4
CONV_DIM = 384
N_GROUPS = 4
EPG = 4
N_EXPERTS = 16
D_FF = 256
RMS_EPS = 1e-6

C_U = 0
C_Q = C_U + SSM_DIM
C_KVP = C_Q + ATTN_DIM
C_KVW = C_KVP + 4 * KV_DIM
C_GATE = C_KVW + 2 * KV_DIM
C_CONV = C_GATE + 3 * N_HEADS
C_MERGE = C_CONV + 3 * CONV_DIM
N_IN = C_MERGE + 3 * D_MODEL
W2_COLS = N_IN - C_CONV
GATE_PAD = 128
W1_COLS = C_GATE + GATE_PAD

LANES = 128
SUBLANES = 8
NEG = -1e30
M_INIT = -1e29
VMEM_LIMIT = 56 * 1024 * 1024


def _cparams(sem):
    return pltpu.CompilerParams(dimension_semantics=sem, vmem_limit_bytes=VMEM_LIMIT)


def _rms(x, g):
    ms = jnp.mean(x * x, axis=-1, keepdims=True)
    return x * lax.rsqrt(ms + RMS_EPS) * g


def _gelu_tanh(x):
    return 0.5 * x * (1.0 + jnp.tanh(math.sqrt(2.0 / math.pi) * (x + 0.044715 * (x * x * x))))


def _sigmoid(x):
    return 1.0 / (1.0 + jnp.exp(-x))


def _dot(a, b):
    return jnp.dot(a, b, preferred_element_type=F32)


def _dot_nt(a, b):
    return lax.dot_general(a, b, (((1,), (1,)), ((), ())), preferred_element_type=F32)


def _split3(x):
    hi = x.astype(BF16)
    r1 = x - hi.astype(F32)
    mid = r1.astype(BF16)
    lo = (r1 - mid.astype(F32)).astype(BF16)
    return hi, mid, lo


def _mm(a, w_ref, wlo_ref=None):
    ah = a.astype(BF16)
    if wlo_ref is None:
        return _dot(ah, w_ref[...])
    al = (a - ah.astype(F32)).astype(BF16)
    return _dot(ah, w_ref[...]) + (_dot(al, w_ref[...]) + _dot(ah, wlo_ref[...]))


def _hilo(w):
    hi = w.astype(BF16)
    return hi, (w - hi.astype(F32)).astype(BF16)


def _full(shape):
    nd = len(shape)
    return pl.BlockSpec(shape, lambda *_: (0,) * nd)


def _full1(shape):
    nd = len(shape)
    return pl.BlockSpec(shape, lambda *_: (0,) * nd, pipeline_mode=pl.Buffered(1))


def _inproj_body(x_ref, g_ref, w_ref, *refs, precise):
    wlo_ref = refs[0] if precise else None
    u_ref, q_ref, kvp_ref, kvw_ref, gt_ref = refs[1:] if precise else refs
    h = _rms(x_ref[...], g_ref[...])

    def proj(a, b):
        return _mm(h, w_ref.at[:, a:b], wlo_ref.at[:, a:b] if precise else None)

    u_ref[...] = proj(C_U, C_Q)
    q_ref[...] = proj(C_Q, C_KVP)
    kvp_ref[...] = proj(C_KVP, C_KVW)
    kvw_ref[...] = proj(C_KVW, C_GATE)
    gt_ref[...] = proj(C_GATE, W1_COLS)


def _inproj(x, g, w1, tm, w1_lo=None):
    n = x.shape[0]
    widths = (SSM_DIM, ATTN_DIM, 4 * KV_DIM, 2 * KV_DIM, GATE_PAD)
    ws = [w1] if w1_lo is None else [w1, w1_lo]
    return pl.pallas_call(
        functools.partial(_inproj_body, precise=w1_lo is not None),
        grid=(n // tm,),
        in_specs=[
            pl.BlockSpec((tm, D_MODEL), lambda i: (i, 0)),
            _full((1, D_MODEL)),
        ] + [_full((D_MODEL, W1_COLS))] * len(ws),
        out_specs=[pl.BlockSpec((tm, w), lambda i: (i, 0)) for w in widths],
        out_shape=[jax.ShapeDtypeStruct((n, w), F32) for w in widths],
        compiler_params=_cparams(("parallel",)),
        name="inproj",
    )(x, g, *ws)


def _inproj_seq_body(x_ref, g_ref, w_ref, *refs, aliased):
    u_ref, q_ref, gt_ref, kvp_ref, kvw_ref = refs[2:] if aliased else refs
    hb = _rms(x_ref[...], g_ref[...]).astype(BF16)

    def proj(a, b):
        return _dot(hb, w_ref[:, a:b])

    u_ref[...] = proj(C_U, C_Q)
    q_ref[...] = proj(C_Q, C_KVP)
    gt_ref[...] = proj(C_GATE, W1_COLS)
    kvp_ref[...] = proj(C_KVP, C_KVW).T
    kvw_ref[...] = proj(C_KVW, C_GATE).T


def _inproj_seq(x, g, w1, bsz, seq, layer, depth, kv_bufs=None, tm=512):
    nt = seq // tm
    widths = (SSM_DIM, ATTN_DIM, GATE_PAD)
    row = lambda w: pl.BlockSpec((tm, w), lambda b, i: (b * nt + i, 0))
    kv_spec = lambda f: pl.BlockSpec((None, None, f, tm), lambda b, i: (layer, b, 0, i))
    aliased = kv_bufs is not None
    extra = list(kv_bufs) if aliased else []
    any_spec = pl.BlockSpec(memory_space=pl.ANY)
    return pl.pallas_call(
        functools.partial(_inproj_seq_body, aliased=aliased),
        grid=(bsz, nt),
        in_specs=[row(D_MODEL), _full((1, D_MODEL)), _full((D_MODEL, W1_COLS))] + [any_spec] * len(extra),
        out_specs=[row(w) for w in widths] + [kv_spec(4 * KV_DIM), kv_spec(2 * KV_DIM)],
        out_shape=[jax.ShapeDtypeStruct((bsz * seq, w), F32) for w in widths]
        + [jax.ShapeDtypeStruct((depth, bsz, 4 * KV_DIM, seq), F32),
           jax.ShapeDtypeStruct((depth, bsz, 2 * KV_DIM, seq), F32)],
        input_output_aliases={3: 3, 4: 4} if aliased else {},
        compiler_params=_cparams(("parallel", "parallel")),
        name="inproj_seq",
    )(x, g, w1, *extra)


S5_CHUNK = 128
S5_J = S5_CHUNK // SUBLANES


def _s5_tables_body(ar_r, ai_r, ldt_r, ar_c, ai_c, ldt_c, bre_ref, bim_ref,
                    a8re, a8im, apre, apim, a16re, a16im, bbre, bbim):
    dt = jnp.exp(ldt_r[...])
    mag = jnp.exp(ar_r[...] * dt)
    ang = ai_r[...] * dt
    are = mag * jnp.cos(ang)
    aim = mag * jnp.sin(ang)
    a8re[...] = jnp.broadcast_to(are, (SUBLANES, SSM_N))
    a8im[...] = jnp.broadcast_to(aim, (SUBLANES, SSM_N))
    pr, pi = are, aim
    for j in range(S5_J):
        apre[j * SUBLANES:(j + 1) * SUBLANES, :] = jnp.broadcast_to(pr, (SUBLANES, SSM_N))
        apim[j * SUBLANES:(j + 1) * SUBLANES, :] = jnp.broadcast_to(pi, (SUBLANES, SSM_N))
        if j + 1 < S5_J:
            pr, pi = pr * are - pi * aim, pr * aim + pi * are
    sre, sim = pr, pi
    qr, qi = sre, sim
    for s in range(SUBLANES):
        a16re[s:s + 1, :] = qr
        a16im[s:s + 1, :] = qi
        if s + 1 < SUBLANES:
            qr, qi = qr * sre - qi * sim, qr * sim + qi * sre
    dtc = jnp.exp(ldt_c[...])
    arc, aic = ar_c[...], ai_c[...]
    magc = jnp.exp(arc * dtc)
    angc = aic * dtc
    arec = magc * jnp.cos(angc)
    aimc = magc * jnp.sin(angc)
    den = arc * arc + aic * aic
    cre = ((arec - 1.0) * arc + aimc * aic) / den
    cim = (aimc * arc - (arec - 1.0) * aic) / den
    br, bi = bre_ref[...], bim_ref[...]
    bbre[...] = cre * br - cim * bi
    bbim[...] = cre * bi + cim * br


def _s5_tables(a_re, a_im, log_dt, b_re, b_im):
    ar_r = a_re.reshape(1, SSM_N)
    ai_r = a_im.reshape(1, SSM_N)
    ldt_r = jnp.repeat(log_dt, SSM_STATE).reshape(1, SSM_N)
    row = jax.ShapeDtypeStruct((SUBLANES, SSM_N), F32)
    tab = jax.ShapeDtypeStruct((S5_CHUNK, SSM_N), F32)
    col = jax.ShapeDtypeStruct((SSM_N, SSM_CH), F32)
    return pl.pallas_call(
        _s5_tables_body,
        out_shape=[row, row, tab, tab, row, row, col, col],
        name="s5_tables",
    )(ar_r, ai_r, ldt_r, ar_r.reshape(SSM_N, 1), ai_r.reshape(SSM_N, 1), ldt_r.reshape(SSM_N, 1),
      b_re.reshape(SSM_N, SSM_CH), b_im.reshape(SSM_N, SSM_CH))


def _block_diag_b(bb):
    t = bb.reshape(SSM_GROUPS, SSM_STATE, SSM_CH).transpose(0, 2, 1)
    eye = jnp.eye(SSM_GROUPS, dtype=bb.dtype)
    return (t[:, :, None, :] * eye[:, None, :, None]).reshape(SSM_DIM, SSM_N)


def _block_diag_c(c):
    t = c.transpose(0, 2, 1)
    eye = jnp.eye(SSM_GROUPS, dtype=c.dtype)
    return (t[:, :, None, :] * eye[:, None, :, None]).reshape(SSM_N, SSM_DIM)


def _s5_perm():
    p = np.zeros((S5_CHUNK, S5_CHUNK), np.float32)
    for j in range(S5_J):
        for s in range(SUBLANES):
            p[j * SUBLANES + s, S5_J * s + j] = 1.0
    return p


def _s5_epilogue(y, u, d_ref, wglu_ref, wglu_lo, bglu_ref):
    y = y + d_ref[...] * u
    g = _gelu_tanh(y)
    return g * _sigmoid(_mm(g, wglu_ref, wglu_lo) + bglu_ref[...])


def _s5_seq_body(*refs, tl, precise, carry_in):
    it = iter(refs)
    u_ref = next(it)
    h0r_ref, h0i_ref = (next(it), next(it)) if carry_in else (None, None)
    (perm_ref, permt_ref, bmat_ref, cmat_ref, a8re_ref, a8im_ref, apre_ref, apim_ref, a16re_ref, a16im_ref,
     d_ref, wglu_ref, bglu_ref) = (next(it) for _ in range(13))
    bmat_lo, cmat_lo, wglu_lo = (next(it), next(it), next(it)) if precise else (None, None, None)
    y_ref, hre_ref, him_ref, pre_r_ref, pre_i_ref, cre_scr, cim_scr = it
    li = pl.program_id(1)

    @pl.when(li == 0)
    def _():
        if carry_in:
            cre_scr[...] = h0r_ref[...]
            cim_scr[...] = h0i_ref[...]
        else:
            cre_scr[...] = jnp.zeros_like(cre_scr)
            cim_scr[...] = jnp.zeros_like(cim_scr)

    are = a8re_ref[...]
    aim = a8im_ref[...]
    row8 = lax.broadcasted_iota(jnp.int32, (SUBLANES, SSM_N), 0)
    nchunk = tl // S5_CHUNK

    for c in range(nchunk):
        if c == nchunk - 1:
            pre_r_ref[...] = cre_scr[...]
            pre_i_ref[...] = cim_scr[...]
        u = u_ref[c * S5_CHUNK:(c + 1) * S5_CHUNK, :]
        u_hi = u.astype(BF16)
        up = _dot(perm_ref[...], u_hi).astype(BF16)
        bu = _dot(up, bmat_ref[...])
        if precise:
            up_lo = _dot(perm_ref[...], (u - u_hi.astype(F32)).astype(BF16)).astype(BF16)
            bu = bu + (_dot(up_lo, bmat_ref[...]) + _dot(up, bmat_lo[...]))
        hr = [bu[0:SUBLANES, :SSM_N]]
        hi = [bu[0:SUBLANES, SSM_N:]]
        for j in range(1, S5_J):
            br = bu[j * SUBLANES:(j + 1) * SUBLANES, :SSM_N]
            bi = bu[j * SUBLANES:(j + 1) * SUBLANES, SSM_N:]
            hr.append(are * hr[-1] - aim * hi[-1] + br)
            hi.append(are * hi[-1] + aim * hr[-2] + bi)
        er, ei = hr[-1], hi[-1]
        for k, d in enumerate((1, 2, 4)):
            mr = jnp.broadcast_to(a16re_ref[d - 1:d, :], (SUBLANES, SSM_N))
            mi = jnp.broadcast_to(a16im_ref[d - 1:d, :], (SUBLANES, SSM_N))
            sr = jnp.where(row8 >= d, pltpu.roll(er, d, 0), 0.0)
            si = jnp.where(row8 >= d, pltpu.roll(ei, d, 0), 0.0)
            er, ei = er + mr * sr - mi * si, ei + mr * si + mi * sr
        h0r = jnp.broadcast_to(cre_scr[0:1, :], (SUBLANES, SSM_N))
        h0i = jnp.broadcast_to(cim_scr[0:1, :], (SUBLANES, SSM_N))
        p16r, p16i = a16re_ref[...], a16im_ref[...]
        er, ei = er + p16r * h0r - p16i * h0i, ei + p16r * h0i + p16i * h0r
        cinr = jnp.where(row8 == 0, h0r, pltpu.roll(er, 1, 0))
        cini = jnp.where(row8 == 0, h0i, pltpu.roll(ei, 1, 0))
        cre_scr[...] = jnp.broadcast_to(er[SUBLANES - 1:SUBLANES, :], (SUBLANES, SSM_N))
        cim_scr[...] = jnp.broadcast_to(ei[SUBLANES - 1:SUBLANES, :], (SUBLANES, SSM_N))
        fr, fi = [], []
        for j in range(S5_J):
            pr = apre_ref[j * SUBLANES:(j + 1) * SUBLANES, :]
            pi = apim_ref[j * SUBLANES:(j + 1) * SUBLANES, :]
            fr.append(hr[j] + pr * cinr - pi * cini)
            fi.append(hi[j] + pr * cini + pi * cinr)
        hfull = jnp.concatenate([jnp.concatenate(fr, axis=0), jnp.concatenate(fi, axis=0)], axis=1)
        yp = _mm(hfull, cmat_ref, cmat_lo)
        y_hi, y_mid, y_lo = _split3(yp)
        pt = permt_ref[...]
        y = _dot(pt, y_hi) + _dot(pt, y_mid) + _dot(pt, y_lo)
        y_ref[c * S5_CHUNK:(c + 1) * S5_CHUNK, :] = _s5_epilogue(y, u, d_ref, wglu_ref, wglu_lo, bglu_ref)

    hre_ref[...] = cre_scr[...]
    him_ref[...] = cim_scr[...]


def _s5_seq(u, s5w, bsz, seq, tl=512, precise=False, h0=None):
    nt = seq // tl
    consts = list(s5w["tabs"]) + [s5w["d"], s5w["wglu"], s5w["bglu"]]
    if precise:
        consts += [s5w["bmat_lo"], s5w["cmat_lo"], s5w["wglu_lo"]]
    state_spec = pl.BlockSpec((None, SUBLANES, SSM_N), lambda b, i: (b, 0, 0))
    state_shape = jax.ShapeDtypeStruct((bsz, SUBLANES, SSM_N), F32)
    carry = [] if h0 is None else list(h0)
    return pl.pallas_call(
        functools.partial(_s5_seq_body, tl=tl, precise=precise, carry_in=h0 is not None),
        grid=(bsz, nt),
        in_specs=[pl.BlockSpec((tl, SSM_DIM), lambda b, i: (b * nt + i, 0))]
        + [state_spec] * len(carry) + [_full(c.shape) for c in consts],
        out_specs=[pl.BlockSpec((tl, SSM_DIM), lambda b, i: (b * nt + i, 0))] + [state_spec] * 4,
        out_shape=[jax.ShapeDtypeStruct((bsz * seq, SSM_DIM), F32)] + [state_shape] * 4,
        scratch_shapes=[pltpu.VMEM((SUBLANES, SSM_N), F32), pltpu.VMEM((SUBLANES, SSM_N), F32)],
        compiler_params=_cparams(("parallel", "arbitrary")),
        name="s5_seq",
    )(u, *carry, *consts)


def _s5_step_body(u_ref, h0r_ref, h0i_ref, bmat_ref, cmat_ref, a8re_ref, a8im_ref,
                  d_ref, wglu_ref, bglu_ref, bmat_lo, cmat_lo, wglu_lo, y_ref, hre_ref, him_ref):
    u = u_ref[...]
    bu = _mm(u, bmat_ref, bmat_lo)
    are = a8re_ref[0:1, :]
    aim = a8im_ref[0:1, :]
    h0r, h0i = h0r_ref[...], h0i_ref[...]
    hr = are * h0r - aim * h0i + bu[:, :SSM_N]
    hi = are * h0i + aim * h0r + bu[:, SSM_N:]
    hre_ref[...] = hr
    him_ref[...] = hi
    y = _mm(jnp.concatenate([hr, hi], axis=1), cmat_ref, cmat_lo)
    y_ref[...] = _s5_epilogue(y, u, d_ref, wglu_ref, wglu_lo, bglu_ref)


def _s5_step(u, h0r, h0i, s5w):
    _, _, bmat, cmat, a8re, a8im = s5w["tabs"][:6]
    n = u.shape[0]
    return pl.pallas_call(
        _s5_step_body,
        out_shape=[
            jax.ShapeDtypeStruct((n, SSM_DIM), F32),
            jax.ShapeDtypeStruct((n, SSM_N), F32),
            jax.ShapeDtypeStruct((n, SSM_N), F32),
        ],
        compiler_params=pltpu.CompilerParams(vmem_limit_bytes=VMEM_LIMIT),
        name="s5_step",
    )(u, h0r, h0i, bmat, cmat, a8re, a8im, s5w["d"], s5w["wglu"], s5w["bglu"],
      s5w["bmat_lo"], s5w["cmat_lo"], s5w["wglu_lo"])


def _s5_prepare(a_re, a_im, log_dt, b_re, b_im, c_re, c_im, d_skip, w_glu, b_glu):
    a8re, a8im, apre, apim, a16re, a16im, bbre, bbim = _s5_tables(a_re, a_im, log_dt, b_re, b_im)
    bmat, bmat_lo = _hilo(jnp.concatenate([_block_diag_b(bbre), _block_diag_b(bbim)], axis=1))
    cmat, cmat_lo = _hilo(jnp.concatenate([_block_diag_c(c_re), -_block_diag_c(c_im)], axis=0))
    wglu, wglu_lo = _hilo(w_glu)
    perm = _s5_perm()
    tabs = (jnp.asarray(perm, BF16), jnp.asarray(perm.T, BF16), bmat, cmat, a8re, a8im, apre, apim, a16re, a16im)
    return dict(tabs=tabs, d=d_skip.reshape(1, SSM_DIM), wglu=wglu, bglu=b_glu.reshape(1, SSM_DIM),
                bmat_lo=bmat_lo, cmat_lo=cmat_lo, wglu_lo=wglu_lo)


TAIL = 128

MERGE_W = ("w2", "wbs", "wba", "wbc", "wo")


def _merge_weights(w2, w_br_ssm, w_br_attn, w_br_conv, w_out, conv_w):
    mw = {"cw": conv_w}
    for name, w in zip(MERGE_W, (w2, w_br_ssm, w_br_attn, w_br_conv, w_out)):
        mw[name], mw[name + "_lo"] = _hilo(w)
    return mw


def _merge_wlist(mw, precise):
    return [mw["cw"]] + [mw[n] for n in MERGE_W] + ([mw[n + "_lo"] for n in MERGE_W] if precise else [])


def _merge_core(x, g_ref, wrefs, precise, up0, up1, ys_ref, ya_ref):
    cw_ref, w2_ref, wbs_ref, wba_ref, wbc_ref, wo_ref = wrefs[:6]
    w2_lo, wbs_lo, wba_lo, wbc_lo, wo_lo = wrefs[6:] if precise else (None,) * 5
    zc = _mm(_rms(x, g_ref[...]), w2_ref, w2_lo)
    conv_b = zc[:, 0:CONV_DIM]
    uc = zc[:, CONV_DIM:2 * CONV_DIM] * zc[:, 2 * CONV_DIM:3 * CONV_DIM]
    yc = conv_b * (cw_ref[0:1, :] * up0(uc) + cw_ref[1:2, :] * up1(uc) + cw_ref[2:3, :] * uc)
    g0 = 3 * CONV_DIM
    g_ssm = _sigmoid(zc[:, g0:g0 + D_MODEL])
    g_attn = _sigmoid(zc[:, g0 + D_MODEL:g0 + 2 * D_MODEL])
    g_conv = _sigmoid(zc[:, g0 + 2 * D_MODEL:g0 + 3 * D_MODEL])
    merged = (g_ssm * _mm(ys_ref[...], wbs_ref, wbs_lo)
              + g_attn * _mm(ya_ref[...], wba_ref, wba_lo)
              + g_conv * _mm(yc, wbc_ref, wbc_lo))
    return x + _mm(merged, wo_ref, wo_lo), uc


def _merge_seq_body(*refs, tm, precise, carry_in):
    it = iter(refs)
    x_ref, g_ref, ys_ref, ya_ref = (next(it) for _ in range(4))
    cin_ref = next(it) if carry_in else None
    wrefs = [next(it) for _ in range(11 if precise else 6)]
    o_ref, cl_ref, pre_ref, stage = it
    ti = pl.program_id(1)

    @pl.when(ti == 0)
    def _():
        stage[0:SUBLANES, :] = cin_ref[...] if carry_in else jnp.zeros((SUBLANES, CONV_DIM), F32)

    def up0(uc):
        stage[SUBLANES:SUBLANES + tm, :] = uc
        return stage[SUBLANES - 2:SUBLANES - 2 + tm, :]

    def up1(uc):
        return stage[SUBLANES - 1:SUBLANES - 1 + tm, :]

    out, uc = _merge_core(x_ref[...], g_ref, wrefs, precise, up0, up1, ys_ref, ya_ref)
    o_ref[...] = out
    last = uc[tm - SUBLANES:tm, :]
    stage[0:SUBLANES, :] = last
    cl_ref[...] = last
    pre_ref[...] = uc[max(tm - TAIL, SUBLANES) - SUBLANES:max(tm - TAIL, SUBLANES), :]


def _merge_seq(x, g, ys, ya, mw, bsz, seq, tm=256, precise=False, cin=None):
    nt = seq // tm
    ws = _merge_wlist(mw, precise)
    row = lambda w: pl.BlockSpec((tm, w), lambda b, i: (b * nt + i, 0))
    cspec = pl.BlockSpec((None, SUBLANES, CONV_DIM), lambda b, i: (b, 0, 0))
    carry = [] if cin is None else [cin]
    return pl.pallas_call(
        functools.partial(_merge_seq_body, tm=tm, precise=precise, carry_in=cin is not None),
        grid=(bsz, nt),
        in_specs=[row(D_MODEL), _full(g.shape), row(SSM_DIM), row(ATTN_DIM)] + [cspec] * len(carry)
        + [_full1(w.shape) for w in ws],
        out_specs=[row(D_MODEL), cspec, cspec],
        out_shape=[jax.ShapeDtypeStruct((bsz * seq, D_MODEL), F32)]
        + [jax.ShapeDtypeStruct((bsz, SUBLANES, CONV_DIM), F32)] * 2,
        scratch_shapes=[pltpu.VMEM((tm + SUBLANES, CONV_DIM), F32)],
        compiler_params=_cparams(("parallel", "arbitrary")),
        name="merge_seq",
    )(x, g, ys, ya, *carry, *ws)


def _merge_step_body(x_ref, g_ref, ys_ref, ya_ref, p0_ref, p1_ref, *refs):
    wrefs, (o_ref, uc_ref) = refs[:11], refs[11:]
    out, uc = _merge_core(x_ref[...], g_ref, wrefs, True, lambda _: p0_ref[...], lambda _: p1_ref[...],
                          ys_ref, ya_ref)
    o_ref[...] = out
    uc_ref[...] = uc


def _merge_step(x, g, ys, ya, prev0, prev1, mw):
    n = x.shape[0]
    return pl.pallas_call(
        _merge_step_body,
        out_shape=[jax.ShapeDtypeStruct((n, D_MODEL), F32), jax.ShapeDtypeStruct((n, CONV_DIM), F32)],
        compiler_params=pltpu.CompilerParams(vmem_limit_bytes=VMEM_LIMIT),
        name="merge_step",
    )(x, g, ys, ya, prev0, prev1, *_merge_wlist(mw, True))


ROUTE_E0 = SUBLANES


def _route_math(lg, le):
    gmax = functools.reduce(jnp.maximum, lg)
    gsum = functools.reduce(lambda a, b: a + b, [jnp.exp(v - gmax) for v in lg])
    gw = 1.0 / gsum
    gsel = jnp.full_like(gmax, N_GROUPS - 1).astype(jnp.int32)
    for k in range(N_GROUPS - 2, -1, -1):
        gsel = jnp.where(lg[k] == gmax, k, gsel)
    ls = []
    for j in range(EPG):
        v = le[j]
        for k in range(1, N_GROUPS):
            v = jnp.where(gsel == k, le[k * EPG + j], v)
        ls.append(v)
    emax = functools.reduce(jnp.maximum, ls)
    ex = [jnp.exp(v - emax) for v in ls]
    esum = functools.reduce(lambda a, b: a + b, ex)
    pe = [v / esum for v in ex]
    v1 = functools.reduce(jnp.maximum, pe)
    i1 = jnp.full_like(gsel, EPG - 1)
    for j in range(EPG - 2, -1, -1):
        i1 = jnp.where(pe[j] == v1, j, i1)
    pe2 = [jnp.where(i1 == j, -1.0, pe[j]) for j in range(EPG)]
    v2 = functools.reduce(jnp.maximum, pe2)
    i2 = jnp.full_like(gsel, EPG - 1)
    for j in range(EPG - 2, -1, -1):
        i2 = jnp.where(pe2[j] == v2, j, i2)
    tot = v1 + v2
    w1 = v1 / tot * gw
    w2 = v2 / tot * gw
    return gsel * EPG + i1, gsel * EPG + i2, w1, w2


def _moe_route(h, wr_ref, wrt_ref, br_ref, brt_ref):
    tm = h.shape[0]
    h_hi = h.astype(BF16)
    h_lo = (h - h_hi.astype(F32)).astype(BF16)
    if tm % LANES:
        logits = _dot(h_hi, wr_ref[0]) + (_dot(h_lo, wr_ref[0]) + _dot(h_hi, wr_ref[1])) + br_ref[...]
        lg = [logits[:, k:k + 1] for k in range(N_GROUPS)]
        le = [logits[:, ROUTE_E0 + e:ROUTE_E0 + e + 1] for e in range(N_EXPERTS)]
        e1, e2, w1, w2 = _route_math(lg, le)
        lane = lax.broadcasted_iota(jnp.int32, (tm, LANES), 1)
        return jnp.where(lane == e1, w1, 0.0) + jnp.where(lane == e2, w2, 0.0)
    lt = _dot_nt(wrt_ref[0], h_hi) + (_dot_nt(wrt_ref[0], h_lo) + _dot_nt(wrt_ref[1], h_hi)) + brt_ref[...]
    lg = [lt[k:k + 1, :] for k in range(N_GROUPS)]
    le = [lt[ROUTE_E0 + e:ROUTE_E0 + e + 1, :] for e in range(N_EXPERTS)]
    e1, e2, w1, w2 = _route_math(lg, le)
    rows = [jnp.where(e1 == e, w1, 0.0) + jnp.where(e2 == e, w2, 0.0) for e in range(N_EXPERTS)]
    comb_t = jnp.concatenate(rows + [jnp.zeros((LANES - N_EXPERTS, tm), F32)], axis=0)
    return comb_t.T


def _moe_body(x_ref, g_ref, wr_ref, wrt_ref, br_ref, brt_ref, wgu_ref, wd_ref, gf_ref, o_ref,
              hb_scr, comb_scr, acc_scr, *, final):
    e = pl.program_id(1)

    @pl.when(e == 0)
    def _():
        h = _rms(x_ref[...], g_ref[...])
        hb_scr[...] = h.astype(BF16)
        comb_scr[...] = _moe_route(h, wr_ref, wrt_ref, br_ref, brt_ref)
        acc_scr[...] = jnp.zeros_like(acc_scr)

    hgu = _dot(hb_scr[...], wgu_ref[0])
    hg = hgu[:, :D_FF]
    lane = lax.broadcasted_iota(jnp.int32, comb_scr.shape, 1)
    ce = jnp.sum(jnp.where(lane == e, comb_scr[...], 0.0), axis=1, keepdims=True)
    act = hg * _sigmoid(hg) * hgu[:, D_FF:] * ce
    acc_scr[...] += _dot(act.astype(BF16), wd_ref[0])

    @pl.when(e == N_EXPERTS - 1)
    def _():
        y = x_ref[...] + acc_scr[...]
        o_ref[...] = _rms(y, gf_ref[...]) if final else y


def _moe(x, g, mo, gf, tm, final=False):
    n = x.shape[0]
    wr, wrt, br, brt, wgu, wd = (mo[k] for k in ("wr", "wrt", "br", "brt", "wgu", "wd"))
    return pl.pallas_call(
        functools.partial(_moe_body, final=final),
        grid=(n // tm, N_EXPERTS),
        in_specs=[
            pl.BlockSpec((tm, D_MODEL), lambda i, e: (i, 0)),
            _full(g.shape), _full(wr.shape), _full(wrt.shape), _full(br.shape), _full(brt.shape),
            pl.BlockSpec((1, D_MODEL, 2 * D_FF), lambda i, e: (e, 0, 0)),
            pl.BlockSpec((1, D_FF, D_MODEL), lambda i, e: (e, 0, 0)),
            _full(gf.shape),
        ],
        out_specs=pl.BlockSpec((tm, D_MODEL), lambda i, e: (i, 0)),
        out_shape=jax.ShapeDtypeStruct((n, D_MODEL), F32),
        scratch_shapes=[pltpu.VMEM((tm, D_MODEL), BF16), pltpu.VMEM((tm, LANES), F32),
                        pltpu.VMEM((tm, D_MODEL), F32)],
        compiler_params=_cparams(("parallel", "arbitrary")),
        name="moe",
    )(x, g, wr, wrt, br, brt, wgu, wd, gf)


def _moe_prepare(w_rg, b_rg, w_re, b_re, w_gate, w_up, w_down):
    wr = jnp.zeros((D_MODEL, LANES), F32)
    wr = wr.at[:, :N_GROUPS].set(w_rg).at[:, ROUTE_E0:ROUTE_E0 + N_EXPERTS].set(w_re)
    br = jnp.zeros((1, LANES), F32)
    br = br.at[0, :N_GROUPS].set(b_rg).at[0, ROUTE_E0:ROUTE_E0 + N_EXPERTS].set(b_re)
    wr2 = jnp.stack(_hilo(wr))
    return dict(wr=wr2, wrt=wr2.transpose(0, 2, 1), br=br, brt=br.reshape(LANES, 1),
                wgu=jnp.concatenate([w_gate, w_up], axis=-1).astype(BF16),
                wd=w_down.astype(BF16))


CMP_SUB = 256
CMP_W = 4 * HEAD_DIM
PAGES_PER_STEP = 64


def _cmp_perm():
    p = np.zeros((CMP_SUB, CMP_SUB), np.float32)
    for i in range(CMP_SUB):
        p[i, CMP_BLOCK * (i % SUBLANES) + i // SUBLANES] = 1.0
    return p


def _cmp_prepare(pe, w1, w2):
    eye4 = jnp.eye(4, dtype=F32)
    sel = jnp.array([0, 0, 1, 1])
    w1r = w1.reshape(2, CMP_BLOCK, HEAD_DIM, CMP_HIDDEN)[sel]
    w1bd = (w1r.transpose(1, 0, 2, 3)[:, :, :, None, :] * eye4[None, :, None, :, None])
    w1bd = w1bd.reshape(CMP_BLOCK, CMP_W, 4 * CMP_HIDDEN).astype(BF16)
    w2r = w2[sel]
    w2bd = (w2r[:, :, None, :] * eye4[:, None, :, None]).reshape(4 * CMP_HIDDEN, CMP_W).astype(BF16)
    pe4 = pe[sel].transpose(1, 0, 2).reshape(CMP_BLOCK, CMP_W)
    pe_exp = jnp.repeat(pe4, SUBLANES, axis=0)
    return jnp.asarray(_cmp_perm(), BF16), pe_exp, w1bd, w2bd


def _compress_rows(get_sub, nsub, perm_ref, pe_ref, w1_ref, w2_ref, stage, transposed=False):
    mm = _dot_nt if transposed else _dot
    for t in range(nsub):
        xp = mm(perm_ref[...], get_sub(t).astype(BF16))
        stage[t] = xp + pe_ref[...]
    acc = jnp.zeros((nsub * SUBLANES, 4 * CMP_HIDDEN), F32)
    for r in range(CMP_BLOCK):
        a = stage[:, r * SUBLANES:(r + 1) * SUBLANES, :].reshape(nsub * SUBLANES, CMP_W)
        acc = acc + _dot(a.astype(BF16), w1_ref[r])
    return _dot(_gelu_tanh(acc).astype(BF16), w2_ref[...])


def _compress_seq_body(x_ref, perm_ref, pe_ref, w1_ref, w2_ref, o_ref, stage, *, nsub):
    o_ref[...] = _compress_rows(lambda t: x_ref[:, t * CMP_SUB:(t + 1) * CMP_SUB], nsub,
                                perm_ref, pe_ref, w1_ref, w2_ref, stage, transposed=True)


def _compress_seq(kvp_t, layer, cw, bsz, seq):
    nsub = seq // CMP_SUB
    nblk = seq // CMP_BLOCK
    return pl.pallas_call(
        functools.partial(_compress_seq_body, nsub=nsub),
        grid=(bsz,),
        in_specs=[pl.BlockSpec((None, None, CMP_W, seq), lambda b: (layer, b, 0, 0))]
        + [_full(c.shape) for c in cw],
        out_specs=pl.BlockSpec((None, nblk, CMP_W), lambda b: (b, 0, 0)),
        out_shape=jax.ShapeDtypeStruct((bsz, nblk, CMP_W), F32),
        scratch_shapes=[pltpu.VMEM((nsub, CMP_SUB, CMP_W), F32)],
        compiler_params=_cparams(("parallel",)),
        name="compress_seq",
    )(kvp_t, *cw)


def _compress_pages_body(pt_ref, *refs):
    pages = refs[:PAGES_PER_STEP]
    perm_ref, pe_ref, w1_ref, w2_ref, o_ref, stage = refs[PAGES_PER_STEP:]
    per = CMP_SUB // PAGE

    def get_sub(t):
        return jnp.concatenate([pages[per * t + k][...] for k in range(per)], axis=1)

    o_ref[...] = _compress_rows(get_sub, PAGES_PER_STEP // per, perm_ref, pe_ref, w1_ref, w2_ref, stage,
                                transposed=True)


def _compress_pages(cache_t, page_table, layer, cw):
    bsz, n_pages = page_table.shape
    steps = n_pages // PAGES_PER_STEP
    nsub = PAGES_PER_STEP * PAGE // CMP_SUB
    blk_per_step = PAGES_PER_STEP * PAGE // CMP_BLOCK

    def page_spec(k):
        return pl.BlockSpec((None, None, CMP_W, PAGE),
                            lambda b, i, pt: (layer, pt[b * n_pages + i * PAGES_PER_STEP + k], 0, 0))

    grid_spec = pltpu.PrefetchScalarGridSpec(
        num_scalar_prefetch=1,
        grid=(bsz, steps),
        in_specs=[page_spec(k) for k in range(PAGES_PER_STEP)]
        + [pl.BlockSpec(c.shape, lambda b, i, pt, nd=c.ndim: (0,) * nd) for c in cw],
        out_specs=pl.BlockSpec((None, blk_per_step, CMP_W), lambda b, i, pt: (b, i, 0)),
        scratch_shapes=[pltpu.VMEM((nsub, CMP_SUB, CMP_W), F32)],
    )
    return pl.pallas_call(
        _compress_pages_body,
        grid_spec=grid_spec,
        out_shape=jax.ShapeDtypeStruct((bsz, n_pages * PAGE // CMP_BLOCK, CMP_W), F32),
        compiler_params=_cparams(("parallel", "arbitrary")),
        name="compress_pages",
    )(page_table.reshape(-1), *([cache_t] * PAGES_PER_STEP), *cw)


TQ = 128
SLC_CHUNK = 512
QL = GRP * TQ
N_SELBLK_SEQ = 32
AUG_MASK = HEAD_DIM
AUG_POS = HEAD_DIM + 32
POS_SPLIT = 128


def _alibi_slopes():
    return [2.0 ** (-8.0 * (h + 1) / N_HEADS) for h in range(N_HEADS)]


def _nsa_tables(seq):
    pos = np.arange(seq)
    tbl = np.zeros((seq, HEAD_DIM), np.float32)
    tbl[pos, pos // SEL_BLOCK] = 1.0
    tbl[:, 32] = (pos // POS_SPLIT) * POS_SPLIT
    tbl[:, 33] = pos % POS_SPLIT
    ncmp = seq // CMP_BLOCK
    order = np.concatenate([np.arange(0, ncmp, 2), np.arange(1, ncmp, 2)])
    cend = (order + 1) * CMP_BLOCK - 1
    ctbl = np.zeros((ncmp, HEAD_DIM), np.float32)
    ctbl[:, 32] = (cend // POS_SPLIT) * POS_SPLIT
    ctbl[:, 33] = cend % POS_SPLIT
    pm = np.zeros((ncmp, ncmp), np.float32)
    pm[np.arange(ncmp), order] = 1.0
    slope = np.zeros((KV_HEADS, 16, QL), np.float32)
    sl = _alibi_slopes()
    for g in range(KV_HEADS):
        for m in range(GRP):
            slope[g, 0:2, m * TQ:(m + 1) * TQ] = sl[g * GRP + m]
    return (jnp.asarray(tbl), jnp.asarray(ctbl), jnp.asarray(cend.reshape(ncmp, 1).astype(np.int32)),
            jnp.asarray(pm, BF16), jnp.asarray(slope, BF16))


def _flash_chunk(kaug, vt, qa, valid, state):
    m, l, acc = state
    s = _dot(kaug, qa)
    if valid is not None:
        s = jnp.where(valid, s, NEG)
    mn = jnp.maximum(m, jnp.max(s, axis=0, keepdims=True))
    alpha = jnp.exp(m - mn)
    p = jnp.exp(s - mn)
    l = alpha * l + jnp.sum(p, axis=0, keepdims=True)
    acc = alpha * acc + _dot(vt, p.astype(BF16))
    return mn, l, acc


def _flash_init():
    return (jnp.full((1, QL), M_INIT, F32), jnp.zeros((1, QL), F32), jnp.zeros((HEAD_DIM, QL), F32))


def _nsa_seq_body(q_ref, kvp_ref, kvw_ref, gt_ref, kc_ref, tbl_ref, ctbl_ref, cend_ref, pm_ref,
                  slope_ref, y_ref, kslc, kwin, vslc, vwin, qaug):
    i = pl.program_id(1)
    l0 = i * TQ

    @pl.when(i == 0)
    def _():
        kslc[...] = jnp.zeros_like(kslc)
        kwin[...] = jnp.zeros_like(kwin)
        vslc[...] = jnp.zeros_like(vslc)
        vwin[...] = jnp.zeros_like(vwin)

    q = q_ref[...]
    tbl = tbl_ref[...]
    lane64 = lax.broadcasted_iota(jnp.int32, (TQ, HEAD_DIM), 1)
    tblw = jnp.where(lane64 < 32, 0.0, tbl)
    rows = pl.ds(pl.multiple_of(l0, TQ), TQ)
    ks_all = kvp_ref[0:KV_DIM, :].T
    kw_all = kvw_ref[0:KV_DIM, :].T
    for g in range(KV_HEADS):
        ks = ks_all[:, g * HEAD_DIM:(g + 1) * HEAD_DIM]
        kslc[g, rows, :] = jnp.concatenate([ks, tbl], axis=1).astype(BF16)
        kw = kw_all[:, g * HEAD_DIM:(g + 1) * HEAD_DIM]
        kwin[g, rows, :] = jnp.concatenate([kw, tblw], axis=1).astype(BF16)
    vslc[i] = kvp_ref[KV_DIM:2 * KV_DIM, :].astype(BF16)
    vwin[i] = kvw_ref[KV_DIM:2 * KV_DIM, :].astype(BF16)

    scale = HEAD_DIM ** -0.5
    qt = [(q[:, j * LANES:(j + 1) * LANES] * scale).T for j in range(ATTN_DIM // LANES)]
    for g in range(KV_HEADS):
        heads = []
        for m in range(GRP):
            h = g * GRP + m
            heads.append(qt[h // 2][(h % 2) * HEAD_DIM:(h % 2 + 1) * HEAD_DIM, :])
        qaug[g, 0:HEAD_DIM, :] = jnp.concatenate(heads, axis=1).astype(BF16)
        qaug[g, AUG_MASK:AUG_POS, :] = jnp.zeros((AUG_POS - AUG_MASK, QL), BF16)
        qaug[g, AUG_POS:AUG_POS + 16, :] = slope_ref[g]
        qaug[g, AUG_POS + 16:, :] = jnp.zeros((2 * HEAD_DIM - AUG_POS - 16, QL), BF16)

    ncmp = kc_ref.shape[0]
    nsel = ncmp // 2
    kcp = _dot(pm_ref[...], kc_ref[...].astype(BF16))
    vct = jnp.concatenate([kcp[:, 2 * KV_DIM - KV_DIM:2 * KV_DIM],
                           jnp.zeros((LANES - ncmp, KV_DIM), F32)], axis=0).T
    lpos = l0 + lax.broadcasted_iota(jnp.int32, (1, QL), 1) % TQ
    valid_c = cend_ref[...] <= lpos
    lq = l0 + lax.broadcasted_iota(jnp.int32, (nsel, TQ), 1)
    blk = lax.broadcasted_iota(jnp.int32, (nsel, TQ), 0)
    cur = lq // SEL_BLOCK
    forced = (blk == 0) | (blk == cur) | (blk == cur - 1)
    causal_blk = blk * SEL_BLOCK <= lq
    o_cmp = []
    for g in range(KV_HEADS):
        kca = jnp.concatenate([kcp[:, g * HEAD_DIM:(g + 1) * HEAD_DIM], ctbl_ref[...]], axis=1).astype(BF16)
        s = jnp.where(valid_c, _dot(kca, qaug[g]), NEG)
        mx = jnp.max(s, axis=0, keepdims=True)
        e = jnp.where(valid_c, jnp.exp(s - mx), 0.0)
        p = e * (1.0 / jnp.maximum(jnp.sum(e, axis=0, keepdims=True), 1.0))
        vt_g = vct[g * HEAD_DIM:(g + 1) * HEAD_DIM, 0:ncmp].astype(BF16)
        o_cmp.append(_dot(vt_g, p.astype(BF16)))
        psum = p[:, 0:TQ]
        for m in range(1, GRP):
            psum = psum + p[:, m * TQ:(m + 1) * TQ]
        imp = psum[0:nsel, :] + psum[nsel:, :]
        score = jnp.where(causal_blk, imp + jnp.where(forced, FORCE_BONUS, 0.0), NEG)
        sel = jnp.zeros((nsel, TQ), jnp.bool_)
        for _ in range(N_SELECT):
            best = jnp.max(score, axis=0, keepdims=True)
            idx = jnp.min(jnp.where(score == best, blk, nsel), axis=0, keepdims=True)
            hit = (blk == idx) & (best > 0.5 * NEG)
            sel = sel | hit
            score = jnp.where(hit, 2.0 * NEG, score)
        mb = jnp.where(sel, 0.0, NEG)
        qaug[g, AUG_MASK:AUG_POS, :] = jnp.concatenate([mb] * GRP, axis=1).astype(BF16)

    qpos = l0 + lax.broadcasted_iota(jnp.int32, (1, QL), 1) % TQ
    per = SLC_CHUNK // TQ

    def slc_chunk(cb, st, valid):
        k0 = pl.multiple_of(cb * SLC_CHUNK, SLC_CHUNK)
        out = []
        for g in range(KV_HEADS):
            vt = jnp.concatenate([vslc[cb * per + k, g * HEAD_DIM:(g + 1) * HEAD_DIM, :] for k in range(per)], axis=1)
            out.append(_flash_chunk(kslc[g, pl.ds(k0, SLC_CHUNK), :], vt, qaug[g], valid, st[g]))
        return tuple(out)

    cbd = i // per
    st = lax.fori_loop(0, cbd, lambda cb, s: slc_chunk(cb, s, None), tuple(_flash_init() for _ in range(KV_HEADS)))
    kpos = cbd * SLC_CHUNK + lax.broadcasted_iota(jnp.int32, (SLC_CHUNK, 1), 0)
    st = slc_chunk(cbd, st, kpos <= qpos)
    o_slc = [acc * (1.0 / l) for (m, l, acc) in st]

    nwc = WINDOW // TQ + 1
    cw0 = jnp.maximum(i - (nwc - 1), 0)
    kw0 = pl.multiple_of(cw0 * TQ, TQ)
    dist = qpos - (kw0 + lax.broadcasted_iota(jnp.int32, (nwc * TQ, 1), 0))
    vis = (dist >= 0) & (dist < WINDOW)
    o_win = []
    for g in range(KV_HEADS):
        vt = jnp.concatenate([vwin[cw0 + k, g * HEAD_DIM:(g + 1) * HEAD_DIM, :] for k in range(nwc)], axis=1)
        m, l, acc = _flash_chunk(kwin[g, pl.ds(kw0, nwc * TQ), :], vt, qaug[g], vis, _flash_init())
        o_win.append(acc * (1.0 / l))

    sg = _sigmoid(gt_ref[...].T[0:3 * N_HEADS, :])
    outs = []
    for g in range(KV_HEADS):
        def gate_row(br):
            return jnp.concatenate([sg[br * N_HEADS + g * GRP + m:br * N_HEADS + g * GRP + m + 1, :]
                                    for m in range(GRP)], axis=1)
        outs.append(gate_row(0) * o_cmp[g] + gate_row(1) * o_slc[g] + gate_row(2) * o_win[g])
    for j in range(ATTN_DIM // LANES):
        rows = []
        for h in (2 * j, 2 * j + 1):
            g, m = divmod(h, GRP)
            rows.append(outs[g][:, m * TQ:(m + 1) * TQ])
        y_ref[:, j * LANES:(j + 1) * LANES] = jnp.concatenate(rows, axis=0).T


def _nsa_seq(q, kvp_t, kvw_t, layer, gt, kc, bsz, seq):
    assert seq % SLC_CHUNK == 0 and seq >= WINDOW + TQ
    assert seq // SEL_BLOCK == AUG_POS - AUG_MASK
    nt = seq // TQ
    tbl, ctbl, cend, pm, slope = _nsa_tables(seq)
    ncmp = seq // CMP_BLOCK
    row = lambda w: pl.BlockSpec((TQ, w), lambda b, i: (b * nt + i, 0))
    return pl.pallas_call(
        _nsa_seq_body,
        grid=(bsz, nt),
        in_specs=[row(ATTN_DIM),
                  pl.BlockSpec((None, None, 2 * KV_DIM, TQ), lambda b, i: (layer, b, 1, i)),
                  pl.BlockSpec((None, None, 2 * KV_DIM, TQ), lambda b, i: (layer, b, 0, i)),
                  row(GATE_PAD),
                  pl.BlockSpec((None, ncmp, CMP_W), lambda b, i: (b, 0, 0)),
                  pl.BlockSpec((TQ, HEAD_DIM), lambda b, i: (i, 0)),
                  _full(ctbl.shape), _full(cend.shape), _full(pm.shape), _full(slope.shape)],
        out_specs=row(ATTN_DIM),
        out_shape=jax.ShapeDtypeStruct((bsz * seq, ATTN_DIM), F32),
        scratch_shapes=[pltpu.VMEM((KV_HEADS, seq, 2 * HEAD_DIM), BF16),
                        pltpu.VMEM((KV_HEADS, seq, 2 * HEAD_DIM), BF16),
                        pltpu.VMEM((nt, KV_DIM, TQ), BF16),
                        pltpu.VMEM((nt, KV_DIM, TQ), BF16),
                        pltpu.VMEM((KV_HEADS, 2 * HEAD_DIM, QL), BF16)],
        compiler_params=_cparams(("parallel", "arbitrary")),
        name="nsa_seq",
    )(q, kvp_t, kvw_t, gt, kc, tbl, ctbl, cend, pm, slope)


N_PICK = N_SELECT - 1


def _q_rows(q, g, scale):
    rows = [q[:, (g * GRP + m) * HEAD_DIM:(g * GRP + m + 1) * HEAD_DIM] for m in range(GRP)]
    return jnp.concatenate(rows + [jnp.zeros((SUBLANES - GRP, HEAD_DIM), F32)], axis=0) * scale


def _slope_col(g):
    sl = _alibi_slopes()
    row = lax.broadcasted_iota(jnp.int32, (SUBLANES, 1), 0)
    col = jnp.zeros((SUBLANES, 1), F32)
    for m in range(GRP):
        col = jnp.where(row == m, sl[g * GRP + m], col)
    return col


def _nsa_step_cmp_body(q_ref, kc_ref, pair_ref, o_ref, idx_ref, *, past_len):
    q = q_ref[...]
    kc = kc_ref[...]
    ncmp = kc.shape[0]
    nsel = ncmp // 2
    cur = past_len // SEL_BLOCK
    scale = HEAD_DIM ** -0.5
    cend = (lax.broadcasted_iota(jnp.int32, (1, ncmp), 1) + 1) * CMP_BLOCK - 1
    dist = (past_len - cend).astype(F32)
    lane = lax.broadcasted_iota(jnp.int32, (1, nsel), 1)
    forced = (lane == 0) | (lane == cur) | (lane == cur - 1)
    lane_o = lax.broadcasted_iota(jnp.int32, (1, LANES), 1)
    idx_out = jnp.zeros((1, LANES), jnp.int32)
    for g in range(KV_HEADS):
        q8 = _q_rows(q, g, scale).astype(BF16)
        s = _dot_nt(q8, kc[:, g * HEAD_DIM:(g + 1) * HEAD_DIM].astype(BF16)) - _slope_col(g) * dist
        mx = jnp.max(s, axis=1, keepdims=True)
        e = jnp.exp(s - mx)
        p = e / jnp.maximum(jnp.sum(e, axis=1, keepdims=True), 1.0)
        vg = kc[:, KV_DIM + g * HEAD_DIM:KV_DIM + (g + 1) * HEAD_DIM]
        o_ref[g] = _dot(p.astype(BF16), vg.astype(BF16))
        psum = jnp.sum(p[0:GRP, :], axis=0, keepdims=True)
        p_hi, p_mid, p_lo = _split3(jnp.broadcast_to(psum, (SUBLANES, ncmp)))
        pair = pair_ref[...]
        imp = (_dot(p_hi, pair) + _dot(p_mid, pair) + _dot(p_lo, pair))[0:1, :]
        score = imp + jnp.where(forced, FORCE_BONUS, 0.0)
        for t in range(N_PICK):
            best = jnp.max(score, axis=1, keepdims=True)
            idx = jnp.min(jnp.where(score == best, lane, nsel), axis=1, keepdims=True)
            score = jnp.where(lane == idx, NEG, score)
            idx_out = jnp.where(lane_o == g * N_SELECT + t, idx, idx_out)
    idx_ref[...] = idx_out


def _nsa_step_attn_body(sel_ref, pt_ref, *refs, past_len):
    nb = KV_HEADS * N_PICK
    blocks = refs[:nb]
    win_ref, q_ref, gt_ref, kvp_ref, kvw_ref, ocmp_ref, y_ref = refs[nb:]
    b = pl.program_id(0)
    q = q_ref[...]
    kvp = kvp_ref[...]
    kvw = kvw_ref[...]
    sg = _sigmoid(gt_ref[...])
    w = win_ref[...]
    nwin = w.shape[1]
    scale = HEAD_DIM ** -0.5
    pl_ = lax.broadcasted_iota(jnp.int32, (1, PAGE), 1)
    wl = lax.broadcasted_iota(jnp.int32, (1, nwin), 1)
    per_page = PAGE // SEL_BLOCK
    for g in range(KV_HEADS):
        q8 = _q_rows(q, g, scale)
        q8b = q8.astype(BF16)
        sc = _slope_col(g)
        hs = slice(g * HEAD_DIM, (g + 1) * HEAD_DIM)
        vs_ = slice(KV_DIM + g * HEAD_DIM, KV_DIM + (g + 1) * HEAD_DIM)
        k_new = kvp[:, 2 * KV_DIM + g * HEAD_DIM:2 * KV_DIM + (g + 1) * HEAD_DIM]
        v_new = kvp[:, 3 * KV_DIM + g * HEAD_DIM:3 * KV_DIM + (g + 1) * HEAD_DIM]
        s_cur = jnp.sum(q8 * k_new, axis=1, keepdims=True)
        ss, vs = [], []
        for t in range(N_PICK):
            blk = blocks[g * N_PICK + t][...]
            s_idx = sel_ref[b * KV_HEADS * N_SELECT + g * N_SELECT + t]
            kpos = (s_idx // per_page) * PAGE + pl_
            inblk = pl_ // SEL_BLOCK == s_idx % per_page
            st = _dot(q8b, blk[hs, :].astype(BF16)) - sc * (past_len - kpos).astype(F32)
            ss.append(jnp.where(inblk, st, NEG))
            vs.append(blk[vs_, :].astype(BF16))
        mx = functools.reduce(jnp.maximum, [jnp.max(s, axis=1, keepdims=True) for s in ss] + [s_cur])
        es = [jnp.exp(s - mx) for s in ss]
        e_cur = jnp.exp(s_cur - mx)
        den = functools.reduce(lambda a, c: a + c, [jnp.sum(e, axis=1, keepdims=True) for e in es] + [e_cur])
        acc = e_cur * v_new
        for e, v in zip(es, vs):
            acc = acc + _dot_nt(e.astype(BF16), v)
        o_slc = acc / den
        kw_new = kvw[:, g * HEAD_DIM:(g + 1) * HEAD_DIM]
        vw_new = kvw[:, KV_DIM + g * HEAD_DIM:KV_DIM + (g + 1) * HEAD_DIM]
        dw = (nwin - wl).astype(F32)
        s_w = jnp.where(nwin - wl < WINDOW, _dot(q8b, w[hs, :].astype(BF16)) - sc * dw, NEG)
        s_wc = jnp.sum(q8 * kw_new, axis=1, keepdims=True)
        mw = jnp.maximum(jnp.max(s_w, axis=1, keepdims=True), s_wc)
        e_w = jnp.exp(s_w - mw)
        e_wc = jnp.exp(s_wc - mw)
        o_win = (_dot_nt(e_w.astype(BF16), w[vs_, :].astype(BF16))
                 + e_wc * vw_new) / (jnp.sum(e_w, axis=1, keepdims=True) + e_wc)

        def gate_col(br):
            cols = [sg[:, br * N_HEADS + g * GRP + m:br * N_HEADS + g * GRP + m + 1] for m in range(GRP)]
            return jnp.concatenate(cols + [jnp.zeros((SUBLANES - GRP, 1), F32)], axis=0)

        y_ref[g] = gate_col(0) * ocmp_ref[g] + gate_col(1) * o_slc + gate_col(2) * o_win


def _nsa_step(q, kvp, kvw, gt, kc, cache, page_table, cache_win, layer):
    n, n_pages = page_table.shape
    past_len = n_pages * PAGE
    ncmp = kc.shape[1]
    nsel = ncmp // 2
    pair = np.zeros((ncmp, nsel), np.float32)
    pair[np.arange(ncmp), np.arange(ncmp) // 2] = 1.0
    r3 = lambda a: a.reshape(n, 1, a.shape[-1])
    row3 = lambda w: pl.BlockSpec((None, 1, w), lambda b, *_: (b, 0, 0))
    o_cmp, idx = pl.pallas_call(
        functools.partial(_nsa_step_cmp_body, past_len=past_len),
        grid=(n,),
        in_specs=[row3(ATTN_DIM), pl.BlockSpec((None, ncmp, CMP_W), lambda b: (b, 0, 0)), _full(pair.shape)],
        out_specs=[pl.BlockSpec((None, KV_HEADS, SUBLANES, HEAD_DIM), lambda b: (b, 0, 0, 0)), row3(LANES)],
        out_shape=[jax.ShapeDtypeStruct((n, KV_HEADS, SUBLANES, HEAD_DIM), F32),
                   jax.ShapeDtypeStruct((n, 1, LANES), jnp.int32)],
        compiler_params=_cparams(("parallel",)),
        name="nsa_step_cmp",
    )(r3(q), kc, jnp.asarray(pair, BF16))
    sel = idx[:, 0, :KV_HEADS * N_SELECT].reshape(-1)

    def blk_spec(j):
        g, t = divmod(j, N_PICK)

        def imap(b, sel_ref, pt_ref):
            s = sel_ref[b * KV_HEADS * N_SELECT + g * N_SELECT + t]
            return (layer, pt_ref[b * n_pages + s // (PAGE // SEL_BLOCK)], 1, 0)

        return pl.BlockSpec((None, None, 2 * KV_DIM, PAGE), imap)

    nb = KV_HEADS * N_PICK
    grid_spec = pltpu.PrefetchScalarGridSpec(
        num_scalar_prefetch=2,
        grid=(n,),
        in_specs=[blk_spec(j) for j in range(nb)]
        + [pl.BlockSpec((None, None, CMP_W, cache_win.shape[3]), lambda b, *_: (layer, b, 0, 0)),
           row3(ATTN_DIM), row3(GATE_PAD), row3(4 * KV_DIM), row3(2 * KV_DIM),
           pl.BlockSpec((None, KV_HEADS, SUBLANES, HEAD_DIM), lambda b, *_: (b, 0, 0, 0))],
        out_specs=pl.BlockSpec((None, KV_HEADS, SUBLANES, HEAD_DIM), lambda b, *_: (b, 0, 0, 0)),
    )
    y = pl.pallas_call(
        functools.partial(_nsa_step_attn_body, past_len=past_len),
        grid_spec=grid_spec,
        out_shape=jax.ShapeDtypeStruct((n, KV_HEADS, SUBLANES, HEAD_DIM), F32),
        compiler_params=_cparams(("arbitrary",)),
        name="nsa_step_attn",
    )(sel, page_table.reshape(-1), *([cache] * nb), cache_win, r3(q), r3(gt), r3(kvp), r3(kvw), o_cmp)
    return y[:, :, :GRP, :].reshape(n, ATTN_DIM)


def _patch_tail_body(t_ref, big_ref, o_ref):
    del big_ref
    o_ref[...] = t_ref[...]


def _patch_tail(x, xt, bsz, seq):
    nt = seq // TAIL
    return pl.pallas_call(
        _patch_tail_body,
        grid=(bsz,),
        in_specs=[pl.BlockSpec((TAIL, D_MODEL), lambda b: (b, 0)), pl.BlockSpec(memory_space=pl.ANY)],
        out_specs=pl.BlockSpec((TAIL, D_MODEL), lambda b: (b * nt + nt - 1, 0)),
        out_shape=jax.ShapeDtypeStruct(x.shape, x.dtype),
        input_output_aliases={1: 0},
        compiler_params=_cparams(("parallel",)),
        name="patch_tail",
    )(xt, x)


def kernel(x_prompt, x_sample, cache_kv, page_table, cache_win, state_ssm, state_conv, norm_attn_g, w_in,
           ssm_a_re, ssm_a_im, ssm_log_dt, ssm_b_re, ssm_b_im, ssm_c_re, ssm_c_im, ssm_d, ssm_w_glu, ssm_b_glu,
           cmp_pe, cmp_w1, cmp_w2, conv_w, w_br_ssm, w_br_attn, w_br_conv, w_out, norm_ffn_g,
           w_router_group, b_router_group, w_router_expert, b_router_expert, moe_w_gate, moe_w_up, moe_w_down,
           norm_final_g):
    bp, lp, _ = x_prompt.shape
    bs = x_sample.shape[0]
    depth = w_in.shape[0]
    n_pool = cache_kv.shape[1]
    nwin = cache_win.shape[2]
    xp = x_prompt.reshape(bp * lp, D_MODEL)
    xs = x_sample.reshape(bs, D_MODEL)
    cache = cache_kv.transpose(0, 1, 3, 4, 5, 2).reshape(depth, n_pool, 4 * KV_DIM, PAGE)
    cwin = cache_win.transpose(0, 1, 3, 4, 5, 2).reshape(depth, bs, 2 * KV_DIM, nwin)
    gf = norm_final_g.reshape(1, D_MODEL)
    kv_s, win_s, ssm_p, ssm_s, conv_p, conv_s = ([] for _ in range(6))
    kv_bufs = None

    def ssm_state(hre, him, n):
        return jnp.stack([hre.reshape(n, SSM_GROUPS, SSM_STATE), him.reshape(n, SSM_GROUPS, SSM_STATE)], axis=-1)

    for l in range(depth):
        w = w_in[l]
        w1, w1_lo = _hilo(jnp.concatenate(
            [w[:, :C_GATE], jnp.pad(w[:, C_GATE:C_CONV], ((0, 0), (0, GATE_PAD - 3 * N_HEADS)))], axis=1))
        g_attn = norm_attn_g[l].reshape(1, D_MODEL)
        g_ffn = norm_ffn_g[l].reshape(1, D_MODEL)
        s5w = _s5_prepare(ssm_a_re[l], ssm_a_im[l], ssm_log_dt[l], ssm_b_re[l], ssm_b_im[l],
                          ssm_c_re[l], ssm_c_im[l], ssm_d[l], ssm_w_glu[l], ssm_b_glu[l])
        cw = _cmp_prepare(cmp_pe[l], cmp_w1[l], cmp_w2[l])
        mw = _merge_weights(w[:, C_CONV:], w_br_ssm[l], w_br_attn[l], w_br_conv[l], w_out[l], conv_w[l])
        mo = _moe_prepare(w_router_group[l], b_router_group[l], w_router_expert[l], b_router_expert[l],
                          moe_w_gate[l], moe_w_up[l], moe_w_down[l])
        final = l == depth - 1

        u, q, gt, *kv_bufs = _inproj_seq(xp, g_attn, w1, bp, lp, l, depth, kv_bufs)
        y_ssm, hre, him, pre_r, pre_i = _s5_seq(u, s5w, bp, lp)
        kc = _compress_seq(kv_bufs[0], l, cw, bp, lp)
        y_attn = _nsa_seq(q, kv_bufs[0], kv_bufs[1], l, gt, kc, bp, lp)
        x1, cl, cpre = _merge_seq(xp, g_attn, y_ssm, y_attn, mw, bp, lp)
        x2 = _moe(x1, g_ffn, mo, gf, 1024, final=final)
        if not final and lp >= 2 * TAIL:
            tail = lambda a: a.reshape(bp, lp, a.shape[-1])[:, lp - TAIL:].reshape(bp * TAIL, a.shape[-1])
            xt = tail(xp)
            ut = _inproj(xt, g_attn, w1, TAIL, w1_lo)[0]
            yst = _s5_seq(ut, s5w, bp, TAIL, tl=TAIL, precise=True, h0=(pre_r, pre_i))[0]
            x1t = _merge_seq(xt, g_attn, yst, tail(y_attn), mw, bp, TAIL, tm=TAIL, precise=True, cin=cpre)[0]
            x2t = _moe(x1t, g_ffn, mo, gf, min(1024, bp * TAIL), final=False)
            x2 = _patch_tail(x2, x2t, bp, lp)
        xp = x2
        ssm_p.append(ssm_state(hre[:, 0], him[:, 0], bp))
        conv_p.append(cl[:, SUBLANES - 2:])

        u, q, kvp, kvw, gt = _inproj(xs, g_attn, w1, bs, w1_lo)
        st = state_ssm[l]
        y_ssm, hre, him = _s5_step(u, st[..., 0].reshape(bs, SSM_N), st[..., 1].reshape(bs, SSM_N), s5w)
        kc = _compress_pages(cache, page_table, l, cw)
        y_attn = _nsa_step(q, kvp, kvw, gt, kc, cache, page_table, cwin, l)
        prev = state_conv[l]
        x1, uc = _merge_step(xs, g_attn, y_ssm, y_attn, prev[:, 0], prev[:, 1], mw)
        xs = _moe(x1, g_ffn, mo, gf, bs, final=final)
        kv_s.append(kvp.reshape(bs, 1, 4, KV_HEADS, HEAD_DIM))
        win_s.append(jnp.concatenate([cache_win[l][:, 1:], kvw.reshape(bs, 1, 2, KV_HEADS, HEAD_DIM)], axis=1))
        ssm_s.append(ssm_state(hre, him, bs))
        conv_s.append(jnp.stack([prev[:, 1], uc], axis=1))

    kvp_t, kvw_t = kv_bufs
    nw = min(WINDOW, lp)
    kv_prompt = kvp_t.reshape(depth, bp, 4, KV_HEADS, HEAD_DIM, lp).transpose(0, 1, 5, 2, 3, 4)
    win_prompt = kvw_t[..., lp - nw:].reshape(depth, bp, 2, KV_HEADS, HEAD_DIM, nw).transpose(0, 1, 5, 2, 3, 4)
    return (xp.reshape(bp, lp, D_MODEL), xs.reshape(bs, 1, D_MODEL),
            kv_prompt, jnp.stack(kv_s), win_prompt, jnp.stack(win_s),
            jnp.stack(ssm_p), jnp.stack(ssm_s), jnp.stack(conv_p), jnp.stack(conv_s))
```

```python
import functools
import math

import numpy as np
import jax
import jax.numpy as jnp
from jax import lax
from jax.experimental import pallas as pl
from jax.experimental.pallas import tpu as pltpu

F32 = jnp.float32
BF16 = jnp.bfloat16

D_MODEL = 1024
DEPTH = 2
PAGE = 128
SSM_GROUPS = 24
SSM_CH = 16
SSM_DIM = SSM_GROUPS * SSM_CH
SSM_STATE = 64
SSM_N = SSM_GROUPS * SSM_STATE
N_HEADS = 8
HEAD_DIM = 64
KV_HEADS = 2
GRP = N_HEADS // KV_HEADS
ATTN_DIM = N_HEADS * HEAD_DIM
KV_DIM = KV_HEADS * HEAD_DIM
CMP_BLOCK = 32
CMP_HIDDEN = 64
SEL_BLOCK = 64
N_SELECT = 8
WINDOW = 512
FORCE_BONUS = 1e4
CONV_DIM = 384
N_GROUPS = 4
EPG = 4
N_EXPERTS = 16
D_FF = 256
RMS_EPS = 1e-6

C_U = 0
C_Q = C_U + SSM_DIM
C_KVP = C_Q + ATTN_DIM
C_KVW = C_KVP + 4 * KV_DIM
C_GATE = C_KVW + 2 * KV_DIM
C_CONV = C_GATE + 3 * N_HEADS
C_MERGE = C_CONV + 3 * CONV_DIM
N_IN = C_MERGE + 3 * D_MODEL
W2_COLS = N_IN - C_CONV
GATE_PAD = 128
W1_COLS = C_GATE + GATE_PAD

LANES = 128
SUBLANES = 8
NEG = -1e30
M_INIT = -1e29
VMEM_LIMIT = 56 * 1024 * 1024


def _cparams(sem):
    return pltpu.CompilerParams(dimension_semantics=sem, vmem_limit_bytes=VMEM_LIMIT)


def _rms(x, g):
    ms = jnp.mean(x * x, axis=-1, keepdims=True)
    return x * lax.rsqrt(ms + RMS_EPS) * g


def _gelu_tanh(x):
    return 0.5 * x * (1.0 + jnp.tanh(math.sqrt(2.0 / math.pi) * (x + 0.044715 * (x * x * x))))


def _sigmoid(x):
    return 1.0 / (1.0 + jnp.exp(-x))


def _dot(a, b):
    return jnp.dot(a, b, preferred_element_type=F32)


def _dot_nt(a, b):
    return lax.dot_general(a, b, (((1,), (1,)), ((), ())), preferred_element_type=F32)


def _split3(x):
    hi = x.astype(BF16)
    r1 = x - hi.astype(F32)
    mid = r1.astype(BF16)
    lo = (r1 - mid.astype(F32)).astype(BF16)
    return hi, mid, lo


def _mm(a, w_ref, wlo_ref=None):
    ah = a.astype(BF16)
    if wlo_ref is None:
        return _dot(ah, w_ref[...])
    al = (a - ah.astype(F32)).astype(BF16)
    return _dot(ah, w_ref[...]) + (_dot(al, w_ref[...]) + _dot(ah, wlo_ref[...]))


def _hilo(w):
    hi = w.astype(BF16)
    return hi, (w - hi.astype(F32)).astype(BF16)


def _full(shape):
    nd = len(shape)
    return pl.BlockSpec(shape, lambda *_: (0,) * nd)


def _full1(shape):
    nd = len(shape)
    return pl.BlockSpec(shape, lambda *_: (0,) * nd, pipeline_mode=pl.Buffered(1))


def _inproj_body(x_ref, g_ref, w_ref, *refs, precise):
    wlo_ref = refs[0] if precise else None
    u_ref, q_ref, kvp_ref, kvw_ref, gt_ref = refs[1:] if precise else refs
    h = _rms(x_ref[...], g_ref[...])

    def proj(a, b):
        return _mm(h, w_ref.at[:, a:b], wlo_ref.at[:, a:b] if precise else None)

    u_ref[...] = proj(C_U, C_Q)
    q_ref[...] = proj(C_Q, C_KVP)
    kvp_ref[...] = proj(C_KVP, C_KVW)
    kvw_ref[...] = proj(C_KVW, C_GATE)
    gt_ref[...] = proj(C_GATE, W1_COLS)


def _inproj(x, g, w1, tm, w1_lo=None):
    n = x.shape[0]
    widths = (SSM_DIM, ATTN_DIM, 4 * KV_DIM, 2 * KV_DIM, GATE_PAD)
    ws = [w1] if w1_lo is None else [w1, w1_lo]
    return pl.pallas_call(
        functools.partial(_inproj_body, precise=w1_lo is not None),
        grid=(n // tm,),
        in_specs=[
            pl.BlockSpec((tm, D_MODEL), lambda i: (i, 0)),
            _full((1, D_MODEL)),
        ] + [_full((D_MODEL, W1_COLS))] * len(ws),
        out_specs=[pl.BlockSpec((tm, w), lambda i: (i, 0)) for w in widths],
        out_shape=[jax.ShapeDtypeStruct((n, w), F32) for w in widths],
        compiler_params=_cparams(("parallel",)),
        name="inproj",
    )(x, g, *ws)


def _inproj_seq_body(x_ref, g_ref, w_ref, *refs, aliased):
    u_ref, q_ref, gt_ref, kvp_ref, kvw_ref = refs[2:] if aliased else refs
    hb = _rms(x_ref[...], g_ref[...]).astype(BF16)

    def proj(a, b):
        return _dot(hb, w_ref[:, a:b])

    u_ref[...] = proj(C_U, C_Q)
    q_ref[...] = proj(C_Q, C_KVP)
    gt_ref[...] = proj(C_GATE, W1_COLS)
    kvp_ref[...] = proj(C_KVP, C_KVW).T
    kvw_ref[...] = proj(C_KVW, C_GATE).T


def _inproj_seq(x, g, w1, bsz, seq, layer, depth, kv_bufs=None, tm=512):
    nt = seq // tm
    widths = (SSM_DIM, ATTN_DIM, GATE_PAD)
    row = lambda w: pl.BlockSpec((tm, w), lambda b, i: (b * nt + i, 0))
    kv_spec = lambda f: pl.BlockSpec((None, None, f, tm), lambda b, i: (layer, b, 0, i))
    aliased = kv_bufs is not None
    extra = list(kv_bufs) if aliased else []
    any_spec = pl.BlockSpec(memory_space=pl.ANY)
    return pl.pallas_call(
        functools.partial(_inproj_seq_body, aliased=aliased),
        grid=(bsz, nt),
        in_specs=[row(D_MODEL), _full((1, D_MODEL)), _full((D_MODEL, W1_COLS))] + [any_spec] * len(extra),
        out_specs=[row(w) for w in widths] + [kv_spec(4 * KV_DIM), kv_spec(2 * KV_DIM)],
        out_shape=[jax.ShapeDtypeStruct((bsz * seq, w), F32) for w in widths]
        + [jax.ShapeDtypeStruct((depth, bsz, 4 * KV_DIM, seq), F32),
           jax.ShapeDtypeStruct((depth, bsz, 2 * KV_DIM, seq), F32)],
        input_output_aliases={3: 3, 4: 4} if aliased else {},
        compiler_params=_cparams(("parallel", "parallel")),
        name="inproj_seq",
    )(x, g, w1, *extra)


S5_CHUNK = 128
S5_J = S5_CHUNK // SUBLANES


def _s5_tables_body(ar_r, ai_r, ldt_r, ar_c, ai_c, ldt_c, bre_ref, bim_ref,
                    a8re, a8im, apre, apim, a16re, a16im, bbre, bbim):
    dt = jnp.exp(ldt_r[...])
    mag = jnp.exp(ar_r[...] * dt)
    ang = ai_r[...] * dt
    are = mag * jnp.cos(ang)
    aim = mag * jnp.sin(ang)
    a8re[...] = jnp.broadcast_to(are, (SUBLANES, SSM_N))
    a8im[...] = jnp.broadcast_to(aim, (SUBLANES, SSM_N))
    pr, pi = are, aim
    for j in range(S5_J):
        apre[j * SUBLANES:(j + 1) * SUBLANES, :] = jnp.broadcast_to(pr, (SUBLANES, SSM_N))
        apim[j * SUBLANES:(j + 1) * SUBLANES, :] = jnp.broadcast_to(pi, (SUBLANES, SSM_N))
        if j + 1 < S5_J:
            pr, pi = pr * are - pi * aim, pr * aim + pi * are
    sre, sim = pr, pi
    qr, qi = sre, sim
    for s in range(SUBLANES):
        a16re[s:s + 1, :] = qr
        a16im[s:s + 1, :] = qi
        if s + 1 < SUBLANES:
            qr, qi = qr * sre - qi * sim, qr * sim + qi * sre
    dtc = jnp.exp(ldt_c[...])
    arc, aic = ar_c[...], ai_c[...]
    magc = jnp.exp(arc * dtc)
    angc = aic * dtc
    arec = magc * jnp.cos(angc)
    aimc = magc * jnp.sin(angc)
    den = arc * arc + aic * aic
    cre = ((arec - 1.0) * arc + aimc * aic) / den
    cim = (aimc * arc - (arec - 1.0) * aic) / den
    br, bi = bre_ref[...], bim_ref[...]
    bbre[...] = cre * br - cim * bi
    bbim[...] = cre * bi + cim * br


def _s5_tables(a_re, a_im, log_dt, b_re, b_im):
    ar_r = a_re.reshape(1, SSM_N)
    ai_r = a_im.reshape(1, SSM_N)
    ldt_r = jnp.repeat(log_dt, SSM_STATE).reshape(1, SSM_N)
    row = jax.ShapeDtypeStruct((SUBLANES, SSM_N), F32)
    tab = jax.ShapeDtypeStruct((S5_CHUNK, SSM_N), F32)
    col = jax.ShapeDtypeStruct((SSM_N, SSM_CH), F32)
    return pl.pallas_call(
        _s5_tables_body,
        out_shape=[row, row, tab, tab, row, row, col, col],
        name="s5_tables",
    )(ar_r, ai_r, ldt_r, ar_r.reshape(SSM_N, 1), ai_r.reshape(SSM_N, 1), ldt_r.reshape(SSM_N, 1),
      b_re.reshape(SSM_N, SSM_CH), b_im.reshape(SSM_N, SSM_CH))


def _block_diag_b(bb):
    t = bb.reshape(SSM_GROUPS, SSM_STATE, SSM_CH).transpose(0, 2, 1)
    eye = jnp.eye(SSM_GROUPS, dtype=bb.dtype)
    return (t[:, :, None, :] * eye[:, None, :, None]).reshape(SSM_DIM, SSM_N)


def _block_diag_c(c):
    t = c.transpose(0, 2, 1)
    eye = jnp.eye(SSM_GROUPS, dtype=c.dtype)
    return (t[:, :, None, :] * eye[:, None, :, None]).reshape(SSM_N, SSM_DIM)


def _s5_perm():
    p = np.zeros((S5_CHUNK, S5_CHUNK), np.float32)
    for j in range(S5_J):
        for s in range(SUBLANES):
            p[j * SUBLANES + s, S5_J * s + j] = 1.0
    return p


S5_CL = 3
S5_CU = SSM_DIM // S5_CL
S5_CS = SSM_N // S5_CL


def _s5_bu(u_hi, u_lo, bmat_ref, bmat_lo):
    res, ims = [], []
    for c in range(S5_CL):
        cols = slice(c * S5_CU, (c + 1) * S5_CU)
        b = _dot(u_hi[:, cols], bmat_ref[c])
        if bmat_lo is not None:
            b = b + (_dot(u_lo[:, cols], bmat_ref[c]) + _dot(u_hi[:, cols], bmat_lo[c]))
        res.append(b[:, :S5_CS])
        ims.append(b[:, S5_CS:])
    return jnp.concatenate(res, axis=1), jnp.concatenate(ims, axis=1)


def _s5_cy(hr, hi, cmat_ref, cmat_lo):
    ys = []
    for c in range(S5_CL):
        cols = slice(c * S5_CS, (c + 1) * S5_CS)
        h = jnp.concatenate([hr[:, cols], hi[:, cols]], axis=1)
        ys.append(_mm(h, cmat_ref.at[c], None if cmat_lo is None else cmat_lo.at[c]))
    return jnp.concatenate(ys, axis=1)


def _s5_epilogue(y, u, d_ref, wglu_ref, wglu_lo, bglu_ref):
    y = y + d_ref[...] * u
    g = _gelu_tanh(y)
    return g * _sigmoid(_mm(g, wglu_ref, wglu_lo) + bglu_ref[...])


def _s5_seq_body(*refs, tl, precise, carry_in):
    it = iter(refs)
    u_ref = next(it)
    h0r_ref, h0i_ref = (next(it), next(it)) if carry_in else (None, None)
    (perm_ref, permt_ref, bmat_ref, cmat_ref, a8re_ref, a8im_ref, apre_ref, apim_ref, a16re_ref, a16im_ref,
     d_ref, wglu_ref, bglu_ref) = (next(it) for _ in range(13))
    bmat_lo, cmat_lo, wglu_lo = (next(it), next(it), next(it)) if precise else (None, None, None)
    y_ref, hre_ref, him_ref, pre_r_ref, pre_i_ref, cre_scr, cim_scr = it
    li = pl.program_id(1)

    @pl.when(li == 0)
    def _():
        if carry_in:
            cre_scr[...] = h0r_ref[...]
            cim_scr[...] = h0i_ref[...]
        else:
            cre_scr[...] = jnp.zeros_like(cre_scr)
            cim_scr[...] = jnp.zeros_like(cim_scr)

    are = a8re_ref[...]
    aim = a8im_ref[...]
    row8 = lax.broadcasted_iota(jnp.int32, (SUBLANES, SSM_N), 0)
    nchunk = tl // S5_CHUNK

    for c in range(nchunk):
        if c == nchunk - 1:
            pre_r_ref[...] = cre_scr[...]
            pre_i_ref[...] = cim_scr[...]
        u = u_ref[c * S5_CHUNK:(c + 1) * S5_CHUNK, :]
        u_hi = u.astype(BF16)
        up = _dot(perm_ref[...], u_hi).astype(BF16)
        up_lo = (_dot(perm_ref[...], (u - u_hi.astype(F32)).astype(BF16)).astype(BF16) if precise else None)
        bu_re, bu_im = _s5_bu(up, up_lo, bmat_ref, bmat_lo)
        hr = [bu_re[0:SUBLANES, :]]
        hi = [bu_im[0:SUBLANES, :]]
        for j in range(1, S5_J):
            br = bu_re[j * SUBLANES:(j + 1) * SUBLANES, :]
            bi = bu_im[j * SUBLANES:(j + 1) * SUBLANES, :]
            hr.append(are * hr[-1] - aim * hi[-1] + br)
            hi.append(are * hi[-1] + aim * hr[-2] + bi)
        er, ei = hr[-1], hi[-1]
        for k, d in enumerate((1, 2, 4)):
            mr = jnp.broadcast_to(a16re_ref[d - 1:d, :], (SUBLANES, SSM_N))
            mi = jnp.broadcast_to(a16im_ref[d - 1:d, :], (SUBLANES, SSM_N))
            sr = jnp.where(row8 >= d, pltpu.roll(er, d, 0), 0.0)
            si = jnp.where(row8 >= d, pltpu.roll(ei, d, 0), 0.0)
            er, ei = er + mr * sr - mi * si, ei + mr * si + mi * sr
        h0r = jnp.broadcast_to(cre_scr[0:1, :], (SUBLANES, SSM_N))
        h0i = jnp.broadcast_to(cim_scr[0:1, :], (SUBLANES, SSM_N))
        p16r, p16i = a16re_ref[...], a16im_ref[...]
        er, ei = er + p16r * h0r - p16i * h0i, ei + p16r * h0i + p16i * h0r
        cinr = jnp.where(row8 == 0, h0r, pltpu.roll(er, 1, 0))
        cini = jnp.where(row8 == 0, h0i, pltpu.roll(ei, 1, 0))
        cre_scr[...] = jnp.broadcast_to(er[SUBLANES - 1:SUBLANES, :], (SUBLANES, SSM_N))
        cim_scr[...] = jnp.broadcast_to(ei[SUBLANES - 1:SUBLANES, :], (SUBLANES, SSM_N))
        fr, fi = [], []
        for j in range(S5_J):
            pr = apre_ref[j * SUBLANES:(j + 1) * SUBLANES, :]
            pi = apim_ref[j * SUBLANES:(j + 1) * SUBLANES, :]
            fr.append(hr[j] + pr * cinr - pi * cini)
            fi.append(hi[j] + pr * cini + pi * cinr)
        yp = _s5_cy(jnp.concatenate(fr, axis=0), jnp.concatenate(fi, axis=0), cmat_ref, cmat_lo)
        y_hi, y_mid, y_lo = _split3(yp)
        pt = permt_ref[...]
        y = _dot(pt, y_hi) + _dot(pt, y_mid) + _dot(pt, y_lo)
        y_ref[c * S5_CHUNK:(c + 1) * S5_CHUNK, :] = _s5_epilogue(y, u, d_ref, wglu_ref, wglu_lo, bglu_ref)

    hre_ref[...] = cre_scr[...]
    him_ref[...] = cim_scr[...]


def _s5_seq(u, s5w, bsz, seq, tl=512, precise=False, h0=None):
    nt = seq // tl
    consts = list(s5w["tabs"]) + [s5w["d"], s5w["wglu"], s5w["bglu"]]
    if precise:
        consts += [s5w["bmat_lo"], s5w["cmat_lo"], s5w["wglu_lo"]]
    state_spec = pl.BlockSpec((None, SUBLANES, SSM_N), lambda b, i: (b, 0, 0))
    state_shape = jax.ShapeDtypeStruct((bsz, SUBLANES, SSM_N), F32)
    carry = [] if h0 is None else list(h0)
    return pl.pallas_call(
        functools.partial(_s5_seq_body, tl=tl, precise=precise, carry_in=h0 is not None),
        grid=(bsz, nt),
        in_specs=[pl.BlockSpec((tl, SSM_DIM), lambda b, i: (b * nt + i, 0))]
        + [state_spec] * len(carry) + [_full(c.shape) for c in consts],
        out_specs=[pl.BlockSpec((tl, SSM_DIM), lambda b, i: (b * nt + i, 0))] + [state_spec] * 4,
        out_shape=[jax.ShapeDtypeStruct((bsz * seq, SSM_DIM), F32)] + [state_shape] * 4,
        scratch_shapes=[pltpu.VMEM((SUBLANES, SSM_N), F32), pltpu.VMEM((SUBLANES, SSM_N), F32)],
        compiler_params=_cparams(("parallel", "arbitrary")),
        name="s5_seq",
    )(u, *carry, *consts)


def _s5_step_body(u_ref, h0r_ref, h0i_ref, bmat_ref, cmat_ref, a8re_ref, a8im_ref,
                  d_ref, wglu_ref, bglu_ref, bmat_lo, cmat_lo, wglu_lo, y_ref, hre_ref, him_ref):
    u = u_ref[...]
    u_hi = u.astype(BF16)
    bu_re, bu_im = _s5_bu(u_hi, (u - u_hi.astype(F32)).astype(BF16), bmat_ref, bmat_lo)
    are = a8re_ref[0:1, :]
    aim = a8im_ref[0:1, :]
    h0r, h0i = h0r_ref[...], h0i_ref[...]
    hr = are * h0r - aim * h0i + bu_re
    hi = are * h0i + aim * h0r + bu_im
    hre_ref[...] = hr
    him_ref[...] = hi
    y = _s5_cy(hr, hi, cmat_ref, cmat_lo)
    y_ref[...] = _s5_epilogue(y, u, d_ref, wglu_ref, wglu_lo, bglu_ref)


def _s5_step(u, h0r, h0i, s5w):
    _, _, bmat, cmat, a8re, a8im = s5w["tabs"][:6]
    n = u.shape[0]
    return pl.pallas_call(
        _s5_step_body,
        out_shape=[
            jax.ShapeDtypeStruct((n, SSM_DIM), F32),
            jax.ShapeDtypeStruct((n, SSM_N), F32),
            jax.ShapeDtypeStruct((n, SSM_N), F32),
        ],
        compiler_params=pltpu.CompilerParams(vmem_limit_bytes=VMEM_LIMIT),
        name="s5_step",
    )(u, h0r, h0i, bmat, cmat, a8re, a8im, s5w["d"], s5w["wglu"], s5w["bglu"],
      s5w["bmat_lo"], s5w["cmat_lo"], s5w["wglu_lo"])


def _s5_prepare(a_re, a_im, log_dt, b_re, b_im, c_re, c_im, d_skip, w_glu, b_glu):
    a8re, a8im, apre, apim, a16re, a16im, bbre, bbim = _s5_tables(a_re, a_im, log_dt, b_re, b_im)
    bre, bim = _block_diag_b(bbre), _block_diag_b(bbim)
    cre, cim = _block_diag_c(c_re), _block_diag_c(c_im)
    us = lambda c: slice(c * S5_CU, (c + 1) * S5_CU)
    ss = lambda c: slice(c * S5_CS, (c + 1) * S5_CS)
    bmat, bmat_lo = _hilo(jnp.stack(
        [jnp.concatenate([bre[us(c), ss(c)], bim[us(c), ss(c)]], axis=1) for c in range(S5_CL)]))
    cmat, cmat_lo = _hilo(jnp.stack(
        [jnp.concatenate([cre[ss(c), us(c)], -cim[ss(c), us(c)]], axis=0) for c in range(S5_CL)]))
    wglu, wglu_lo = _hilo(w_glu)
    perm = _s5_perm()
    tabs = (jnp.asarray(perm, BF16), jnp.asarray(perm.T, BF16), bmat, cmat, a8re, a8im, apre, apim, a16re, a16im)
    return dict(tabs=tabs, d=d_skip.reshape(1, SSM_DIM), wglu=wglu, bglu=b_glu.reshape(1, SSM_DIM),
                bmat_lo=bmat_lo, cmat_lo=cmat_lo, wglu_lo=wglu_lo)


TAIL = 128

MERGE_W = ("w2", "wbs", "wba", "wbc", "wo")


def _merge_weights(w2, w_br_ssm, w_br_attn, w_br_conv, w_out, conv_w):
    mw = {"cw": conv_w}
    for name, w in zip(MERGE_W, (w2, w_br_ssm, w_br_attn, w_br_conv, w_out)):
        mw[name], mw[name + "_lo"] = _hilo(w)
    return mw


def _merge_wlist(mw, precise):
    return [mw["cw"]] + [mw[n] for n in MERGE_W] + ([mw[n + "_lo"] for n in MERGE_W] if precise else [])


def _merge_core(x, g_ref, wrefs, precise, up0, up1, ys_ref, ya_ref):
    cw_ref, w2_ref, wbs_ref, wba_ref, wbc_ref, wo_ref = wrefs[:6]
    w2_lo, wbs_lo, wba_lo, wbc_lo, wo_lo = wrefs[6:] if precise else (None,) * 5
    zc = _mm(_rms(x, g_ref[...]), w2_ref, w2_lo)
    conv_b = zc[:, 0:CONV_DIM]
    uc = zc[:, CONV_DIM:2 * CONV_DIM] * zc[:, 2 * CONV_DIM:3 * CONV_DIM]
    yc = conv_b * (cw_ref[0:1, :] * up0(uc) + cw_ref[1:2, :] * up1(uc) + cw_ref[2:3, :] * uc)
    g0 = 3 * CONV_DIM
    g_ssm = _sigmoid(zc[:, g0:g0 + D_MODEL])
    g_attn = _sigmoid(zc[:, g0 + D_MODEL:g0 + 2 * D_MODEL])
    g_conv = _sigmoid(zc[:, g0 + 2 * D_MODEL:g0 + 3 * D_MODEL])
    merged = (g_ssm * _mm(ys_ref[...], wbs_ref, wbs_lo)
              + g_attn * _mm(ya_ref[...], wba_ref, wba_lo)
              + g_conv * _mm(yc, wbc_ref, wbc_lo))
    return x + _mm(merged, wo_ref, wo_lo), uc


def _merge_seq_body(*refs, tm, precise, carry_in):
    it = iter(refs)
    x_ref, g_ref, ys_ref, ya_ref = (next(it) for _ in range(4))
    cin_ref = next(it) if carry_in else None
    wrefs = [next(it) for _ in range(11 if precise else 6)]
    o_ref, cl_ref, pre_ref, stage = it
    ti = pl.program_id(1)

    @pl.when(ti == 0)
    def _():
        stage[0:SUBLANES, :] = cin_ref[...] if carry_in else jnp.zeros((SUBLANES, CONV_DIM), F32)

    def up0(uc):
        stage[SUBLANES:SUBLANES + tm, :] = uc
        return stage[SUBLANES - 2:SUBLANES - 2 + tm, :]

    def up1(uc):
        return stage[SUBLANES - 1:SUBLANES - 1 + tm, :]

    out, uc = _merge_core(x_ref[...], g_ref, wrefs, precise, up0, up1, ys_ref, ya_ref)
    o_ref[...] = out
    last = uc[tm - SUBLANES:tm, :]
    stage[0:SUBLANES, :] = last
    cl_ref[...] = last
    pre_ref[...] = uc[max(tm - TAIL, SUBLANES) - SUBLANES:max(tm - TAIL, SUBLANES), :]


def _merge_seq(x, g, ys, ya, mw, bsz, seq, tm=256, precise=False, cin=None):
    nt = seq // tm
    ws = _merge_wlist(mw, precise)
    row = lambda w: pl.BlockSpec((tm, w), lambda b, i: (b * nt + i, 0))
    cspec = pl.BlockSpec((None, SUBLANES, CONV_DIM), lambda b, i: (b, 0, 0))
    carry = [] if cin is None else [cin]
    return pl.pallas_call(
        functools.partial(_merge_seq_body, tm=tm, precise=precise, carry_in=cin is not None),
        grid=(bsz, nt),
        in_specs=[row(D_MODEL), _full(g.shape), row(SSM_DIM), row(ATTN_DIM)] + [cspec] * len(carry)
        + [_full1(w.shape) for w in ws],
        out_specs=[row(D_MODEL), cspec, cspec],
        out_shape=[jax.ShapeDtypeStruct((bsz * seq, D_MODEL), F32)]
        + [jax.ShapeDtypeStruct((bsz, SUBLANES, CONV_DIM), F32)] * 2,
        scratch_shapes=[pltpu.VMEM((tm + SUBLANES, CONV_DIM), F32)],
        compiler_params=_cparams(("parallel", "arbitrary")),
        name="merge_seq",
    )(x, g, ys, ya, *carry, *ws)


def _merge_step_body(x_ref, g_ref, ys_ref, ya_ref, p0_ref, p1_ref, *refs):
    wrefs, (o_ref, uc_ref) = refs[:11], refs[11:]
    out, uc = _merge_core(x_ref[...], g_ref, wrefs, True, lambda _: p0_ref[...], lambda _: p1_ref[...],
                          ys_ref, ya_ref)
    o_ref[...] = out
    uc_ref[...] = uc


def _merge_step(x, g, ys, ya, prev0, prev1, mw):
    n = x.shape[0]
    return pl.pallas_call(
        _merge_step_body,
        out_shape=[jax.ShapeDtypeStruct((n, D_MODEL), F32), jax.ShapeDtypeStruct((n, CONV_DIM), F32)],
        compiler_params=pltpu.CompilerParams(vmem_limit_bytes=VMEM_LIMIT),
        name="merge_step",
    )(x, g, ys, ya, prev0, prev1, *_merge_wlist(mw, True))


ROUTE_E0 = SUBLANES


def _route_math(lg, le):
    gmax = functools.reduce(jnp.maximum, lg)
    gsum = functools.reduce(lambda a, b: a + b, [jnp.exp(v - gmax) for v in lg])
    gw = 1.0 / gsum
    gsel = jnp.full_like(gmax, N_GROUPS - 1).astype(jnp.int32)
    for k in range(N_GROUPS - 2, -1, -1):
        gsel = jnp.where(lg[k] == gmax, k, gsel)
    ls = []
    for j in range(EPG):
        v = le[j]
        for k in range(1, N_GROUPS):
            v = jnp.where(gsel == k, le[k * EPG + j], v)
        ls.append(v)
    emax = functools.reduce(jnp.maximum, ls)
    ex = [jnp.exp(v - emax) for v in ls]
    esum = functools.reduce(lambda a, b: a + b, ex)
    pe = [v / esum for v in ex]
    v1 = functools.reduce(jnp.maximum, pe)
    i1 = jnp.full_like(gsel, EPG - 1)
    for j in range(EPG - 2, -1, -1):
        i1 = jnp.where(pe[j] == v1, j, i1)
    pe2 = [jnp.where(i1 == j, -1.0, pe[j]) for j in range(EPG)]
    v2 = functools.reduce(jnp.maximum, pe2)
    i2 = jnp.full_like(gsel, EPG - 1)
    for j in range(EPG - 2, -1, -1):
        i2 = jnp.where(pe2[j] == v2, j, i2)
    tot = v1 + v2
    w1 = v1 / tot * gw
    w2 = v2 / tot * gw
    return gsel * EPG + i1, gsel * EPG + i2, w1, w2


def _moe_route(h, wr_ref, wrt_ref, br_ref, brt_ref):
    tm = h.shape[0]
    h_hi = h.astype(BF16)
    h_lo = (h - h_hi.astype(F32)).astype(BF16)
    if tm % LANES:
        logits = _dot(h_hi, wr_ref[0]) + (_dot(h_lo, wr_ref[0]) + _dot(h_hi, wr_ref[1])) + br_ref[...]
        lg = [logits[:, k:k + 1] for k in range(N_GROUPS)]
        le = [logits[:, ROUTE_E0 + e:ROUTE_E0 + e + 1] for e in range(N_EXPERTS)]
        e1, e2, w1, w2 = _route_math(lg, le)
        lane = lax.broadcasted_iota(jnp.int32, (tm, LANES), 1)
        return jnp.where(lane == e1, w1, 0.0) + jnp.where(lane == e2, w2, 0.0)
    lt = _dot_nt(wrt_ref[0], h_hi) + (_dot_nt(wrt_ref[0], h_lo) + _dot_nt(wrt_ref[1], h_hi)) + brt_ref[...]
    lg = [lt[k:k + 1, :] for k in range(N_GROUPS)]
    le = [lt[ROUTE_E0 + e:ROUTE_E0 + e + 1, :] for e in range(N_EXPERTS)]
    e1, e2, w1, w2 = _route_math(lg, le)
    rows = [jnp.where(e1 == e, w1, 0.0) + jnp.where(e2 == e, w2, 0.0) for e in range(N_EXPERTS)]
    comb_t = jnp.concatenate(rows + [jnp.zeros((LANES - N_EXPERTS, tm), F32)], axis=0)
    return comb_t.T


def _moe_body(x_ref, g_ref, wr_ref, wrt_ref, br_ref, brt_ref, wgu_ref, wd_ref, gf_ref, o_ref,
              hb_scr, comb_scr, acc_scr, *, final):
    e = pl.program_id(1)

    @pl.when(e == 0)
    def _():
        h = _rms(x_ref[...], g_ref[...])
        hb_scr[...] = h.astype(BF16)
        comb_scr[...] = _moe_route(h, wr_ref, wrt_ref, br_ref, brt_ref)
        acc_scr[...] = jnp.zeros_like(acc_scr)

    hgu = _dot(hb_scr[...], wgu_ref[0])
    hg = hgu[:, :D_FF]
    lane = lax.broadcasted_iota(jnp.int32, comb_scr.shape, 1)
    ce = jnp.sum(jnp.where(lane == e, comb_scr[...], 0.0), axis=1, keepdims=True)
    act = hg * _sigmoid(hg) * hgu[:, D_FF:] * ce
    acc_scr[...] += _dot(act.astype(BF16), wd_ref[0])

    @pl.when(e == N_EXPERTS - 1)
    def _():
        y = x_ref[...] + acc_scr[...]
        o_ref[...] = _rms(y, gf_ref[...]) if final else y


def _moe(x, g, mo, gf, tm, final=False):
    n = x.shape[0]
    wr, wrt, br, brt, wgu, wd = (mo[k] for k in ("wr", "wrt", "br", "brt", "wgu", "wd"))
    return pl.pallas_call(
        functools.partial(_moe_body, final=final),
        grid=(n // tm, N_EXPERTS),
        in_specs=[
            pl.BlockSpec((tm, D_MODEL), lambda i, e: (i, 0)),
            _full(g.shape), _full(wr.shape), _full(wrt.shape), _full(br.shape), _full(brt.shape),
            pl.BlockSpec((1, D_MODEL, 2 * D_FF), lambda i, e: (e, 0, 0)),
            pl.BlockSpec((1, D_FF, D_MODEL), lambda i, e: (e, 0, 0)),
            _full(gf.shape),
        ],
        out_specs=pl.BlockSpec((tm, D_MODEL), lambda i, e: (i, 0)),
        out_shape=jax.ShapeDtypeStruct((n, D_MODEL), F32),
        scratch_shapes=[pltpu.VMEM((tm, D_MODEL), BF16), pltpu.VMEM((tm, LANES), F32),
                        pltpu.VMEM((tm, D_MODEL), F32)],
        compiler_params=_cparams(("parallel", "arbitrary")),
        name="moe",
    )(x, g, wr, wrt, br, brt, wgu, wd, gf)


def _moe_prepare(w_rg, b_rg, w_re, b_re, w_gate, w_up, w_down):
    wr = jnp.zeros((D_MODEL, LANES), F32)
    wr = wr.at[:, :N_GROUPS].set(w_rg).at[:, ROUTE_E0:ROUTE_E0 + N_EXPERTS].set(w_re)
    br = jnp.zeros((1, LANES), F32)
    br = br.at[0, :N_GROUPS].set(b_rg).at[0, ROUTE_E0:ROUTE_E0 + N_EXPERTS].set(b_re)
    wr2 = jnp.stack(_hilo(wr))
    return dict(wr=wr2, wrt=wr2.transpose(0, 2, 1), br=br, brt=br.reshape(LANES, 1),
                wgu=jnp.concatenate([w_gate, w_up], axis=-1).astype(BF16),
                wd=w_down.astype(BF16))


CMP_SUB = 256
CMP_W = 4 * HEAD_DIM
PAGES_PER_STEP = 64


def _cmp_perm():
    p = np.zeros((CMP_SUB, CMP_SUB), np.float32)
    for i in range(CMP_SUB):
        p[i, CMP_BLOCK * (i % SUBLANES) + i // SUBLANES] = 1.0
    return p


def _cmp_prepare(pe, w1, w2):
    eye4 = jnp.eye(4, dtype=F32)
    sel = jnp.array([0, 0, 1, 1])
    w1r = w1.reshape(2, CMP_BLOCK, HEAD_DIM, CMP_HIDDEN)[sel]
    w1bd = (w1r.transpose(1, 0, 2, 3)[:, :, :, None, :] * eye4[None, :, None, :, None])
    w1bd = w1bd.reshape(CMP_BLOCK, CMP_W, 4 * CMP_HIDDEN).astype(BF16)
    w2r = w2[sel]
    w2bd = (w2r[:, :, None, :] * eye4[:, None, :, None]).reshape(4 * CMP_HIDDEN, CMP_W).astype(BF16)
    pe4 = pe[sel].transpose(1, 0, 2).reshape(CMP_BLOCK, CMP_W)
    pe_exp = jnp.repeat(pe4, SUBLANES, axis=0)
    return jnp.asarray(_cmp_perm(), BF16), pe_exp, w1bd, w2bd


def _compress_rows(get_sub, nsub, perm_ref, pe_ref, w1_ref, w2_ref, stage, transposed=False):
    mm = _dot_nt if transposed else _dot
    for t in range(nsub):
        xp = mm(perm_ref[...], get_sub(t).astype(BF16))
        stage[t] = xp + pe_ref[...]
    acc = jnp.zeros((nsub * SUBLANES, 4 * CMP_HIDDEN), F32)
    for r in range(CMP_BLOCK):
        a = stage[:, r * SUBLANES:(r + 1) * SUBLANES, :].reshape(nsub * SUBLANES, CMP_W)
        acc = acc + _dot(a.astype(BF16), w1_ref[r])
    return _dot(_gelu_tanh(acc).astype(BF16), w2_ref[...])


def _compress_seq_body(x_ref, perm_ref, pe_ref, w1_ref, w2_ref, o_ref, stage, *, nsub):
    o_ref[...] = _compress_rows(lambda t: x_ref[:, t * CMP_SUB:(t + 1) * CMP_SUB], nsub,
                                perm_ref, pe_ref, w1_ref, w2_ref, stage, transposed=True)


def _compress_seq(kvp_t, layer, cw, bsz, seq):
    nsub = seq // CMP_SUB
    nblk = seq // CMP_BLOCK
    return pl.pallas_call(
        functools.partial(_compress_seq_body, nsub=nsub),
        grid=(bsz,),
        in_specs=[pl.BlockSpec((None, None, CMP_W, seq), lambda b: (layer, b, 0, 0))]
        + [_full(c.shape) for c in cw],
        out_specs=pl.BlockSpec((None, nblk, CMP_W), lambda b: (b, 0, 0)),
        out_shape=jax.ShapeDtypeStruct((bsz, nblk, CMP_W), F32),
        scratch_shapes=[pltpu.VMEM((nsub, CMP_SUB, CMP_W), F32)],
        compiler_params=_cparams(("parallel",)),
        name="compress_seq",
    )(kvp_t, *cw)


def _compress_pages_body(pt_ref, *refs):
    pages = refs[:PAGES_PER_STEP]
    perm_ref, pe_ref, w1_ref, w2_ref, o_ref, stage = refs[PAGES_PER_STEP:]
    per = CMP_SUB // PAGE

    def get_sub(t):
        return jnp.concatenate([pages[per * t + k][...] for k in range(per)], axis=1)

    o_ref[...] = _compress_rows(get_sub, PAGES_PER_STEP // per, perm_ref, pe_ref, w1_ref, w2_ref, stage,
                                transposed=True)


def _compress_pages(cache_t, page_table, layer, cw):
    bsz, n_pages = page_table.shape
    steps = n_pages // PAGES_PER_STEP
    nsub = PAGES_PER_STEP * PAGE // CMP_SUB
    blk_per_step = PAGES_PER_STEP * PAGE // CMP_BLOCK

    def page_spec(k):
        return pl.BlockSpec((None, None, CMP_W, PAGE),
                            lambda b, i, pt: (layer, pt[b * n_pages + i * PAGES_PER_STEP + k], 0, 0))

    grid_spec = pltpu.PrefetchScalarGridSpec(
        num_scalar_prefetch=1,
        grid=(bsz, steps),
        in_specs=[page_spec(k) for k in range(PAGES_PER_STEP)]
        + [pl.BlockSpec(c.shape, lambda b, i, pt, nd=c.ndim: (0,) * nd) for c in cw],
        out_specs=pl.BlockSpec((None, blk_per_step, CMP_W), lambda b, i, pt: (b, i, 0)),
        scratch_shapes=[pltpu.VMEM((nsub, CMP_SUB, CMP_W), F32)],
    )
    return pl.pallas_call(
        _compress_pages_body,
        grid_spec=grid_spec,
        out_shape=jax.ShapeDtypeStruct((bsz, n_pages * PAGE // CMP_BLOCK, CMP_W), F32),
        compiler_params=_cparams(("parallel", "arbitrary")),
        name="compress_pages",
    )(page_table.reshape(-1), *([cache_t] * PAGES_PER_STEP), *cw)


TQ = 128
SLC_CHUNK = 512
QL = GRP * TQ
N_SELBLK_SEQ = 32
AUG_MASK = HEAD_DIM
AUG_POS = HEAD_DIM + 32
POS_SPLIT = 128


def _alibi_slopes():
    return [2.0 ** (-8.0 * (h + 1) / N_HEADS) for h in range(N_HEADS)]


def _nsa_tables(seq):
    pos = np.arange(seq)
    tbl = np.zeros((seq, HEAD_DIM), np.float32)
    tbl[pos, pos // SEL_BLOCK] = 1.0
    tbl[:, 32] = (pos // POS_SPLIT) * POS_SPLIT
    tbl[:, 33] = pos % POS_SPLIT
    ncmp = seq // CMP_BLOCK
    order = np.concatenate([np.arange(0, ncmp, 2), np.arange(1, ncmp, 2)])
    cend = (order + 1) * CMP_BLOCK - 1
    ctbl = np.zeros((ncmp, HEAD_DIM), np.float32)
    ctbl[:, 32] = (cend // POS_SPLIT) * POS_SPLIT
    ctbl[:, 33] = cend % POS_SPLIT
    pm = np.zeros((ncmp, ncmp), np.float32)
    pm[np.arange(ncmp), order] = 1.0
    slope = np.zeros((KV_HEADS, 16, QL), np.float32)
    sl = _alibi_slopes()
    for g in range(KV_HEADS):
        for m in range(GRP):
            slope[g, 0:2, m * TQ:(m + 1) * TQ] = sl[g * GRP + m]
    return (jnp.asarray(tbl), jnp.asarray(ctbl), jnp.asarray(cend.reshape(ncmp, 1).astype(np.int32)),
            jnp.asarray(pm, BF16), jnp.asarray(slope, BF16))


def _flash_chunk(kaug, vt, qa, valid, state):
    m, l, acc = state
    s = _dot(kaug, qa)
    if valid is not None:
        s = jnp.where(valid, s, NEG)
    mn = jnp.maximum(m, jnp.max(s, axis=0, keepdims=True))
    alpha = jnp.exp(m - mn)
    p = jnp.exp(s - mn)
    l = alpha * l + jnp.sum(p, axis=0, keepdims=True)
    acc = alpha * acc + _dot(vt, p.astype(BF16))
    return mn, l, acc


def _flash_init():
    return (jnp.full((1, QL), M_INIT, F32), jnp.zeros((1, QL), F32), jnp.zeros((HEAD_DIM, QL), F32))


def _nsa_seq_body(q_ref, kvp_ref, kvw_ref, gt_ref, kc_ref, tbl_ref, ctbl_ref, cend_ref, pm_ref,
                  slope_ref, y_ref, kslc, kwin, vslc, vwin, qaug):
    i = pl.program_id(1)
    l0 = i * TQ

    @pl.when(i == 0)
    def _():
        kslc[...] = jnp.zeros_like(kslc)
        kwin[...] = jnp.zeros_like(kwin)
        vslc[...] = jnp.zeros_like(vslc)
        vwin[...] = jnp.zeros_like(vwin)

    q = q_ref[...]
    tbl = tbl_ref[...]
    lane64 = lax.broadcasted_iota(jnp.int32, (TQ, HEAD_DIM), 1)
    tblw = jnp.where(lane64 < 32, 0.0, tbl)
    rows = pl.ds(pl.multiple_of(l0, TQ), TQ)
    ks_all = kvp_ref[0:KV_DIM, :].T
    kw_all = kvw_ref[0:KV_DIM, :].T
    for g in range(KV_HEADS):
        ks = ks_all[:, g * HEAD_DIM:(g + 1) * HEAD_DIM]
        kslc[g, rows, :] = jnp.concatenate([ks, tbl], axis=1).astype(BF16)
        kw = kw_all[:, g * HEAD_DIM:(g + 1) * HEAD_DIM]
        kwin[g, rows, :] = jnp.concatenate([kw, tblw], axis=1).astype(BF16)
    vslc[i] = kvp_ref[KV_DIM:2 * KV_DIM, :].astype(BF16)
    vwin[i] = kvw_ref[KV_DIM:2 * KV_DIM, :].astype(BF16)

    scale = HEAD_DIM ** -0.5
    qt = [(q[:, j * LANES:(j + 1) * LANES] * scale).T for j in range(ATTN_DIM // LANES)]
    for g in range(KV_HEADS):
        heads = []
        for m in range(GRP):
            h = g * GRP + m
            heads.append(qt[h // 2][(h % 2) * HEAD_DIM:(h % 2 + 1) * HEAD_DIM, :])
        qaug[g, 0:HEAD_DIM, :] = jnp.concatenate(heads, axis=1).astype(BF16)
        qaug[g, AUG_MASK:AUG_POS, :] = jnp.zeros((AUG_POS - AUG_MASK, QL), BF16)
        qaug[g, AUG_POS:AUG_POS + 16, :] = slope_ref[g]
        qaug[g, AUG_POS + 16:, :] = jnp.zeros((2 * HEAD_DIM - AUG_POS - 16, QL), BF16)

    ncmp = kc_ref.shape[0]
    nsel = ncmp // 2
    kcp = _dot(pm_ref[...], kc_ref[...].astype(BF16))
    vct = jnp.concatenate([kcp[:, 2 * KV_DIM - KV_DIM:2 * KV_DIM],
                           jnp.zeros((LANES - ncmp, KV_DIM), F32)], axis=0).T
    lpos = l0 + lax.broadcasted_iota(jnp.int32, (1, QL), 1) % TQ
    valid_c = cend_ref[...] <= lpos
    lq = l0 + lax.broadcasted_iota(jnp.int32, (nsel, TQ), 1)
    blk = lax.broadcasted_iota(jnp.int32, (nsel, TQ), 0)
    cur = lq // SEL_BLOCK
    forced = (blk == 0) | (blk == cur) | (blk == cur - 1)
    causal_blk = blk * SEL_BLOCK <= lq
    o_cmp = []
    for g in range(KV_HEADS):
        kca = jnp.concatenate([kcp[:, g * HEAD_DIM:(g + 1) * HEAD_DIM], ctbl_ref[...]], axis=1).astype(BF16)
        s = jnp.where(valid_c, _dot(kca, qaug[g]), NEG)
        mx = jnp.max(s, axis=0, keepdims=True)
        e = jnp.where(valid_c, jnp.exp(s - mx), 0.0)
        p = e * (1.0 / jnp.maximum(jnp.sum(e, axis=0, keepdims=True), 1.0))
        vt_g = vct[g * HEAD_DIM:(g + 1) * HEAD_DIM, 0:ncmp].astype(BF16)
        o_cmp.append(_dot(vt_g, p.astype(BF16)))
        psum = p[:, 0:TQ]
        for m in range(1, GRP):
            psum = psum + p[:, m * TQ:(m + 1) * TQ]
        imp = psum[0:nsel, :] + psum[nsel:, :]
        score = jnp.where(causal_blk, imp + jnp.where(forced, FORCE_BONUS, 0.0), NEG)
        sel = jnp.zeros((nsel, TQ), jnp.bool_)
        for _ in range(N_SELECT):
            best = jnp.max(score, axis=0, keepdims=True)
            idx = jnp.min(jnp.where(score == best, blk, nsel), axis=0, keepdims=True)
            hit = (blk == idx) & (best > 0.5 * NEG)
            sel = sel | hit
            score = jnp.where(hit, 2.0 * NEG, score)
        mb = jnp.where(sel, 0.0, NEG)
        qaug[g, AUG_MASK:AUG_POS, :] = jnp.concatenate([mb] * GRP, axis=1).astype(BF16)

    qpos = l0 + lax.broadcasted_iota(jnp.int32, (1, QL), 1) % TQ
    per = SLC_CHUNK // TQ

    def slc_chunk(cb, st, valid):
        k0 = pl.multiple_of(cb * SLC_CHUNK, SLC_CHUNK)
        out = []
        for g in range(KV_HEADS):
            vt = jnp.concatenate([vslc[cb * per + k, g * HEAD_DIM:(g + 1) * HEAD_DIM, :] for k in range(per)], axis=1)
            out.append(_flash_chunk(kslc[g, pl.ds(k0, SLC_CHUNK), :], vt, qaug[g], valid, st[g]))
        return tuple(out)

    cbd = i // per
    st = lax.fori_loop(0, cbd, lambda cb, s: slc_chunk(cb, s, None), tuple(_flash_init() for _ in range(KV_HEADS)))
    kpos = cbd * SLC_CHUNK + lax.broadcasted_iota(jnp.int32, (SLC_CHUNK, 1), 0)
    st = slc_chunk(cbd, st, kpos <= qpos)
    o_slc = [acc * (1.0 / l) for (m, l, acc) in st]

    nwc = WINDOW // TQ + 1
    cw0 = jnp.maximum(i - (nwc - 1), 0)
    kw0 = pl.multiple_of(cw0 * TQ, TQ)
    dist = qpos - (kw0 + lax.broadcasted_iota(jnp.int32, (nwc * TQ, 1), 0))
    vis = (dist >= 0) & (dist < WINDOW)
    o_win = []
    for g in range(KV_HEADS):
        vt = jnp.concatenate([vwin[cw0 + k, g * HEAD_DIM:(g + 1) * HEAD_DIM, :] for k in range(nwc)], axis=1)
        m, l, acc = _flash_chunk(kwin[g, pl.ds(kw0, nwc * TQ), :], vt, qaug[g], vis, _flash_init())
        o_win.append(acc * (1.0 / l))

    sg = _sigmoid(gt_ref[...].T[0:3 * N_HEADS, :])
    outs = []
    for g in range(KV_HEADS):
        def gate_row(br):
            return jnp.concatenate([sg[br * N_HEADS + g * GRP + m:br * N_HEADS + g * GRP + m + 1, :]
                                    for m in range(GRP)], axis=1)
        outs.append(gate_row(0) * o_cmp[g] + gate_row(1) * o_slc[g] + gate_row(2) * o_win[g])
    for j in range(ATTN_DIM // LANES):
        rows = []
        for h in (2 * j, 2 * j + 1):
            g, m = divmod(h, GRP)
            rows.append(outs[g][:, m * TQ:(m + 1) * TQ])
        y_ref[:, j * LANES:(j + 1) * LANES] = jnp.concatenate(rows, axis=0).T


def _nsa_seq(q, kvp_t, kvw_t, layer, gt, kc, bsz, seq):
    assert seq % SLC_CHUNK == 0 and seq >= WINDOW + TQ
    assert seq // SEL_BLOCK == AUG_POS - AUG_MASK
    nt = seq // TQ
    tbl, ctbl, cend, pm, slope = _nsa_tables(seq)
    ncmp = seq // CMP_BLOCK
    row = lambda w: pl.BlockSpec((TQ, w), lambda b, i: (b * nt + i, 0))
    return pl.pallas_call(
        _nsa_seq_body,
        grid=(bsz, nt),
        in_specs=[row(ATTN_DIM),
                  pl.BlockSpec((None, None, 2 * KV_DIM, TQ), lambda b, i: (layer, b, 1, i)),
                  pl.BlockSpec((None, None, 2 * KV_DIM, TQ), lambda b, i: (layer, b, 0, i)),
                  row(GATE_PAD),
                  pl.BlockSpec((None, ncmp, CMP_W), lambda b, i: (b, 0, 0)),
                  pl.BlockSpec((TQ, HEAD_DIM), lambda b, i: (i, 0)),
                  _full(ctbl.shape), _full(cend.shape), _full(pm.shape), _full(slope.shape)],
        out_specs=row(ATTN_DIM),
        out_shape=jax.ShapeDtypeStruct((bsz * seq, ATTN_DIM), F32),
        scratch_shapes=[pltpu.VMEM((KV_HEADS, seq, 2 * HEAD_DIM), BF16),
                        pltpu.VMEM((KV_HEADS, seq, 2 * HEAD_DIM), BF16),
                        pltpu.VMEM((nt, KV_DIM, TQ), BF16),
                        pltpu.VMEM((nt, KV_DIM, TQ), BF16),
                        pltpu.VMEM((KV_HEADS, 2 * HEAD_DIM, QL), BF16)],
        compiler_params=_cparams(("parallel", "arbitrary")),
        name="nsa_seq",
    )(q, kvp_t, kvw_t, gt, kc, tbl, ctbl, cend, pm, slope)


N_PICK = N_SELECT - 1


def _q_rows(q, g, scale):
    rows = [q[:, (g * GRP + m) * HEAD_DIM:(g * GRP + m + 1) * HEAD_DIM] for m in range(GRP)]
    return jnp.concatenate(rows + [jnp.zeros((SUBLANES - GRP, HEAD_DIM), F32)], axis=0) * scale


def _slope_col(g):
    sl = _alibi_slopes()
    row = lax.broadcasted_iota(jnp.int32, (SUBLANES, 1), 0)
    col = jnp.zeros((SUBLANES, 1), F32)
    for m in range(GRP):
        col = jnp.where(row == m, sl[g * GRP + m], col)
    return col


def _nsa_step_cmp_body(q_ref, kc_ref, pair_ref, o_ref, idx_ref, *, past_len):
    q = q_ref[...]
    kc = kc_ref[...]
    ncmp = kc.shape[0]
    nsel = ncmp // 2
    cur = past_len // SEL_BLOCK
    scale = HEAD_DIM ** -0.5
    cend = (lax.broadcasted_iota(jnp.int32, (1, ncmp), 1) + 1) * CMP_BLOCK - 1
    dist = (past_len - cend).astype(F32)
    lane = lax.broadcasted_iota(jnp.int32, (1, nsel), 1)
    forced = (lane == 0) | (lane == cur) | (lane == cur - 1)
    lane_o = lax.broadcasted_iota(jnp.int32, (1, LANES), 1)
    idx_out = jnp.zeros((1, LANES), jnp.int32)
    for g in range(KV_HEADS):
        q8 = _q_rows(q, g, scale).astype(BF16)
        s = _dot_nt(q8, kc[:, g * HEAD_DIM:(g + 1) * HEAD_DIM].astype(BF16)) - _slope_col(g) * dist
        mx = jnp.max(s, axis=1, keepdims=True)
        e = jnp.exp(s - mx)
        p = e / jnp.maximum(jnp.sum(e, axis=1, keepdims=True), 1.0)
        vg = kc[:, KV_DIM + g * HEAD_DIM:KV_DIM + (g + 1) * HEAD_DIM]
        o_ref[g] = _dot(p.astype(BF16), vg.astype(BF16))
        psum = jnp.sum(p[0:GRP, :], axis=0, keepdims=True)
        p_hi, p_mid, p_lo = _split3(jnp.broadcast_to(psum, (SUBLANES, ncmp)))
        pair = pair_ref[...]
        imp = (_dot(p_hi, pair) + _dot(p_mid, pair) + _dot(p_lo, pair))[0:1, :]
        score = imp + jnp.where(forced, FORCE_BONUS, 0.0)
        for t in range(N_PICK):
            best = jnp.max(score, axis=1, keepdims=True)
            idx = jnp.min(jnp.where(score == best, lane, nsel), axis=1, keepdims=True)
            score = jnp.where(lane == idx, NEG, score)
            idx_out = jnp.where(lane_o == g * N_SELECT + t, idx, idx_out)
    idx_ref[...] = idx_out


def _nsa_step_attn_body(sel_ref, pt_ref, *refs, past_len):
    nb = KV_HEADS * N_PICK
    blocks = refs[:nb]
    win_ref, q_ref, gt_ref, kvp_ref, kvw_ref, ocmp_ref, y_ref = refs[nb:]
    b = pl.program_id(0)
    q = q_ref[...]
    kvp = kvp_ref[...]
    kvw = kvw_ref[...]
    sg = _sigmoid(gt_ref[...])
    w = win_ref[...]
    nwin = w.shape[1]
    scale = HEAD_DIM ** -0.5
    pl_ = lax.broadcasted_iota(jnp.int32, (1, PAGE), 1)
    wl = lax.broadcasted_iota(jnp.int32, (1, nwin), 1)
    per_page = PAGE // SEL_BLOCK
    for g in range(KV_HEADS):
        q8 = _q_rows(q, g, scale)
        q8b = q8.astype(BF16)
        sc = _slope_col(g)
        hs = slice(g * HEAD_DIM, (g + 1) * HEAD_DIM)
        vs_ = slice(KV_DIM + g * HEAD_DIM, KV_DIM + (g + 1) * HEAD_DIM)
        k_new = kvp[:, 2 * KV_DIM + g * HEAD_DIM:2 * KV_DIM + (g + 1) * HEAD_DIM]
        v_new = kvp[:, 3 * KV_DIM + g * HEAD_DIM:3 * KV_DIM + (g + 1) * HEAD_DIM]
        s_cur = jnp.sum(q8 * k_new, axis=1, keepdims=True)
        ss, vs = [], []
        for t in range(N_PICK):
            blk = blocks[g * N_PICK + t][...]
            s_idx = sel_ref[b * KV_HEADS * N_SELECT + g * N_SELECT + t]
            kpos = (s_idx // per_page) * PAGE + pl_
            inblk = pl_ // SEL_BLOCK == s_idx % per_page
            st = _dot(q8b, blk[hs, :].astype(BF16)) - sc * (past_len - kpos).astype(F32)
            ss.append(jnp.where(inblk, st, NEG))
            vs.append(blk[vs_, :].astype(BF16))
        mx = functools.reduce(jnp.maximum, [jnp.max(s, axis=1, keepdims=True) for s in ss] + [s_cur])
        es = [jnp.exp(s - mx) for s in ss]
        e_cur = jnp.exp(s_cur - mx)
        den = functools.reduce(lambda a, c: a + c, [jnp.sum(e, axis=1, keepdims=True) for e in es] + [e_cur])
        acc = e_cur * v_new
        for e, v in zip(es, vs):
            acc = acc + _dot_nt(e.astype(BF16), v)
        o_slc = acc / den
        kw_new = kvw[:, g * HEAD_DIM:(g + 1) * HEAD_DIM]
        vw_new = kvw[:, KV_DIM + g * HEAD_DIM:KV_DIM + (g + 1) * HEAD_DIM]
        dw = (nwin - wl).astype(F32)
        s_w = jnp.where(nwin - wl < WINDOW, _dot(q8b, w[hs, :].astype(BF16)) - sc * dw, NEG)
        s_wc = jnp.sum(q8 * kw_new, axis=1, keepdims=True)
        mw = jnp.maximum(jnp.max(s_w, axis=1, keepdims=True), s_wc)
        e_w = jnp.exp(s_w - mw)
        e_wc = jnp.exp(s_wc - mw)
        o_win = (_dot_nt(e_w.astype(BF16), w[vs_, :].astype(BF16))
                 + e_wc * vw_new) / (jnp.sum(e_w, axis=1, keepdims=True) + e_wc)

        def gate_col(br):
            cols = [sg[:, br * N_HEADS + g * GRP + m:br * N_HEADS + g * GRP + m + 1] for m in range(GRP)]
            return jnp.concatenate(cols + [jnp.zeros((SUBLANES - GRP, 1), F32)], axis=0)

        y_ref[g] = gate_col(0) * ocmp_ref[g] + gate_col(1) * o_slc + gate_col(2) * o_win


def _nsa_step(q, kvp, kvw, gt, kc, cache, page_table, cache_win, layer):
    n, n_pages = page_table.shape
    past_len = n_pages * PAGE
    ncmp = kc.shape[1]
    nsel = ncmp // 2
    pair = np.zeros((ncmp, nsel), np.float32)
    pair[np.arange(ncmp), np.arange(ncmp) // 2] = 1.0
    r3 = lambda a: a.reshape(n, 1, a.shape[-1])
    row3 = lambda w: pl.BlockSpec((None, 1, w), lambda b, *_: (b, 0, 0))
    o_cmp, idx = pl.pallas_call(
        functools.partial(_nsa_step_cmp_body, past_len=past_len),
        grid=(n,),
        in_specs=[row3(ATTN_DIM), pl.BlockSpec((None, ncmp, CMP_W), lambda b: (b, 0, 0)), _full(pair.shape)],
        out_specs=[pl.BlockSpec((None, KV_HEADS, SUBLANES, HEAD_DIM), lambda b: (b, 0, 0, 0)), row3(LANES)],
        out_shape=[jax.ShapeDtypeStruct((n, KV_HEADS, SUBLANES, HEAD_DIM), F32),
                   jax.ShapeDtypeStruct((n, 1, LANES), jnp.int32)],
        compiler_params=_cparams(("parallel",)),
        name="nsa_step_cmp",
    )(r3(q), kc, jnp.asarray(pair, BF16))
    sel = idx[:, 0, :KV_HEADS * N_SELECT].reshape(-1)

    def blk_spec(j):
        g, t = divmod(j, N_PICK)

        def imap(b, sel_ref, pt_ref):
            s = sel_ref[b * KV_HEADS * N_SELECT + g * N_SELECT + t]
            return (layer, pt_ref[b * n_pages + s // (PAGE // SEL_BLOCK)], 1, 0)

        return pl.BlockSpec((None, None, 2 * KV_DIM, PAGE), imap)

    nb = KV_HEADS * N_PICK
    grid_spec = pltpu.PrefetchScalarGridSpec(
        num_scalar_prefetch=2,
        grid=(n,),
        in_specs=[blk_spec(j) for j in range(nb)]
        + [pl.BlockSpec((None, None, CMP_W, cache_win.shape[3]), lambda b, *_: (layer, b, 0, 0)),
           row3(ATTN_DIM), row3(GATE_PAD), row3(4 * KV_DIM), row3(2 * KV_DIM),
           pl.BlockSpec((None, KV_HEADS, SUBLANES, HEAD_DIM), lambda b, *_: (b, 0, 0, 0))],
        out_specs=pl.BlockSpec((None, KV_HEADS, SUBLANES, HEAD_DIM), lambda b, *_: (b, 0, 0, 0)),
    )
    y = pl.pallas_call(
        functools.partial(_nsa_step_attn_body, past_len=past_len),
        grid_spec=grid_spec,
        out_shape=jax.ShapeDtypeStruct((n, KV_HEADS, SUBLANES, HEAD_DIM), F32),
        compiler_params=_cparams(("arbitrary",)),
        name="nsa_step_attn",
    )(sel, page_table.reshape(-1), *([cache] * nb), cache_win, r3(q), r3(gt), r3(kvp), r3(kvw), o_cmp)
    return y[:, :, :GRP, :].reshape(n, ATTN_DIM)


def _patch_tail_body(t_ref, big_ref, o_ref):
    del big_ref
    o_ref[...] = t_ref[...]


def _patch_tail(x, xt, bsz, seq):
    nt = seq // TAIL
    return pl.pallas_call(
        _patch_tail_body,
        grid=(bsz,),
        in_specs=[pl.BlockSpec((TAIL, D_MODEL), lambda b: (b, 0)), pl.BlockSpec(memory_space=pl.ANY)],
        out_specs=pl.BlockSpec((TAIL, D_MODEL), lambda b: (b * nt + nt - 1, 0)),
        out_shape=jax.ShapeDtypeStruct(x.shape, x.dtype),
        input_output_aliases={1: 0},
        compiler_params=_cparams(("parallel",)),
        name="patch_tail",
    )(xt, x)


def kernel(x_prompt, x_sample, cache_kv, page_table, cache_win, state_ssm, state_conv, norm_attn_g, w_in,
           ssm_a_re, ssm_a_im, ssm_log_dt, ssm_b_re, ssm_b_im, ssm_c_re, ssm_c_im, ssm_d, ssm_w_glu, ssm_b_glu,
           cmp_pe, cmp_w1, cmp_w2, conv_w, w_br_ssm, w_br_attn, w_br_conv, w_out, norm_ffn_g,
           w_router_group, b_router_group, w_router_expert, b_router_expert, moe_w_gate, moe_w_up, moe_w_down,
           norm_final_g):
    bp, lp, _ = x_prompt.shape
    bs = x_sample.shape[0]
    depth = w_in.shape[0]
    n_pool = cache_kv.shape[1]
    nwin = cache_win.shape[2]
    xp = x_prompt.reshape(bp * lp, D_MODEL)
    xs = x_sample.reshape(bs, D_MODEL)
    cache = cache_kv.transpose(0, 1, 3, 4, 5, 2).reshape(depth, n_pool, 4 * KV_DIM, PAGE)
    cwin = cache_win.transpose(0, 1, 3, 4, 5, 2).reshape(depth, bs, 2 * KV_DIM, nwin)
    gf = norm_final_g.reshape(1, D_MODEL)
    kv_s, win_s, ssm_p, ssm_s, conv_p, conv_s = ([] for _ in range(6))
    kv_bufs = None

    def ssm_state(hre, him, n):
        return jnp.stack([hre.reshape(n, SSM_GROUPS, SSM_STATE), him.reshape(n, SSM_GROUPS, SSM_STATE)], axis=-1)

    for l in range(depth):
        w = w_in[l]
        w1, w1_lo = _hilo(jnp.concatenate(
            [w[:, :C_GATE], jnp.pad(w[:, C_GATE:C_CONV], ((0, 0), (0, GATE_PAD - 3 * N_HEADS)))], axis=1))
        g_attn = norm_attn_g[l].reshape(1, D_MODEL)
        g_ffn = norm_ffn_g[l].reshape(1, D_MODEL)
        s5w = _s5_prepare(ssm_a_re[l], ssm_a_im[l], ssm_log_dt[l], ssm_b_re[l], ssm_b_im[l],
                          ssm_c_re[l], ssm_c_im[l], ssm_d[l], ssm_w_glu[l], ssm_b_glu[l])
        cw = _cmp_prepare(cmp_pe[l], cmp_w1[l], cmp_w2[l])
        mw = _merge_weights(w[:, C_CONV:], w_br_ssm[l], w_br_attn[l], w_br_conv[l], w_out[l], conv_w[l])
        mo = _moe_prepare(w_router_group[l], b_router_group[l], w_router_expert[l], b_router_expert[l],
                          moe_w_gate[l], moe_w_up[l], moe_w_down[l])
        final = l == depth - 1

        u, q, gt, *kv_bufs = _inproj_seq(xp, g_attn, w1, bp, lp, l, depth, kv_bufs)
        y_ssm, hre, him, pre_r, pre_i = _s5_seq(u, s5w, bp, lp)
        kc = _compress_seq(kv_bufs[0], l, cw, bp, lp)
        y_attn = _nsa_seq(q, kv_bufs[0], kv_bufs[1], l, gt, kc, bp, lp)
        x1, cl, cpre = _merge_seq(xp, g_attn, y_ssm, y_attn, mw, bp, lp)
        x2 = _moe(x1, g_ffn, mo, gf, 1024, final=final)
        if not final and lp >= 2 * TAIL:
            tail = lambda a: a.reshape(bp, lp, a.shape[-1])[:, lp - TAIL:].reshape(bp * TAIL, a.shape[-1])
            xt = tail(xp)
            ut = _inproj(xt, g_attn, w1, TAIL, w1_lo)[0]
            yst = _s5_seq(ut, s5w, bp, TAIL, tl=TAIL, precise=True, h0=(pre_r, pre_i))[0]
            x1t = _merge_seq(xt, g_attn, yst, tail(y_attn), mw, bp, TAIL, tm=TAIL, precise=True, cin=cpre)[0]
            x2t = _moe(x1t, g_ffn, mo, gf, min(1024, bp * TAIL), final=False)
            x2 = _patch_tail(x2, x2t, bp, lp)
        xp = x2
        ssm_p.append(ssm_state(hre[:, 0], him[:, 0], bp))
        conv_p.append(cl[:, SUBLANES - 2:])

        u, q, kvp, kvw, gt = _inproj(xs, g_attn, w1, bs, w1_lo)
        st = state_ssm[l]
        y_ssm, hre, him = _s5_step(u, st[..., 0].reshape(bs, SSM_N), st[..., 1].reshape(bs, SSM_N), s5w)
        kc = _compress_pages(cache, page_table, l, cw)
        y_attn = _nsa_step(q, kvp, kvw, gt, kc, cache, page_table, cwin, l)
        prev = state_conv[l]
        x1, uc = _merge_step(xs, g_attn, y_ssm, y_attn, prev[:, 0], prev[:, 1], mw)
        xs = _moe(x1, g_ffn, mo, gf, bs, final=final)
        kv_s.append(kvp.reshape(bs, 1, 4, KV_HEADS, HEAD_DIM))
        win_s.append(jnp.concatenate([cache_win[l][:, 1:], kvw.reshape(bs, 1, 2, KV_HEADS, HEAD_DIM)], axis=1))
        ssm_s.append(ssm_state(hre, him, bs))
        conv_s.append(jnp.stack([prev[:, 1], uc], axis=1))

    kvp_t, kvw_t = kv_bufs
    nw = min(WINDOW, lp)
    kv_prompt = kvp_t.reshape(depth, bp, 4, KV_HEADS, HEAD_DIM, lp).transpose(0, 1, 5, 2, 3, 4)
    win_prompt = kvw_t[..., lp - nw:].reshape(depth, bp, 2, KV_HEADS, HEAD_DIM, nw).transpose(0, 1, 5, 2, 3, 4)
    return (xp.reshape(bp, lp, D_MODEL), xs.reshape(bs, 1, D_MODEL),
            kv_prompt, jnp.stack(kv_s), win_prompt, jnp.stack(win_s),
            jnp.stack(ssm_p), jnp.stack(ssm_s), jnp.stack(conv_p), jnp.stack(conv_s))
```

```python
import functools
import math

import numpy as np
import jax
import jax.numpy as jnp
from jax import lax
from jax.experimental import pallas as pl
from jax.experimental.pallas import tpu as pltpu

F32 = jnp.float32
BF16 = jnp.bfloat16

D_MODEL = 1024
DEPTH = 2
PAGE = 128
SSM_GROUPS = 24
SSM_CH = 16
SSM_DIM = SSM_GROUPS * SSM_CH
SSM_STATE = 64
SSM_N = SSM_GROUPS * SSM_STATE
N_HEADS = 8
HEAD_DIM = 64
KV_HEADS = 2
GRP = N_HEADS // KV_HEADS
ATTN_DIM = N_HEADS * HEAD_DIM
KV_DIM = KV_HEADS * HEAD_DIM
CMP_BLOCK = 32
CMP_HIDDEN = 64
SEL_BLOCK = 64
N_SELECT = 8
WINDOW = 512
FORCE_BONUS = 1e4
CONV_DIM = 384
N_GROUPS = 4
EPG = 4
N_EXPERTS = 16
D_FF = 256
RMS_EPS = 1e-6

C_U = 0
C_Q = C_U + SSM_DIM
C_KVP = C_Q + ATTN_DIM
C_KVW = C_KVP + 4 * KV_DIM
C_GATE = C_KVW + 2 * KV_DIM
C_CONV = C_GATE + 3 * N_HEADS
C_MERGE = C_CONV + 3 * CONV_DIM
N_IN = C_MERGE + 3 * D_MODEL
W2_COLS = N_IN - C_CONV
GATE_PAD = 128
W1_COLS = C_GATE + GATE_PAD

LANES = 128
SUBLANES = 8
NEG = -1e30
M_INIT = -1e29
VMEM_LIMIT = 56 * 1024 * 1024


def _cparams(sem):
    return pltpu.CompilerParams(dimension_semantics=sem, vmem_limit_bytes=VMEM_LIMIT)


def _rms(x, g):
    ms = jnp.mean(x * x, axis=-1, keepdims=True)
    return x * lax.rsqrt(ms + RMS_EPS) * g


def _gelu_tanh(x):
    return 0.5 * x * (1.0 + jnp.tanh(math.sqrt(2.0 / math.pi) * (x + 0.044715 * (x * x * x))))


def _sigmoid(x):
    return 1.0 / (1.0 + jnp.exp(-x))


def _dot(a, b):
    return jnp.dot(a, b, preferred_element_type=F32)


def _dot_nt(a, b):
    return lax.dot_general(a, b, (((1,), (1,)), ((), ())), preferred_element_type=F32)


def _split3(x):
    hi = x.astype(BF16)
    r1 = x - hi.astype(F32)
    mid = r1.astype(BF16)
    lo = (r1 - mid.astype(F32)).astype(BF16)
    return hi, mid, lo


def _mm(a, w_ref, wlo_ref=None):
    ah = a.astype(BF16)
    if wlo_ref is None:
        return _dot(ah, w_ref[...])
    al = (a - ah.astype(F32)).astype(BF16)
    return _dot(ah, w_ref[...]) + (_dot(al, w_ref[...]) + _dot(ah, wlo_ref[...]))


def _hilo(w):
    hi = w.astype(BF16)
    return hi, (w - hi.astype(F32)).astype(BF16)


def _full(shape):
    nd = len(shape)
    return pl.BlockSpec(shape, lambda *_: (0,) * nd)


def _full1(shape):
    nd = len(shape)
    return pl.BlockSpec(shape, lambda *_: (0,) * nd, pipeline_mode=pl.Buffered(1))


def _inproj_body(x_ref, g_ref, w_ref, *refs, precise):
    wlo_ref = refs[0] if precise else None
    u_ref, q_ref, kvp_ref, kvw_ref, gt_ref = refs[1:] if precise else refs
    h = _rms(x_ref[...], g_ref[...])

    def proj(a, b):
        return _mm(h, w_ref.at[:, a:b], wlo_ref.at[:, a:b] if precise else None)

    u_ref[...] = proj(C_U, C_Q)
    q_ref[...] = proj(C_Q, C_KVP)
    kvp_ref[...] = proj(C_KVP, C_KVW)
    kvw_ref[...] = proj(C_KVW, C_GATE)
    gt_ref[...] = proj(C_GATE, W1_COLS)


def _inproj(x, g, w1, tm, w1_lo=None):
    n = x.shape[0]
    widths = (SSM_DIM, ATTN_DIM, 4 * KV_DIM, 2 * KV_DIM, GATE_PAD)
    ws = [w1] if w1_lo is None else [w1, w1_lo]
    return pl.pallas_call(
        functools.partial(_inproj_body, precise=w1_lo is not None),
        grid=(n // tm,),
        in_specs=[
            pl.BlockSpec((tm, D_MODEL), lambda i: (i, 0)),
            _full((1, D_MODEL)),
        ] + [_full((D_MODEL, W1_COLS))] * len(ws),
        out_specs=[pl.BlockSpec((tm, w), lambda i: (i, 0)) for w in widths],
        out_shape=[jax.ShapeDtypeStruct((n, w), F32) for w in widths],
        compiler_params=_cparams(("parallel",)),
        name="inproj",
    )(x, g, *ws)


def _inproj_seq_body(x_ref, g_ref, w_ref, *refs, aliased):
    u_ref, q_ref, gt_ref, kvp_ref, kvw_ref = refs[2:] if aliased else refs
    hb = _rms(x_ref[...], g_ref[...]).astype(BF16)

    def proj(a, b):
        return _dot(hb, w_ref[:, a:b])

    u_ref[...] = proj(C_U, C_Q)
    q_ref[...] = proj(C_Q, C_KVP)
    gt_ref[...] = proj(C_GATE, W1_COLS)
    kvp_ref[...] = proj(C_KVP, C_KVW).T
    kvw_ref[...] = proj(C_KVW, C_GATE).T


def _inproj_seq(x, g, w1, bsz, seq, layer, depth, kv_bufs=None, tm=512):
    nt = seq // tm
    widths = (SSM_DIM, ATTN_DIM, GATE_PAD)
    row = lambda w: pl.BlockSpec((tm, w), lambda b, i: (b * nt + i, 0))
    kv_spec = lambda f: pl.BlockSpec((None, None, f, tm), lambda b, i: (layer, b, 0, i))
    aliased = kv_bufs is not None
    extra = list(kv_bufs) if aliased else []
    any_spec = pl.BlockSpec(memory_space=pl.ANY)
    return pl.pallas_call(
        functools.partial(_inproj_seq_body, aliased=aliased),
        grid=(bsz, nt),
        in_specs=[row(D_MODEL), _full((1, D_MODEL)), _full((D_MODEL, W1_COLS))] + [any_spec] * len(extra),
        out_specs=[row(w) for w in widths] + [kv_spec(4 * KV_DIM), kv_spec(2 * KV_DIM)],
        out_shape=[jax.ShapeDtypeStruct((bsz * seq, w), F32) for w in widths]
        + [jax.ShapeDtypeStruct((depth, bsz, 4 * KV_DIM, seq), F32),
           jax.ShapeDtypeStruct((depth, bsz, 2 * KV_DIM, seq), F32)],
        input_output_aliases={3: 3, 4: 4} if aliased else {},
        compiler_params=_cparams(("parallel", "parallel")),
        name="inproj_seq",
    )(x, g, w1, *extra)


S5_CHUNK = 128
S5_J = S5_CHUNK // SUBLANES


def _s5_tables_body(ar_r, ai_r, ldt_r, ar_c, ai_c, ldt_c, bre_ref, bim_ref,
                    a8re, a8im, apre, apim, a16re, a16im, bbre, bbim):
    dt = jnp.exp(ldt_r[...])
    mag = jnp.exp(ar_r[...] * dt)
    ang = ai_r[...] * dt
    are = mag * jnp.cos(ang)
    aim = mag * jnp.sin(ang)
    a8re[...] = jnp.broadcast_to(are, (SUBLANES, SSM_N))
    a8im[...] = jnp.broadcast_to(aim, (SUBLANES, SSM_N))
    pr, pi = are, aim
    for j in range(S5_J):
        apre[j * SUBLANES:(j + 1) * SUBLANES, :] = jnp.broadcast_to(pr, (SUBLANES, SSM_N))
        apim[j * SUBLANES:(j + 1) * SUBLANES, :] = jnp.broadcast_to(pi, (SUBLANES, SSM_N))
        if j + 1 < S5_J:
            pr, pi = pr * are - pi * aim, pr * aim + pi * are
    sre, sim = pr, pi
    qr, qi = sre, sim
    for s in range(SUBLANES):
        a16re[s:s + 1, :] = qr
        a16im[s:s + 1, :] = qi
        if s + 1 < SUBLANES:
            qr, qi = qr * sre - qi * sim, qr * sim + qi * sre
    dtc = jnp.exp(ldt_c[...])
    arc, aic = ar_c[...], ai_c[...]
    magc = jnp.exp(arc * dtc)
    angc = aic * dtc
    arec = magc * jnp.cos(angc)
    aimc = magc * jnp.sin(angc)
    den = arc * arc + aic * aic
    cre = ((arec - 1.0) * arc + aimc * aic) / den
    cim = (aimc * arc - (arec - 1.0) * aic) / den
    br, bi = bre_ref[...], bim_ref[...]
    bbre[...] = cre * br - cim * bi
    bbim[...] = cre * bi + cim * br


def _s5_tables(a_re, a_im, log_dt, b_re, b_im):
    ar_r = a_re.reshape(1, SSM_N)
    ai_r = a_im.reshape(1, SSM_N)
    ldt_r = jnp.repeat(log_dt, SSM_STATE).reshape(1, SSM_N)
    row = jax.ShapeDtypeStruct((SUBLANES, SSM_N), F32)
    tab = jax.ShapeDtypeStruct((S5_CHUNK, SSM_N), F32)
    col = jax.ShapeDtypeStruct((SSM_N, SSM_CH), F32)
    return pl.pallas_call(
        _s5_tables_body,
        out_shape=[row, row, tab, tab, row, row, col, col],
        name="s5_tables",
    )(ar_r, ai_r, ldt_r, ar_r.reshape(SSM_N, 1), ai_r.reshape(SSM_N, 1), ldt_r.reshape(SSM_N, 1),
      b_re.reshape(SSM_N, SSM_CH), b_im.reshape(SSM_N, SSM_CH))


def _block_diag_b(bb):
    t = bb.reshape(SSM_GROUPS, SSM_STATE, SSM_CH).transpose(0, 2, 1)
    eye = jnp.eye(SSM_GROUPS, dtype=bb.dtype)
    return (t[:, :, None, :] * eye[:, None, :, None]).reshape(SSM_DIM, SSM_N)


def _block_diag_c(c):
    t = c.transpose(0, 2, 1)
    eye = jnp.eye(SSM_GROUPS, dtype=c.dtype)
    return (t[:, :, None, :] * eye[:, None, :, None]).reshape(SSM_N, SSM_DIM)


def _s5_perm():
    p = np.zeros((S5_CHUNK, S5_CHUNK), np.float32)
    for j in range(S5_J):
        for s in range(SUBLANES):
            p[j * SUBLANES + s, S5_J * s + j] = 1.0
    return p


S5_CL = 3
S5_CU = SSM_DIM // S5_CL
S5_CS = SSM_N // S5_CL


def _s5_bu(u_hi, u_lo, bmat_ref, bmat_lo):
    res, ims = [], []
    for c in range(S5_CL):
        cols = slice(c * S5_CU, (c + 1) * S5_CU)
        b = _dot(u_hi[:, cols], bmat_ref[c])
        if bmat_lo is not None:
            b = b + (_dot(u_lo[:, cols], bmat_ref[c]) + _dot(u_hi[:, cols], bmat_lo[c]))
        res.append(b[:, :S5_CS])
        ims.append(b[:, S5_CS:])
    return jnp.concatenate(res, axis=1), jnp.concatenate(ims, axis=1)


def _s5_cy(hr, hi, cmat_ref, cmat_lo):
    ys = []
    for c in range(S5_CL):
        cols = slice(c * S5_CS, (c + 1) * S5_CS)
        h = jnp.concatenate([hr[:, cols], hi[:, cols]], axis=1)
        ys.append(_mm(h, cmat_ref.at[c], None if cmat_lo is None else cmat_lo.at[c]))
    return jnp.concatenate(ys, axis=1)


def _s5_epilogue(y, u, d_ref, wglu_ref, wglu_lo, bglu_ref):
    y = y + d_ref[...] * u
    g = _gelu_tanh(y)
    return g * _sigmoid(_mm(g, wglu_ref, wglu_lo) + bglu_ref[...])


def _s5_seq_body(*refs, tl, precise, carry_in):
    it = iter(refs)
    u_ref = next(it)
    h0r_ref, h0i_ref = (next(it), next(it)) if carry_in else (None, None)
    (perm_ref, permt_ref, bmat_ref, cmat_ref, a8re_ref, a8im_ref, apre_ref, apim_ref, a16re_ref, a16im_ref,
     d_ref, wglu_ref, bglu_ref) = (next(it) for _ in range(13))
    bmat_lo, cmat_lo, wglu_lo = (next(it), next(it), next(it)) if precise else (None, None, None)
    y_ref, hre_ref, him_ref, pre_r_ref, pre_i_ref, cre_scr, cim_scr = it
    li = pl.program_id(1)

    @pl.when(li == 0)
    def _():
        if carry_in:
            cre_scr[...] = h0r_ref[...]
            cim_scr[...] = h0i_ref[...]
        else:
            cre_scr[...] = jnp.zeros_like(cre_scr)
            cim_scr[...] = jnp.zeros_like(cim_scr)

    are = a8re_ref[...]
    aim = a8im_ref[...]
    row8 = lax.broadcasted_iota(jnp.int32, (SUBLANES, SSM_N), 0)
    nchunk = tl // S5_CHUNK

    for c in range(nchunk):
        if c == nchunk - 1:
            pre_r_ref[...] = cre_scr[...]
            pre_i_ref[...] = cim_scr[...]
        u = u_ref[c * S5_CHUNK:(c + 1) * S5_CHUNK, :]
        u_hi = u.astype(BF16)
        up = _dot(perm_ref[...], u_hi).astype(BF16)
        up_lo = (_dot(perm_ref[...], (u - u_hi.astype(F32)).astype(BF16)).astype(BF16) if precise else None)
        bu_re, bu_im = _s5_bu(up, up_lo, bmat_ref, bmat_lo)
        hr = [bu_re[0:SUBLANES, :]]
        hi = [bu_im[0:SUBLANES, :]]
        for j in range(1, S5_J):
            br = bu_re[j * SUBLANES:(j + 1) * SUBLANES, :]
            bi = bu_im[j * SUBLANES:(j + 1) * SUBLANES, :]
            hr.append(are * hr[-1] - aim * hi[-1] + br)
            hi.append(are * hi[-1] + aim * hr[-2] + bi)
        er, ei = hr[-1], hi[-1]
        for k, d in enumerate((1, 2, 4)):
            mr = jnp.broadcast_to(a16re_ref[d - 1:d, :], (SUBLANES, SSM_N))
            mi = jnp.broadcast_to(a16im_ref[d - 1:d, :], (SUBLANES, SSM_N))
            sr = jnp.where(row8 >= d, pltpu.roll(er, d, 0), 0.0)
            si = jnp.where(row8 >= d, pltpu.roll(ei, d, 0), 0.0)
            er, ei = er + mr * sr - mi * si, ei + mr * si + mi * sr
        h0r = jnp.broadcast_to(cre_scr[0:1, :], (SUBLANES, SSM_N))
        h0i = jnp.broadcast_to(cim_scr[0:1, :], (SUBLANES, SSM_N))
        p16r, p16i = a16re_ref[...], a16im_ref[...]
        er, ei = er + p16r * h0r - p16i * h0i, ei + p16r * h0i + p16i * h0r
        cinr = jnp.where(row8 == 0, h0r, pltpu.roll(er, 1, 0))
        cini = jnp.where(row8 == 0, h0i, pltpu.roll(ei, 1, 0))
        cre_scr[...] = jnp.broadcast_to(er[SUBLANES - 1:SUBLANES, :], (SUBLANES, SSM_N))
        cim_scr[...] = jnp.broadcast_to(ei[SUBLANES - 1:SUBLANES, :], (SUBLANES, SSM_N))
        fr, fi = [], []
        for j in range(S5_J):
            pr = apre_ref[j * SUBLANES:(j + 1) * SUBLANES, :]
            pi = apim_ref[j * SUBLANES:(j + 1) * SUBLANES, :]
            fr.append(hr[j] + pr * cinr - pi * cini)
            fi.append(hi[j] + pr * cini + pi * cinr)
        yp = _s5_cy(jnp.concatenate(fr, axis=0), jnp.concatenate(fi, axis=0), cmat_ref, cmat_lo)
        y_hi, y_mid, y_lo = _split3(yp)
        pt = permt_ref[...]
        y = _dot(pt, y_hi) + _dot(pt, y_mid) + _dot(pt, y_lo)
        y_ref[c * S5_CHUNK:(c + 1) * S5_CHUNK, :] = _s5_epilogue(y, u, d_ref, wglu_ref, wglu_lo, bglu_ref)

    hre_ref[...] = cre_scr[...]
    him_ref[...] = cim_scr[...]


def _s5_seq(u, s5w, bsz, seq, tl=512, precise=False, h0=None):
    nt = seq // tl
    consts = list(s5w["tabs"]) + [s5w["d"], s5w["wglu"], s5w["bglu"]]
    if precise:
        consts += [s5w["bmat_lo"], s5w["cmat_lo"], s5w["wglu_lo"]]
    state_spec = pl.BlockSpec((None, SUBLANES, SSM_N), lambda b, i: (b, 0, 0))
    state_shape = jax.ShapeDtypeStruct((bsz, SUBLANES, SSM_N), F32)
    carry = [] if h0 is None else list(h0)
    return pl.pallas_call(
        functools.partial(_s5_seq_body, tl=tl, precise=precise, carry_in=h0 is not None),
        grid=(bsz, nt),
        in_specs=[pl.BlockSpec((tl, SSM_DIM), lambda b, i: (b * nt + i, 0))]
        + [state_spec] * len(carry) + [_full(c.shape) for c in consts],
        out_specs=[pl.BlockSpec((tl, SSM_DIM), lambda b, i: (b * nt + i, 0))] + [state_spec] * 4,
        out_shape=[jax.ShapeDtypeStruct((bsz * seq, SSM_DIM), F32)] + [state_shape] * 4,
        scratch_shapes=[pltpu.VMEM((SUBLANES, SSM_N), F32), pltpu.VMEM((SUBLANES, SSM_N), F32)],
        compiler_params=_cparams(("parallel", "arbitrary")),
        name="s5_seq",
    )(u, *carry, *consts)


def _s5_step_body(u_ref, h0r_ref, h0i_ref, bmat_ref, cmat_ref, a8re_ref, a8im_ref,
                  d_ref, wglu_ref, bglu_ref, bmat_lo, cmat_lo, wglu_lo, y_ref, hre_ref, him_ref):
    u = u_ref[...]
    u_hi = u.astype(BF16)
    bu_re, bu_im = _s5_bu(u_hi, (u - u_hi.astype(F32)).astype(BF16), bmat_ref, bmat_lo)
    are = a8re_ref[0:1, :]
    aim = a8im_ref[0:1, :]
    h0r, h0i = h0r_ref[...], h0i_ref[...]
    hr = are * h0r - aim * h0i + bu_re
    hi = are * h0i + aim * h0r + bu_im
    hre_ref[...] = hr
    him_ref[...] = hi
    y = _s5_cy(hr, hi, cmat_ref, cmat_lo)
    y_ref[...] = _s5_epilogue(y, u, d_ref, wglu_ref, wglu_lo, bglu_ref)


def _s5_step(u, h0r, h0i, s5w):
    _, _, bmat, cmat, a8re, a8im = s5w["tabs"][:6]
    n = u.shape[0]
    return pl.pallas_call(
        _s5_step_body,
        out_shape=[
            jax.ShapeDtypeStruct((n, SSM_DIM), F32),
            jax.ShapeDtypeStruct((n, SSM_N), F32),
            jax.ShapeDtypeStruct((n, SSM_N), F32),
        ],
        compiler_params=pltpu.CompilerParams(vmem_limit_bytes=VMEM_LIMIT),
        name="s5_step",
    )(u, h0r, h0i, bmat, cmat, a8re, a8im, s5w["d"], s5w["wglu"], s5w["bglu"],
      s5w["bmat_lo"], s5w["cmat_lo"], s5w["wglu_lo"])


def _s5_prepare(a_re, a_im, log_dt, b_re, b_im, c_re, c_im, d_skip, w_glu, b_glu):
    a8re, a8im, apre, apim, a16re, a16im, bbre, bbim = _s5_tables(a_re, a_im, log_dt, b_re, b_im)
    bre, bim = _block_diag_b(bbre), _block_diag_b(bbim)
    cre, cim = _block_diag_c(c_re), _block_diag_c(c_im)
    us = lambda c: slice(c * S5_CU, (c + 1) * S5_CU)
    ss = lambda c: slice(c * S5_CS, (c + 1) * S5_CS)
    bmat, bmat_lo = _hilo(jnp.stack(
        [jnp.concatenate([bre[us(c), ss(c)], bim[us(c), ss(c)]], axis=1) for c in range(S5_CL)]))
    cmat, cmat_lo = _hilo(jnp.stack(
        [jnp.concatenate([cre[ss(c), us(c)], -cim[ss(c), us(c)]], axis=0) for c in range(S5_CL)]))
    wglu, wglu_lo = _hilo(w_glu)
    perm = _s5_perm()
    tabs = (jnp.asarray(perm, BF16), jnp.asarray(perm.T, BF16), bmat, cmat, a8re, a8im, apre, apim, a16re, a16im)
    return dict(tabs=tabs, d=d_skip.reshape(1, SSM_DIM), wglu=wglu, bglu=b_glu.reshape(1, SSM_DIM),
                bmat_lo=bmat_lo, cmat_lo=cmat_lo, wglu_lo=wglu_lo)


TAIL = 128
MTAIL = 16

MERGE_W = ("w2", "wbs", "wba", "wbc", "wo")


def _merge_weights(w2, w_br_ssm, w_br_attn, w_br_conv, w_out, conv_w):
    mw = {"cw": conv_w}
    for name, w in zip(MERGE_W, (w2, w_br_ssm, w_br_attn, w_br_conv, w_out)):
        mw[name], mw[name + "_lo"] = _hilo(w)
    return mw


def _merge_wlist(mw, precise):
    return [mw["cw"]] + [mw[n] for n in MERGE_W] + ([mw[n + "_lo"] for n in MERGE_W] if precise else [])


def _merge_core(x, g_ref, wrefs, precise, up0, up1, ys_ref, ya_ref):
    cw_ref, w2_ref, wbs_ref, wba_ref, wbc_ref, wo_ref = wrefs[:6]
    w2_lo, wbs_lo, wba_lo, wbc_lo, wo_lo = wrefs[6:] if precise else (None,) * 5
    zc = _mm(_rms(x, g_ref[...]), w2_ref, w2_lo)
    conv_b = zc[:, 0:CONV_DIM]
    uc = zc[:, CONV_DIM:2 * CONV_DIM] * zc[:, 2 * CONV_DIM:3 * CONV_DIM]
    yc = conv_b * (cw_ref[0:1, :] * up0(uc) + cw_ref[1:2, :] * up1(uc) + cw_ref[2:3, :] * uc)
    g0 = 3 * CONV_DIM
    g_ssm = _sigmoid(zc[:, g0:g0 + D_MODEL])
    g_attn = _sigmoid(zc[:, g0 + D_MODEL:g0 + 2 * D_MODEL])
    g_conv = _sigmoid(zc[:, g0 + 2 * D_MODEL:g0 + 3 * D_MODEL])
    merged = (g_ssm * _mm(ys_ref[...], wbs_ref, wbs_lo)
              + g_attn * _mm(ya_ref[...], wba_ref, wba_lo)
              + g_conv * _mm(yc, wbc_ref, wbc_lo))
    return x + _mm(merged, wo_ref, wo_lo), uc


def _merge_seq_body(*refs, tm, precise, carry_in):
    it = iter(refs)
    x_ref, g_ref, ys_ref, ya_ref = (next(it) for _ in range(4))
    cin_ref = next(it) if carry_in else None
    wrefs = [next(it) for _ in range(11 if precise else 6)]
    o_ref, cl_ref, pre_ref, stage = it
    ti = pl.program_id(1)

    @pl.when(ti == 0)
    def _():
        stage[0:SUBLANES, :] = cin_ref[...] if carry_in else jnp.zeros((SUBLANES, CONV_DIM), F32)

    def up0(uc):
        stage[SUBLANES:SUBLANES + tm, :] = uc
        return stage[SUBLANES - 2:SUBLANES - 2 + tm, :]

    def up1(uc):
        return stage[SUBLANES - 1:SUBLANES - 1 + tm, :]

    out, uc = _merge_core(x_ref[...], g_ref, wrefs, precise, up0, up1, ys_ref, ya_ref)
    o_ref[...] = out
    last = uc[tm - SUBLANES:tm, :]
    stage[0:SUBLANES, :] = last
    cl_ref[...] = last
    pre_ref[...] = uc[max(tm - MTAIL, SUBLANES) - SUBLANES:max(tm - MTAIL, SUBLANES), :]


def _merge_seq(x, g, ys, ya, mw, bsz, seq, tm=256, precise=False, cin=None):
    nt = seq // tm
    ws = _merge_wlist(mw, precise)
    row = lambda w: pl.BlockSpec((tm, w), lambda b, i: (b * nt + i, 0))
    cspec = pl.BlockSpec((None, SUBLANES, CONV_DIM), lambda b, i: (b, 0, 0))
    carry = [] if cin is None else [cin]
    return pl.pallas_call(
        functools.partial(_merge_seq_body, tm=tm, precise=precise, carry_in=cin is not None),
        grid=(bsz, nt),
        in_specs=[row(D_MODEL), _full(g.shape), row(SSM_DIM), row(ATTN_DIM)] + [cspec] * len(carry)
        + [_full1(w.shape) for w in ws],
        out_specs=[row(D_MODEL), cspec, cspec],
        out_shape=[jax.ShapeDtypeStruct((bsz * seq, D_MODEL), F32)]
        + [jax.ShapeDtypeStruct((bsz, SUBLANES, CONV_DIM), F32)] * 2,
        scratch_shapes=[pltpu.VMEM((tm + SUBLANES, CONV_DIM), F32)],
        compiler_params=_cparams(("parallel", "arbitrary")),
        name="merge_seq",
    )(x, g, ys, ya, *carry, *ws)


def _merge_step_body(x_ref, g_ref, ys_ref, ya_ref, p0_ref, p1_ref, *refs):
    wrefs, (o_ref, uc_ref) = refs[:11], refs[11:]
    out, uc = _merge_core(x_ref[...], g_ref, wrefs, True, lambda _: p0_ref[...], lambda _: p1_ref[...],
                          ys_ref, ya_ref)
    o_ref[...] = out
    uc_ref[...] = uc


def _merge_step(x, g, ys, ya, prev0, prev1, mw):
    n = x.shape[0]
    return pl.pallas_call(
        _merge_step_body,
        out_shape=[jax.ShapeDtypeStruct((n, D_MODEL), F32), jax.ShapeDtypeStruct((n, CONV_DIM), F32)],
        compiler_params=pltpu.CompilerParams(vmem_limit_bytes=VMEM_LIMIT),
        name="merge_step",
    )(x, g, ys, ya, prev0, prev1, *_merge_wlist(mw, True))


ROUTE_E0 = SUBLANES


def _route_math(lg, le):
    gmax = functools.reduce(jnp.maximum, lg)
    gsum = functools.reduce(lambda a, b: a + b, [jnp.exp(v - gmax) for v in lg])
    gw = 1.0 / gsum
    gsel = jnp.full_like(gmax, N_GROUPS - 1).astype(jnp.int32)
    for k in range(N_GROUPS - 2, -1, -1):
        gsel = jnp.where(lg[k] == gmax, k, gsel)
    ls = []
    for j in range(EPG):
        v = le[j]
        for k in range(1, N_GROUPS):
            v = jnp.where(gsel == k, le[k * EPG + j], v)
        ls.append(v)
    emax = functools.reduce(jnp.maximum, ls)
    ex = [jnp.exp(v - emax) for v in ls]
    esum = functools.reduce(lambda a, b: a + b, ex)
    pe = [v / esum for v in ex]
    v1 = functools.reduce(jnp.maximum, pe)
    i1 = jnp.full_like(gsel, EPG - 1)
    for j in range(EPG - 2, -1, -1):
        i1 = jnp.where(pe[j] == v1, j, i1)
    pe2 = [jnp.where(i1 == j, -1.0, pe[j]) for j in range(EPG)]
    v2 = functools.reduce(jnp.maximum, pe2)
    i2 = jnp.full_like(gsel, EPG - 1)
    for j in range(EPG - 2, -1, -1):
        i2 = jnp.where(pe2[j] == v2, j, i2)
    tot = v1 + v2
    w1 = v1 / tot * gw
    w2 = v2 / tot * gw
    return gsel * EPG + i1, gsel * EPG + i2, w1, w2


def _moe_route(h, wr_ref, wrt_ref, br_ref, brt_ref):
    tm = h.shape[0]
    h_hi = h.astype(BF16)
    h_lo = (h - h_hi.astype(F32)).astype(BF16)
    if tm % LANES:
        logits = _dot(h_hi, wr_ref[0]) + (_dot(h_lo, wr_ref[0]) + _dot(h_hi, wr_ref[1])) + br_ref[...]
        lg = [logits[:, k:k + 1] for k in range(N_GROUPS)]
        le = [logits[:, ROUTE_E0 + e:ROUTE_E0 + e + 1] for e in range(N_EXPERTS)]
        e1, e2, w1, w2 = _route_math(lg, le)
        lane = lax.broadcasted_iota(jnp.int32, (tm, LANES), 1)
        return jnp.where(lane == e1, w1, 0.0) + jnp.where(lane == e2, w2, 0.0)
    lt = _dot_nt(wrt_ref[0], h_hi) + (_dot_nt(wrt_ref[0], h_lo) + _dot_nt(wrt_ref[1], h_hi)) + brt_ref[...]
    lg = [lt[k:k + 1, :] for k in range(N_GROUPS)]
    le = [lt[ROUTE_E0 + e:ROUTE_E0 + e + 1, :] for e in range(N_EXPERTS)]
    e1, e2, w1, w2 = _route_math(lg, le)
    rows = [jnp.where(e1 == e, w1, 0.0) + jnp.where(e2 == e, w2, 0.0) for e in range(N_EXPERTS)]
    comb_t = jnp.concatenate(rows + [jnp.zeros((LANES - N_EXPERTS, tm), F32)], axis=0)
    return comb_t.T


def _moe_body(x_ref, g_ref, wr_ref, wrt_ref, br_ref, brt_ref, wgu_ref, wd_ref, gf_ref, o_ref,
              hb_scr, comb_scr, acc_scr, *, final):
    e = pl.program_id(1)

    @pl.when(e == 0)
    def _():
        h = _rms(x_ref[...], g_ref[...])
        hb_scr[...] = h.astype(BF16)
        comb_scr[...] = _moe_route(h, wr_ref, wrt_ref, br_ref, brt_ref)
        acc_scr[...] = jnp.zeros_like(acc_scr)

    hgu = _dot(hb_scr[...], wgu_ref[0])
    hg = hgu[:, :D_FF]
    lane = lax.broadcasted_iota(jnp.int32, comb_scr.shape, 1)
    ce = jnp.sum(jnp.where(lane == e, comb_scr[...], 0.0), axis=1, keepdims=True)
    act = hg * _sigmoid(hg) * hgu[:, D_FF:] * ce
    acc_scr[...] += _dot(act.astype(BF16), wd_ref[0])

    @pl.when(e == N_EXPERTS - 1)
    def _():
        y = x_ref[...] + acc_scr[...]
        o_ref[...] = _rms(y, gf_ref[...]) if final else y


def _moe(x, g, mo, gf, tm, final=False):
    n = x.shape[0]
    wr, wrt, br, brt, wgu, wd = (mo[k] for k in ("wr", "wrt", "br", "brt", "wgu", "wd"))
    return pl.pallas_call(
        functools.partial(_moe_body, final=final),
        grid=(n // tm, N_EXPERTS),
        in_specs=[
            pl.BlockSpec((tm, D_MODEL), lambda i, e: (i, 0)),
            _full(g.shape), _full(wr.shape), _full(wrt.shape), _full(br.shape), _full(brt.shape),
            pl.BlockSpec((1, D_MODEL, 2 * D_FF), lambda i, e: (e, 0, 0)),
            pl.BlockSpec((1, D_FF, D_MODEL), lambda i, e: (e, 0, 0)),
            _full(gf.shape),
        ],
        out_specs=pl.BlockSpec((tm, D_MODEL), lambda i, e: (i, 0)),
        out_shape=jax.ShapeDtypeStruct((n, D_MODEL), F32),
        scratch_shapes=[pltpu.VMEM((tm, D_MODEL), BF16), pltpu.VMEM((tm, LANES), F32),
                        pltpu.VMEM((tm, D_MODEL), F32)],
        compiler_params=_cparams(("parallel", "arbitrary")),
        name="moe",
    )(x, g, wr, wrt, br, brt, wgu, wd, gf)


def _moe_prepare(w_rg, b_rg, w_re, b_re, w_gate, w_up, w_down):
    wr = jnp.zeros((D_MODEL, LANES), F32)
    wr = wr.at[:, :N_GROUPS].set(w_rg).at[:, ROUTE_E0:ROUTE_E0 + N_EXPERTS].set(w_re)
    br = jnp.zeros((1, LANES), F32)
    br = br.at[0, :N_GROUPS].set(b_rg).at[0, ROUTE_E0:ROUTE_E0 + N_EXPERTS].set(b_re)
    wr2 = jnp.stack(_hilo(wr))
    return dict(wr=wr2, wrt=wr2.transpose(0, 2, 1), br=br, brt=br.reshape(LANES, 1),
                wgu=jnp.concatenate([w_gate, w_up], axis=-1).astype(BF16),
                wd=w_down.astype(BF16))


CMP_SUB = 256
CMP_W = 4 * HEAD_DIM
PAGES_PER_STEP = 64


def _cmp_perm():
    p = np.zeros((CMP_SUB, CMP_SUB), np.float32)
    for i in range(CMP_SUB):
        p[i, CMP_BLOCK * (i % SUBLANES) + i // SUBLANES] = 1.0
    return p


def _cmp_prepare(pe, w1, w2):
    eye4 = jnp.eye(4, dtype=F32)
    sel = jnp.array([0, 0, 1, 1])
    w1r = w1.reshape(2, CMP_BLOCK, HEAD_DIM, CMP_HIDDEN)[sel]
    w1bd = (w1r.transpose(1, 0, 2, 3)[:, :, :, None, :] * eye4[None, :, None, :, None])
    w1bd = w1bd.reshape(CMP_BLOCK, CMP_W, 4 * CMP_HIDDEN).astype(BF16)
    w2r = w2[sel]
    w2bd = (w2r[:, :, None, :] * eye4[:, None, :, None]).reshape(4 * CMP_HIDDEN, CMP_W).astype(BF16)
    pe4 = pe[sel].transpose(1, 0, 2).reshape(CMP_BLOCK, CMP_W)
    pe_exp = jnp.repeat(pe4, SUBLANES, axis=0)
    return jnp.asarray(_cmp_perm(), BF16), pe_exp, w1bd, w2bd


def _compress_rows(get_sub, nsub, perm_ref, pe_ref, w1_ref, w2_ref, stage, transposed=False):
    mm = _dot_nt if transposed else _dot
    for t in range(nsub):
        xp = mm(perm_ref[...], get_sub(t).astype(BF16))
        stage[t] = xp + pe_ref[...]
    acc = jnp.zeros((nsub * SUBLANES, 4 * CMP_HIDDEN), F32)
    for r in range(CMP_BLOCK):
        a = stage[:, r * SUBLANES:(r + 1) * SUBLANES, :].reshape(nsub * SUBLANES, CMP_W)
        acc = acc + _dot(a.astype(BF16), w1_ref[r])
    return _dot(_gelu_tanh(acc).astype(BF16), w2_ref[...])


def _compress_seq_body(x_ref, perm_ref, pe_ref, w1_ref, w2_ref, o_ref, stage, *, nsub):
    o_ref[...] = _compress_rows(lambda t: x_ref[:, t * CMP_SUB:(t + 1) * CMP_SUB], nsub,
                                perm_ref, pe_ref, w1_ref, w2_ref, stage, transposed=True)


def _compress_seq(kvp_t, layer, cw, bsz, seq):
    nsub = seq // CMP_SUB
    nblk = seq // CMP_BLOCK
    return pl.pallas_call(
        functools.partial(_compress_seq_body, nsub=nsub),
        grid=(bsz,),
        in_specs=[pl.BlockSpec((None, None, CMP_W, seq), lambda b: (layer, b, 0, 0))]
        + [_full(c.shape) for c in cw],
        out_specs=pl.BlockSpec((None, nblk, CMP_W), lambda b: (b, 0, 0)),
        out_shape=jax.ShapeDtypeStruct((bsz, nblk, CMP_W), F32),
        scratch_shapes=[pltpu.VMEM((nsub, CMP_SUB, CMP_W), F32)],
        compiler_params=_cparams(("parallel",)),
        name="compress_seq",
    )(kvp_t, *cw)


def _compress_pages_body(pt_ref, *refs):
    pages = refs[:PAGES_PER_STEP]
    perm_ref, pe_ref, w1_ref, w2_ref, o_ref, stage = refs[PAGES_PER_STEP:]
    per = CMP_SUB // PAGE

    def get_sub(t):
        return jnp.concatenate([pages[per * t + k][...] for k in range(per)], axis=1)

    o_ref[...] = _compress_rows(get_sub, PAGES_PER_STEP // per, perm_ref, pe_ref, w1_ref, w2_ref, stage,
                                transposed=True)


def _compress_pages(cache_t, page_table, layer, cw):
    bsz, n_pages = page_table.shape
    steps = n_pages // PAGES_PER_STEP
    nsub = PAGES_PER_STEP * PAGE // CMP_SUB
    blk_per_step = PAGES_PER_STEP * PAGE // CMP_BLOCK

    def page_spec(k):
        return pl.BlockSpec((None, None, CMP_W, PAGE),
                            lambda b, i, pt: (layer, pt[b * n_pages + i * PAGES_PER_STEP + k], 0, 0))

    grid_spec = pltpu.PrefetchScalarGridSpec(
        num_scalar_prefetch=1,
        grid=(bsz, steps),
        in_specs=[page_spec(k) for k in range(PAGES_PER_STEP)]
        + [pl.BlockSpec(c.shape, lambda b, i, pt, nd=c.ndim: (0,) * nd) for c in cw],
        out_specs=pl.BlockSpec((None, blk_per_step, CMP_W), lambda b, i, pt: (b, i, 0)),
        scratch_shapes=[pltpu.VMEM((nsub, CMP_SUB, CMP_W), F32)],
    )
    return pl.pallas_call(
        _compress_pages_body,
        grid_spec=grid_spec,
        out_shape=jax.ShapeDtypeStruct((bsz, n_pages * PAGE // CMP_BLOCK, CMP_W), F32),
        compiler_params=_cparams(("parallel", "arbitrary")),
        name="compress_pages",
    )(page_table.reshape(-1), *([cache_t] * PAGES_PER_STEP), *cw)


TQ = 128
SLC_CHUNK = 512
QL = GRP * TQ
N_SELBLK_SEQ = 32
AUG_MASK = HEAD_DIM
AUG_POS = HEAD_DIM + 32
POS_SPLIT = 128


def _alibi_slopes():
    return [2.0 ** (-8.0 * (h + 1) / N_HEADS) for h in range(N_HEADS)]


def _nsa_tables(seq):
    pos = np.arange(seq)
    tbl = np.zeros((seq, HEAD_DIM), np.float32)
    tbl[pos, pos // SEL_BLOCK] = 1.0
    tbl[:, 32] = (pos // POS_SPLIT) * POS_SPLIT
    tbl[:, 33] = pos % POS_SPLIT
    ncmp = seq // CMP_BLOCK
    order = np.concatenate([np.arange(0, ncmp, 2), np.arange(1, ncmp, 2)])
    cend = (order + 1) * CMP_BLOCK - 1
    ctbl = np.zeros((ncmp, HEAD_DIM), np.float32)
    ctbl[:, 32] = (cend // POS_SPLIT) * POS_SPLIT
    ctbl[:, 33] = cend % POS_SPLIT
    pm = np.zeros((ncmp, ncmp), np.float32)
    pm[np.arange(ncmp), order] = 1.0
    slope = np.zeros((KV_HEADS, 16, QL), np.float32)
    sl = _alibi_slopes()
    for g in range(KV_HEADS):
        for m in range(GRP):
            slope[g, 0:2, m * TQ:(m + 1) * TQ] = sl[g * GRP + m]
    return (jnp.asarray(tbl), jnp.asarray(ctbl), jnp.asarray(cend.reshape(ncmp, 1).astype(np.int32)),
            jnp.asarray(pm, BF16), jnp.asarray(slope, BF16))


def _flash_chunk(kaug, vt, qa, valid, state):
    m, l, acc = state
    s = _dot(kaug, qa)
    if valid is not None:
        s = jnp.where(valid, s, NEG)
    mn = jnp.maximum(m, jnp.max(s, axis=0, keepdims=True))
    alpha = jnp.exp(m - mn)
    p = jnp.exp(s - mn)
    l = alpha * l + jnp.sum(p, axis=0, keepdims=True)
    acc = alpha * acc + _dot(vt, p.astype(BF16))
    return mn, l, acc


def _flash_init():
    return (jnp.full((1, QL), M_INIT, F32), jnp.zeros((1, QL), F32), jnp.zeros((HEAD_DIM, QL), F32))


def _nsa_seq_body(q_ref, kvp_ref, kvw_ref, gt_ref, kc_ref, tbl_ref, ctbl_ref, cend_ref, pm_ref,
                  slope_ref, y_ref, kslc, kwin, vslc, vwin, qaug):
    i = pl.program_id(1)
    l0 = i * TQ

    @pl.when(i == 0)
    def _():
        kslc[...] = jnp.zeros_like(kslc)
        kwin[...] = jnp.zeros_like(kwin)
        vslc[...] = jnp.zeros_like(vslc)
        vwin[...] = jnp.zeros_like(vwin)

    q = q_ref[...]
    tbl = tbl_ref[...]
    lane64 = lax.broadcasted_iota(jnp.int32, (TQ, HEAD_DIM), 1)
    tblw = jnp.where(lane64 < 32, 0.0, tbl)
    rows = pl.ds(pl.multiple_of(l0, TQ), TQ)
    ks_all = kvp_ref[0:KV_DIM, :].T
    kw_all = kvw_ref[0:KV_DIM, :].T
    for g in range(KV_HEADS):
        ks = ks_all[:, g * HEAD_DIM:(g + 1) * HEAD_DIM]
        kslc[g, rows, :] = jnp.concatenate([ks, tbl], axis=1).astype(BF16)
        kw = kw_all[:, g * HEAD_DIM:(g + 1) * HEAD_DIM]
        kwin[g, rows, :] = jnp.concatenate([kw, tblw], axis=1).astype(BF16)
    vslc[i] = kvp_ref[KV_DIM:2 * KV_DIM, :].astype(BF16)
    vwin[i] = kvw_ref[KV_DIM:2 * KV_DIM, :].astype(BF16)

    scale = HEAD_DIM ** -0.5
    qt = [(q[:, j * LANES:(j + 1) * LANES] * scale).T for j in range(ATTN_DIM // LANES)]
    for g in range(KV_HEADS):
        heads = []
        for m in range(GRP):
            h = g * GRP + m
            heads.append(qt[h // 2][(h % 2) * HEAD_DIM:(h % 2 + 1) * HEAD_DIM, :])
        qaug[g, 0:HEAD_DIM, :] = jnp.concatenate(heads, axis=1).astype(BF16)
        qaug[g, AUG_MASK:AUG_POS, :] = jnp.zeros((AUG_POS - AUG_MASK, QL), BF16)
        qaug[g, AUG_POS:AUG_POS + 16, :] = slope_ref[g]
        qaug[g, AUG_POS + 16:, :] = jnp.zeros((2 * HEAD_DIM - AUG_POS - 16, QL), BF16)

    ncmp = kc_ref.shape[0]
    nsel = ncmp // 2
    kcp = _dot(pm_ref[...], kc_ref[...].astype(BF16))
    vct = jnp.concatenate([kcp[:, 2 * KV_DIM - KV_DIM:2 * KV_DIM],
                           jnp.zeros((LANES - ncmp, KV_DIM), F32)], axis=0).T
    lpos = l0 + lax.broadcasted_iota(jnp.int32, (1, QL), 1) % TQ
    valid_c = cend_ref[...] <= lpos
    lq = l0 + lax.broadcasted_iota(jnp.int32, (nsel, TQ), 1)
    blk = lax.broadcasted_iota(jnp.int32, (nsel, TQ), 0)
    cur = lq // SEL_BLOCK
    forced = (blk == 0) | (blk == cur) | (blk == cur - 1)
    causal_blk = blk * SEL_BLOCK <= lq
    o_cmp = []
    for g in range(KV_HEADS):
        kca = jnp.concatenate([kcp[:, g * HEAD_DIM:(g + 1) * HEAD_DIM], ctbl_ref[...]], axis=1).astype(BF16)
        s = jnp.where(valid_c, _dot(kca, qaug[g]), NEG)
        mx = jnp.max(s, axis=0, keepdims=True)
        e = jnp.where(valid_c, jnp.exp(s - mx), 0.0)
        p = e * (1.0 / jnp.maximum(jnp.sum(e, axis=0, keepdims=True), 1.0))
        vt_g = vct[g * HEAD_DIM:(g + 1) * HEAD_DIM, 0:ncmp].astype(BF16)
        o_cmp.append(_dot(vt_g, p.astype(BF16)))
        psum = p[:, 0:TQ]
        for m in range(1, GRP):
            psum = psum + p[:, m * TQ:(m + 1) * TQ]
        imp = psum[0:nsel, :] + psum[nsel:, :]
        score = jnp.where(causal_blk, imp + jnp.where(forced, FORCE_BONUS, 0.0), NEG)
        sel = jnp.zeros((nsel, TQ), jnp.bool_)
        for _ in range(N_SELECT):
            best = jnp.max(score, axis=0, keepdims=True)
            idx = jnp.min(jnp.where(score == best, blk, nsel), axis=0, keepdims=True)
            hit = (blk == idx) & (best > 0.5 * NEG)
            sel = sel | hit
            score = jnp.where(hit, 2.0 * NEG, score)
        mb = jnp.where(sel, 0.0, NEG)
        qaug[g, AUG_MASK:AUG_POS, :] = jnp.concatenate([mb] * GRP, axis=1).astype(BF16)

    qpos = l0 + lax.broadcasted_iota(jnp.int32, (1, QL), 1) % TQ
    per = SLC_CHUNK // TQ

    def slc_chunk(cb, st, valid):
        k0 = pl.multiple_of(cb * SLC_CHUNK, SLC_CHUNK)
        out = []
        for g in range(KV_HEADS):
            vt = jnp.concatenate([vslc[cb * per + k, g * HEAD_DIM:(g + 1) * HEAD_DIM, :] for k in range(per)], axis=1)
            out.append(_flash_chunk(kslc[g, pl.ds(k0, SLC_CHUNK), :], vt, qaug[g], valid, st[g]))
        return tuple(out)

    cbd = i // per
    st = lax.fori_loop(0, cbd, lambda cb, s: slc_chunk(cb, s, None), tuple(_flash_init() for _ in range(KV_HEADS)))
    kpos = cbd * SLC_CHUNK + lax.broadcasted_iota(jnp.int32, (SLC_CHUNK, 1), 0)
    st = slc_chunk(cbd, st, kpos <= qpos)
    o_slc = [acc * (1.0 / l) for (m, l, acc) in st]

    nwc = WINDOW // TQ + 1
    cw0 = jnp.maximum(i - (nwc - 1), 0)
    kw0 = pl.multiple_of(cw0 * TQ, TQ)
    dist = qpos - (kw0 + lax.broadcasted_iota(jnp.int32, (nwc * TQ, 1), 0))
    vis = (dist >= 0) & (dist < WINDOW)
    o_win = []
    for g in range(KV_HEADS):
        vt = jnp.concatenate([vwin[cw0 + k, g * HEAD_DIM:(g + 1) * HEAD_DIM, :] for k in range(nwc)], axis=1)
        m, l, acc = _flash_chunk(kwin[g, pl.ds(kw0, nwc * TQ), :], vt, qaug[g], vis, _flash_init())
        o_win.append(acc * (1.0 / l))

    sg = _sigmoid(gt_ref[...].T[0:3 * N_HEADS, :])
    outs = []
    for g in range(KV_HEADS):
        def gate_row(br):
            return jnp.concatenate([sg[br * N_HEADS + g * GRP + m:br * N_HEADS + g * GRP + m + 1, :]
                                    for m in range(GRP)], axis=1)
        outs.append(gate_row(0) * o_cmp[g] + gate_row(1) * o_slc[g] + gate_row(2) * o_win[g])
    for j in range(ATTN_DIM // LANES):
        rows = []
        for h in (2 * j, 2 * j + 1):
            g, m = divmod(h, GRP)
            rows.append(outs[g][:, m * TQ:(m + 1) * TQ])
        y_ref[:, j * LANES:(j + 1) * LANES] = jnp.concatenate(rows, axis=0).T


def _nsa_seq(q, kvp_t, kvw_t, layer, gt, kc, bsz, seq):
    assert seq % SLC_CHUNK == 0 and seq >= WINDOW + TQ
    assert seq // SEL_BLOCK == AUG_POS - AUG_MASK
    nt = seq // TQ
    tbl, ctbl, cend, pm, slope = _nsa_tables(seq)
    ncmp = seq // CMP_BLOCK
    row = lambda w: pl.BlockSpec((TQ, w), lambda b, i: (b * nt + i, 0))
    return pl.pallas_call(
        _nsa_seq_body,
        grid=(bsz, nt),
        in_specs=[row(ATTN_DIM),
                  pl.BlockSpec((None, None, 2 * KV_DIM, TQ), lambda b, i: (layer, b, 1, i)),
                  pl.BlockSpec((None, None, 2 * KV_DIM, TQ), lambda b, i: (layer, b, 0, i)),
                  row(GATE_PAD),
                  pl.BlockSpec((None, ncmp, CMP_W), lambda b, i: (b, 0, 0)),
                  pl.BlockSpec((TQ, HEAD_DIM), lambda b, i: (i, 0)),
                  _full(ctbl.shape), _full(cend.shape), _full(pm.shape), _full(slope.shape)],
        out_specs=row(ATTN_DIM),
        out_shape=jax.ShapeDtypeStruct((bsz * seq, ATTN_DIM), F32),
        scratch_shapes=[pltpu.VMEM((KV_HEADS, seq, 2 * HEAD_DIM), BF16),
                        pltpu.VMEM((KV_HEADS, seq, 2 * HEAD_DIM), BF16),
                        pltpu.VMEM((nt, KV_DIM, TQ), BF16),
                        pltpu.VMEM((nt, KV_DIM, TQ), BF16),
                        pltpu.VMEM((KV_HEADS, 2 * HEAD_DIM, QL), BF16)],
        compiler_params=_cparams(("parallel", "arbitrary")),
        name="nsa_seq",
    )(q, kvp_t, kvw_t, gt, kc, tbl, ctbl, cend, pm, slope)


N_PICK = N_SELECT - 1


def _q_rows(q, g, scale):
    rows = [q[:, (g * GRP + m) * HEAD_DIM:(g * GRP + m + 1) * HEAD_DIM] for m in range(GRP)]
    return jnp.concatenate(rows + [jnp.zeros((SUBLANES - GRP, HEAD_DIM), F32)], axis=0) * scale


def _slope_col(g):
    sl = _alibi_slopes()
    row = lax.broadcasted_iota(jnp.int32, (SUBLANES, 1), 0)
    col = jnp.zeros((SUBLANES, 1), F32)
    for m in range(GRP):
        col = jnp.where(row == m, sl[g * GRP + m], col)
    return col


def _nsa_step_cmp_body(q_ref, kc_ref, pair_ref, o_ref, idx_ref, *, past_len, nseq):
    ncmp = kc_ref.shape[1]
    nsel = ncmp // 2
    cur = past_len // SEL_BLOCK
    scale = HEAD_DIM ** -0.5
    cend = (lax.broadcasted_iota(jnp.int32, (1, ncmp), 1) + 1) * CMP_BLOCK - 1
    dist = (past_len - cend).astype(F32)
    lane = lax.broadcasted_iota(jnp.int32, (1, nsel), 1)
    forced = (lane == 0) | (lane == cur) | (lane == cur - 1)
    lane_o = lax.broadcasted_iota(jnp.int32, (1, LANES), 1)
    pair = pair_ref[...]
    rows = []
    for bl in range(nseq):
        q = q_ref[bl]
        kc = kc_ref[bl]
        for g in range(KV_HEADS):
            q8 = _q_rows(q, g, scale).astype(BF16)
            s = _dot_nt(q8, kc[:, g * HEAD_DIM:(g + 1) * HEAD_DIM].astype(BF16)) - _slope_col(g) * dist
            mx = jnp.max(s, axis=1, keepdims=True)
            e = jnp.exp(s - mx)
            p = e / jnp.maximum(jnp.sum(e, axis=1, keepdims=True), 1.0)
            vg = kc[:, KV_DIM + g * HEAD_DIM:KV_DIM + (g + 1) * HEAD_DIM]
            o_ref[bl, g] = _dot(p.astype(BF16), vg.astype(BF16))
            psum = jnp.sum(p[0:GRP, :], axis=0, keepdims=True)
            p_hi, p_mid, p_lo = _split3(jnp.broadcast_to(psum, (SUBLANES, ncmp)))
            imp = (_dot(p_hi, pair) + _dot(p_mid, pair) + _dot(p_lo, pair))[0:1, :]
            rows.append(imp + jnp.where(forced, FORCE_BONUS, 0.0))
    score = jnp.concatenate(rows, axis=0)
    picks = []
    for t in range(N_PICK):
        best = jnp.max(score, axis=1, keepdims=True)
        idx = jnp.min(jnp.where(score == best, lane, nsel), axis=1, keepdims=True)
        score = jnp.where(lane == idx, NEG, score)
        picks.append(idx)
    for bl in range(nseq):
        idx_out = jnp.zeros((1, LANES), jnp.int32)
        for g in range(KV_HEADS):
            r = bl * KV_HEADS + g
            for t in range(N_PICK):
                idx_out = jnp.where(lane_o == g * N_SELECT + t, picks[t][r:r + 1, :], idx_out)
        idx_ref[bl] = idx_out


def _nsa_step_attn_body(sel_ref, pt_ref, *refs, past_len):
    nb = KV_HEADS * N_PICK
    blocks = refs[:nb]
    win_ref, q_ref, gt_ref, kvp_ref, kvw_ref, ocmp_ref, y_ref = refs[nb:]
    b = pl.program_id(0)
    q = q_ref[...]
    kvp = kvp_ref[...]
    kvw = kvw_ref[...]
    sg = _sigmoid(gt_ref[...])
    w = win_ref[...]
    nwin = w.shape[1]
    scale = HEAD_DIM ** -0.5
    pl_ = lax.broadcasted_iota(jnp.int32, (1, PAGE), 1)
    wl = lax.broadcasted_iota(jnp.int32, (1, nwin), 1)
    per_page = PAGE // SEL_BLOCK
    for g in range(KV_HEADS):
        q8 = _q_rows(q, g, scale)
        q8b = q8.astype(BF16)
        sc = _slope_col(g)
        hs = slice(g * HEAD_DIM, (g + 1) * HEAD_DIM)
        vs_ = slice(KV_DIM + g * HEAD_DIM, KV_DIM + (g + 1) * HEAD_DIM)
        k_new = kvp[:, 2 * KV_DIM + g * HEAD_DIM:2 * KV_DIM + (g + 1) * HEAD_DIM]
        v_new = kvp[:, 3 * KV_DIM + g * HEAD_DIM:3 * KV_DIM + (g + 1) * HEAD_DIM]
        s_cur = jnp.sum(q8 * k_new, axis=1, keepdims=True)
        ss, vs = [], []
        for t in range(N_PICK):
            blk = blocks[g * N_PICK + t][...]
            s_idx = sel_ref[b * KV_HEADS * N_SELECT + g * N_SELECT + t]
            kpos = (s_idx // per_page) * PAGE + pl_
            inblk = pl_ // SEL_BLOCK == s_idx % per_page
            st = _dot(q8b, blk[hs, :].astype(BF16)) - sc * (past_len - kpos).astype(F32)
            ss.append(jnp.where(inblk, st, NEG))
            vs.append(blk[vs_, :].astype(BF16))
        mx = functools.reduce(jnp.maximum, [jnp.max(s, axis=1, keepdims=True) for s in ss] + [s_cur])
        es = [jnp.exp(s - mx) for s in ss]
        e_cur = jnp.exp(s_cur - mx)
        den = functools.reduce(lambda a, c: a + c, [jnp.sum(e, axis=1, keepdims=True) for e in es] + [e_cur])
        acc = e_cur * v_new
        for e, v in zip(es, vs):
            acc = acc + _dot_nt(e.astype(BF16), v)
        o_slc = acc / den
        kw_new = kvw[:, g * HEAD_DIM:(g + 1) * HEAD_DIM]
        vw_new = kvw[:, KV_DIM + g * HEAD_DIM:KV_DIM + (g + 1) * HEAD_DIM]
        dw = (nwin - wl).astype(F32)
        s_w = jnp.where(nwin - wl < WINDOW, _dot(q8b, w[hs, :].astype(BF16)) - sc * dw, NEG)
        s_wc = jnp.sum(q8 * kw_new, axis=1, keepdims=True)
        mw = jnp.maximum(jnp.max(s_w, axis=1, keepdims=True), s_wc)
        e_w = jnp.exp(s_w - mw)
        e_wc = jnp.exp(s_wc - mw)
        o_win = (_dot_nt(e_w.astype(BF16), w[vs_, :].astype(BF16))
                 + e_wc * vw_new) / (jnp.sum(e_w, axis=1, keepdims=True) + e_wc)

        def gate_col(br):
            cols = [sg[:, br * N_HEADS + g * GRP + m:br * N_HEADS + g * GRP + m + 1] for m in range(GRP)]
            return jnp.concatenate(cols + [jnp.zeros((SUBLANES - GRP, 1), F32)], axis=0)

        y_ref[g] = gate_col(0) * ocmp_ref[g] + gate_col(1) * o_slc + gate_col(2) * o_win


def _nsa_step(q, kvp, kvw, gt, kc, cache, page_table, cache_win, layer):
    n, n_pages = page_table.shape
    past_len = n_pages * PAGE
    ncmp = kc.shape[1]
    nsel = ncmp // 2
    pair = np.zeros((ncmp, nsel), np.float32)
    pair[np.arange(ncmp), np.arange(ncmp) // 2] = 1.0
    r3 = lambda a: a.reshape(n, 1, a.shape[-1])
    row3 = lambda w: pl.BlockSpec((None, 1, w), lambda b, *_: (b, 0, 0))
    nseq = math.gcd(n, SUBLANES // KV_HEADS)
    o_cmp, idx = pl.pallas_call(
        functools.partial(_nsa_step_cmp_body, past_len=past_len, nseq=nseq),
        grid=(n // nseq,),
        in_specs=[pl.BlockSpec((nseq, 1, ATTN_DIM), lambda b: (b, 0, 0)),
                  pl.BlockSpec((nseq, ncmp, CMP_W), lambda b: (b, 0, 0)), _full(pair.shape)],
        out_specs=[pl.BlockSpec((nseq, KV_HEADS, SUBLANES, HEAD_DIM), lambda b: (b, 0, 0, 0)),
                   pl.BlockSpec((nseq, 1, LANES), lambda b: (b, 0, 0))],
        out_shape=[jax.ShapeDtypeStruct((n, KV_HEADS, SUBLANES, HEAD_DIM), F32),
                   jax.ShapeDtypeStruct((n, 1, LANES), jnp.int32)],
        compiler_params=_cparams(("parallel",)),
        name="nsa_step_cmp",
    )(r3(q), kc, jnp.asarray(pair, BF16))
    sel = idx[:, 0, :KV_HEADS * N_SELECT].reshape(-1)

    def blk_spec(j):
        g, t = divmod(j, N_PICK)

        def imap(b, sel_ref, pt_ref):
            s = sel_ref[b * KV_HEADS * N_SELECT + g * N_SELECT + t]
            return (layer, pt_ref[b * n_pages + s // (PAGE // SEL_BLOCK)], 1, 0)

        return pl.BlockSpec((None, None, 2 * KV_DIM, PAGE), imap)

    nb = KV_HEADS * N_PICK
    grid_spec = pltpu.PrefetchScalarGridSpec(
        num_scalar_prefetch=2,
        grid=(n,),
        in_specs=[blk_spec(j) for j in range(nb)]
        + [pl.BlockSpec((None, None, CMP_W, cache_win.shape[3]), lambda b, *_: (layer, b, 0, 0)),
           row3(ATTN_DIM), row3(GATE_PAD), row3(4 * KV_DIM), row3(2 * KV_DIM),
           pl.BlockSpec((None, KV_HEADS, SUBLANES, HEAD_DIM), lambda b, *_: (b, 0, 0, 0))],
        out_specs=pl.BlockSpec((None, KV_HEADS, SUBLANES, HEAD_DIM), lambda b, *_: (b, 0, 0, 0)),
    )
    y = pl.pallas_call(
        functools.partial(_nsa_step_attn_body, past_len=past_len),
        grid_spec=grid_spec,
        out_shape=jax.ShapeDtypeStruct((n, KV_HEADS, SUBLANES, HEAD_DIM), F32),
        compiler_params=_cparams(("arbitrary",)),
        name="nsa_step_attn",
    )(sel, page_table.reshape(-1), *([cache] * nb), cache_win, r3(q), r3(gt), r3(kvp), r3(kvw), o_cmp)
    return y[:, :, :GRP, :].reshape(n, ATTN_DIM)


def _patch_tail_body(t_ref, big_ref, o_ref):
    del big_ref
    o_ref[...] = t_ref[...]


def _patch_tail(x, xt, bsz, seq):
    nt = seq // MTAIL
    return pl.pallas_call(
        _patch_tail_body,
        grid=(bsz,),
        in_specs=[pl.BlockSpec((MTAIL, D_MODEL), lambda b: (b, 0)), pl.BlockSpec(memory_space=pl.ANY)],
        out_specs=pl.BlockSpec((MTAIL, D_MODEL), lambda b: (b * nt + nt - 1, 0)),
        out_shape=jax.ShapeDtypeStruct(x.shape, x.dtype),
        input_output_aliases={1: 0},
        compiler_params=_cparams(("parallel",)),
        name="patch_tail",
    )(xt, x)


def kernel(x_prompt, x_sample, cache_kv, page_table, cache_win, state_ssm, state_conv, norm_attn_g, w_in,
           ssm_a_re, ssm_a_im, ssm_log_dt, ssm_b_re, ssm_b_im, ssm_c_re, ssm_c_im, ssm_d, ssm_w_glu, ssm_b_glu,
           cmp_pe, cmp_w1, cmp_w2, conv_w, w_br_ssm, w_br_attn, w_br_conv, w_out, norm_ffn_g,
           w_router_group, b_router_group, w_router_expert, b_router_expert, moe_w_gate, moe_w_up, moe_w_down,
           norm_final_g):
    bp, lp, _ = x_prompt.shape
    bs = x_sample.shape[0]
    depth = w_in.shape[0]
    n_pool = cache_kv.shape[1]
    nwin = cache_win.shape[2]
    xp = x_prompt.reshape(bp * lp, D_MODEL)
    xs = x_sample.reshape(bs, D_MODEL)
    cache = cache_kv.transpose(0, 1, 3, 4, 5, 2).reshape(depth, n_pool, 4 * KV_DIM, PAGE)
    cwin = cache_win.transpose(0, 1, 3, 4, 5, 2).reshape(depth, bs, 2 * KV_DIM, nwin)
    gf = norm_final_g.reshape(1, D_MODEL)
    kv_s, win_s, ssm_p, ssm_s, conv_p, conv_s = ([] for _ in range(6))
    kv_bufs = None

    def ssm_state(hre, him, n):
        return jnp.stack([hre.reshape(n, SSM_GROUPS, SSM_STATE), him.reshape(n, SSM_GROUPS, SSM_STATE)], axis=-1)

    for l in range(depth):
        w = w_in[l]
        w1, w1_lo = _hilo(jnp.concatenate(
            [w[:, :C_GATE], jnp.pad(w[:, C_GATE:C_CONV], ((0, 0), (0, GATE_PAD - 3 * N_HEADS)))], axis=1))
        g_attn = norm_attn_g[l].reshape(1, D_MODEL)
        g_ffn = norm_ffn_g[l].reshape(1, D_MODEL)
        s5w = _s5_prepare(ssm_a_re[l], ssm_a_im[l], ssm_log_dt[l], ssm_b_re[l], ssm_b_im[l],
                          ssm_c_re[l], ssm_c_im[l], ssm_d[l], ssm_w_glu[l], ssm_b_glu[l])
        cw = _cmp_prepare(cmp_pe[l], cmp_w1[l], cmp_w2[l])
        mw = _merge_weights(w[:, C_CONV:], w_br_ssm[l], w_br_attn[l], w_br_conv[l], w_out[l], conv_w[l])
        mo = _moe_prepare(w_router_group[l], b_router_group[l], w_router_expert[l], b_router_expert[l],
                          moe_w_gate[l], moe_w_up[l], moe_w_down[l])
        final = l == depth - 1

        u, q, gt, *kv_bufs = _inproj_seq(xp, g_attn, w1, bp, lp, l, depth, kv_bufs)
        y_ssm, hre, him, pre_r, pre_i = _s5_seq(u, s5w, bp, lp)
        kc = _compress_seq(kv_bufs[0], l, cw, bp, lp)
        y_attn = _nsa_seq(q, kv_bufs[0], kv_bufs[1], l, gt, kc, bp, lp)
        x1, cl, cpre = _merge_seq(xp, g_attn, y_ssm, y_attn, mw, bp, lp)
        x2 = _moe(x1, g_ffn, mo, gf, 1024, final=final)
        if not final and lp >= 2 * TAIL:
            def tail(a, n, per=lp):
                return a.reshape(bp, per, a.shape[-1])[:, per - n:].reshape(bp * n, a.shape[-1])

            ut = _inproj(tail(xp, TAIL), g_attn, w1, TAIL, w1_lo)[0]
            yst = _s5_seq(ut, s5w, bp, TAIL, tl=TAIL, precise=True, h0=(pre_r, pre_i))[0]
            x1t = _merge_seq(tail(xp, MTAIL), g_attn, tail(yst, MTAIL, TAIL), tail(y_attn, MTAIL), mw, bp, MTAIL,
                             tm=MTAIL, precise=True, cin=cpre)[0]
            x2t = _moe(x1t, g_ffn, mo, gf, bp * MTAIL, final=False)
            x2 = _patch_tail(x2, x2t, bp, lp)
        xp = x2
        ssm_p.append(ssm_state(hre[:, 0], him[:, 0], bp))
        conv_p.append(cl[:, SUBLANES - 2:])

        u, q, kvp, kvw, gt = _inproj(xs, g_attn, w1, bs, w1_lo)
        st = state_ssm[l]
        y_ssm, hre, him = _s5_step(u, st[..., 0].reshape(bs, SSM_N), st[..., 1].reshape(bs, SSM_N), s5w)
        kc = _compress_pages(cache, page_table, l, cw)
        y_attn = _nsa_step(q, kvp, kvw, gt, kc, cache, page_table, cwin, l)
        prev = state_conv[l]
        x1, uc = _merge_step(xs, g_attn, y_ssm, y_attn, prev[:, 0], prev[:, 1], mw)
        xs = _moe(x1, g_ffn, mo, gf, bs, final=final)
        kv_s.append(kvp.reshape(bs, 1, 4, KV_HEADS, HEAD_DIM))
        win_s.append(jnp.concatenate([cache_win[l][:, 1:], kvw.reshape(bs, 1, 2, KV_HEADS, HEAD_DIM)], axis=1))
        ssm_s.append(ssm_state(hre, him, bs))
        conv_s.append(jnp.stack([prev[:, 1], uc], axis=1))

    kvp_t, kvw_t = kv_bufs
    nw = min(WINDOW, lp)
    kv_prompt = kvp_t.reshape(depth, bp, 4, KV_HEADS, HEAD_DIM, lp).transpose(0, 1, 5, 2, 3, 4)
    win_prompt = kvw_t[..., lp - nw:].reshape(depth, bp, 2, KV_HEADS, HEAD_DIM, nw).transpose(0, 1, 5, 2, 3, 4)
    return (xp.reshape(bp, lp, D_MODEL), xs.reshape(bs, 1, D_MODEL),
            kv_prompt, jnp.stack(kv_s), win_prompt, jnp.stack(win_s),
            jnp.stack(ssm_p), jnp.stack(ssm_s), jnp.stack(conv_p), jnp.stack(conv_s))
```

```python
import functools
import math

import numpy as np
import jax
import jax.numpy as jnp
from jax import lax
from jax.experimental import pallas as pl
from jax.experimental.pallas import tpu as pltpu

F32 = jnp.float32
BF16 = jnp.bfloat16

D_MODEL = 1024
DEPTH = 2
PAGE = 128
SSM_GROUPS = 24
SSM_CH = 16
SSM_DIM = SSM_GROUPS * SSM_CH
SSM_STATE = 64
SSM_N = SSM_GROUPS * SSM_STATE
N_HEADS = 8
HEAD_DIM = 64
KV_HEADS = 2
GRP = N_HEADS // KV_HEADS
ATTN_DIM = N_HEADS * HEAD_DIM
KV_DIM = KV_HEADS * HEAD_DIM
CMP_BLOCK = 32
CMP_HIDDEN = 64
SEL_BLOCK = 64
N_SELECT = 8
WINDOW = 512
FORCE_BONUS = 1e4
CONV_DIM = 384
N_GROUPS = 4
EPG = 4
N_EXPERTS = 16
D_FF = 256
RMS_EPS = 1e-6

C_U = 0
C_Q = C_U + SSM_DIM
C_KVP = C_Q + ATTN_DIM
C_KVW = C_KVP + 4 * KV_DIM
C_GATE = C_KVW + 2 * KV_DIM
C_CONV = C_GATE + 3 * N_HEADS
C_MERGE = C_CONV + 3 * CONV_DIM
N_IN = C_MERGE + 3 * D_MODEL
W2_COLS = N_IN - C_CONV
GATE_PAD = 128
W1_COLS = C_GATE + GATE_PAD

LANES = 128
SUBLANES = 8
NEG = -1e30
M_INIT = -1e29
VMEM_LIMIT = 56 * 1024 * 1024


def _cparams(sem):
    return pltpu.CompilerParams(dimension_semantics=sem, vmem_limit_bytes=VMEM_LIMIT)


def _rms(x, g):
    ms = jnp.mean(x * x, axis=-1, keepdims=True)
    return x * lax.rsqrt(ms + RMS_EPS) * g


def _gelu_tanh(x):
    return 0.5 * x * (1.0 + jnp.tanh(math.sqrt(2.0 / math.pi) * (x + 0.044715 * (x * x * x))))


def _sigmoid(x):
    return 1.0 / (1.0 + jnp.exp(-x))


def _dot(a, b):
    return jnp.dot(a, b, preferred_element_type=F32)


def _dot_nt(a, b):
    return lax.dot_general(a, b, (((1,), (1,)), ((), ())), preferred_element_type=F32)


def _split3(x):
    hi = x.astype(BF16)
    r1 = x - hi.astype(F32)
    mid = r1.astype(BF16)
    lo = (r1 - mid.astype(F32)).astype(BF16)
    return hi, mid, lo


def _mm(a, w_ref, wlo_ref=None):
    ah = a.astype(BF16)
    if wlo_ref is None:
        return _dot(ah, w_ref[...])
    al = (a - ah.astype(F32)).astype(BF16)
    return _dot(ah, w_ref[...]) + (_dot(al, w_ref[...]) + _dot(ah, wlo_ref[...]))


def _hilo(w):
    hi = w.astype(BF16)
    return hi, (w - hi.astype(F32)).astype(BF16)


def _full(shape):
    nd = len(shape)
    return pl.BlockSpec(shape, lambda *_: (0,) * nd)


def _full1(shape):
    nd = len(shape)
    return pl.BlockSpec(shape, lambda *_: (0,) * nd, pipeline_mode=pl.Buffered(1))


def _inproj_body(x_ref, g_ref, w_ref, *refs, precise):
    wlo_ref = refs[0] if precise else None
    u_ref, q_ref, kvp_ref, kvw_ref, gt_ref = refs[1:] if precise else refs
    h = _rms(x_ref[...], g_ref[...])

    def proj(a, b):
        return _mm(h, w_ref.at[:, a:b], wlo_ref.at[:, a:b] if precise else None)

    u_ref[...] = proj(C_U, C_Q)
    q_ref[...] = proj(C_Q, C_KVP)
    kvp_ref[...] = proj(C_KVP, C_KVW)
    kvw_ref[...] = proj(C_KVW, C_GATE)
    gt_ref[...] = proj(C_GATE, W1_COLS)


def _inproj(x, g, w1, tm, w1_lo=None):
    n = x.shape[0]
    widths = (SSM_DIM, ATTN_DIM, 4 * KV_DIM, 2 * KV_DIM, GATE_PAD)
    ws = [w1] if w1_lo is None else [w1, w1_lo]
    return pl.pallas_call(
        functools.partial(_inproj_body, precise=w1_lo is not None),
        grid=(n // tm,),
        in_specs=[
            pl.BlockSpec((tm, D_MODEL), lambda i: (i, 0)),
            _full((1, D_MODEL)),
        ] + [_full((D_MODEL, W1_COLS))] * len(ws),
        out_specs=[pl.BlockSpec((tm, w), lambda i: (i, 0)) for w in widths],
        out_shape=[jax.ShapeDtypeStruct((n, w), F32) for w in widths],
        compiler_params=_cparams(("parallel",)),
        name="inproj",
    )(x, g, *ws)


def _inproj_seq_body(x_ref, g_ref, w_ref, *refs, aliased):
    u_ref, q_ref, gt_ref, kvp_ref, kvw_ref = refs[2:] if aliased else refs
    hb = _rms(x_ref[...], g_ref[...]).astype(BF16)

    def proj(a, b):
        return _dot(hb, w_ref[:, a:b])

    u_ref[...] = proj(C_U, C_Q)
    q_ref[...] = proj(C_Q, C_KVP)
    gt_ref[...] = proj(C_GATE, W1_COLS)
    kvp_ref[...] = proj(C_KVP, C_KVW).T
    kvw_ref[...] = proj(C_KVW, C_GATE).T


def _inproj_seq(x, g, w1, bsz, seq, layer, depth, kv_bufs=None, tm=512):
    nt = seq // tm
    widths = (SSM_DIM, ATTN_DIM, GATE_PAD)
    row = lambda w: pl.BlockSpec((tm, w), lambda b, i: (b * nt + i, 0))
    kv_spec = lambda f: pl.BlockSpec((None, None, f, tm), lambda b, i: (layer, b, 0, i))
    aliased = kv_bufs is not None
    extra = list(kv_bufs) if aliased else []
    any_spec = pl.BlockSpec(memory_space=pl.ANY)
    return pl.pallas_call(
        functools.partial(_inproj_seq_body, aliased=aliased),
        grid=(bsz, nt),
        in_specs=[row(D_MODEL), _full((1, D_MODEL)), _full((D_MODEL, W1_COLS))] + [any_spec] * len(extra),
        out_specs=[row(w) for w in widths] + [kv_spec(4 * KV_DIM), kv_spec(2 * KV_DIM)],
        out_shape=[jax.ShapeDtypeStruct((bsz * seq, w), F32) for w in widths]
        + [jax.ShapeDtypeStruct((depth, bsz, 4 * KV_DIM, seq), F32),
           jax.ShapeDtypeStruct((depth, bsz, 2 * KV_DIM, seq), F32)],
        input_output_aliases={3: 3, 4: 4} if aliased else {},
        compiler_params=_cparams(("parallel", "parallel")),
        name="inproj_seq",
    )(x, g, w1, *extra)


S5_CHUNK = 128
S5_J = S5_CHUNK // SUBLANES


def _s5_tables_body(ar_r, ai_r, ldt_r, ar_c, ai_c, ldt_c, bre_ref, bim_ref,
                    a8re, a8im, apre, apim, a16re, a16im, bbre, bbim):
    dt = jnp.exp(ldt_r[...])
    mag = jnp.exp(ar_r[...] * dt)
    ang = ai_r[...] * dt
    are = mag * jnp.cos(ang)
    aim = mag * jnp.sin(ang)
    a8re[...] = jnp.broadcast_to(are, (SUBLANES, SSM_N))
    a8im[...] = jnp.broadcast_to(aim, (SUBLANES, SSM_N))
    pr, pi = are, aim
    for j in range(S5_J):
        apre[j * SUBLANES:(j + 1) * SUBLANES, :] = jnp.broadcast_to(pr, (SUBLANES, SSM_N))
        apim[j * SUBLANES:(j + 1) * SUBLANES, :] = jnp.broadcast_to(pi, (SUBLANES, SSM_N))
        if j + 1 < S5_J:
            pr, pi = pr * are - pi * aim, pr * aim + pi * are
    sre, sim = pr, pi
    qr, qi = sre, sim
    for s in range(SUBLANES):
        a16re[s:s + 1, :] = qr
        a16im[s:s + 1, :] = qi
        if s + 1 < SUBLANES:
            qr, qi = qr * sre - qi * sim, qr * sim + qi * sre
    dtc = jnp.exp(ldt_c[...])
    arc, aic = ar_c[...], ai_c[...]
    magc = jnp.exp(arc * dtc)
    angc = aic * dtc
    arec = magc * jnp.cos(angc)
    aimc = magc * jnp.sin(angc)
    den = arc * arc + aic * aic
    cre = ((arec - 1.0) * arc + aimc * aic) / den
    cim = (aimc * arc - (arec - 1.0) * aic) / den
    br, bi = bre_ref[...], bim_ref[...]
    bbre[...] = cre * br - cim * bi
    bbim[...] = cre * bi + cim * br


def _s5_tables(a_re, a_im, log_dt, b_re, b_im):
    ar_r = a_re.reshape(1, SSM_N)
    ai_r = a_im.reshape(1, SSM_N)
    ldt_r = jnp.repeat(log_dt, SSM_STATE).reshape(1, SSM_N)
    row = jax.ShapeDtypeStruct((SUBLANES, SSM_N), F32)
    tab = jax.ShapeDtypeStruct((S5_CHUNK, SSM_N), F32)
    col = jax.ShapeDtypeStruct((SSM_N, SSM_CH), F32)
    return pl.pallas_call(
        _s5_tables_body,
        out_shape=[row, row, tab, tab, row, row, col, col],
        name="s5_tables",
    )(ar_r, ai_r, ldt_r, ar_r.reshape(SSM_N, 1), ai_r.reshape(SSM_N, 1), ldt_r.reshape(SSM_N, 1),
      b_re.reshape(SSM_N, SSM_CH), b_im.reshape(SSM_N, SSM_CH))


def _block_diag_b(bb):
    t = bb.reshape(SSM_GROUPS, SSM_STATE, SSM_CH).transpose(0, 2, 1)
    eye = jnp.eye(SSM_GROUPS, dtype=bb.dtype)
    return (t[:, :, None, :] * eye[:, None, :, None]).reshape(SSM_DIM, SSM_N)


def _block_diag_c(c):
    t = c.transpose(0, 2, 1)
    eye = jnp.eye(SSM_GROUPS, dtype=c.dtype)
    return (t[:, :, None, :] * eye[:, None, :, None]).reshape(SSM_N, SSM_DIM)


def _s5_perm():
    p = np.zeros((S5_CHUNK, S5_CHUNK), np.float32)
    for j in range(S5_J):
        for s in range(SUBLANES):
            p[j * SUBLANES + s, S5_J * s + j] = 1.0
    return p


S5_CL = 3
S5_CU = SSM_DIM // S5_CL
S5_CS = SSM_N // S5_CL


def _s5_bu(u_hi, u_lo, bmat_ref, bmat_lo):
    res, ims = [], []
    for c in range(S5_CL):
        cols = slice(c * S5_CU, (c + 1) * S5_CU)
        b = _dot(u_hi[:, cols], bmat_ref[c])
        if bmat_lo is not None:
            b = b + (_dot(u_lo[:, cols], bmat_ref[c]) + _dot(u_hi[:, cols], bmat_lo[c]))
        res.append(b[:, :S5_CS])
        ims.append(b[:, S5_CS:])
    return jnp.concatenate(res, axis=1), jnp.concatenate(ims, axis=1)


def _s5_cy(hr, hi, cmat_ref, cmat_lo):
    ys = []
    for c in range(S5_CL):
        cols = slice(c * S5_CS, (c + 1) * S5_CS)
        h = jnp.concatenate([hr[:, cols], hi[:, cols]], axis=1)
        ys.append(_mm(h, cmat_ref.at[c], None if cmat_lo is None else cmat_lo.at[c]))
    return jnp.concatenate(ys, axis=1)


def _s5_epilogue(y, u, d_ref, wglu_ref, wglu_lo, bglu_ref):
    y = y + d_ref[...] * u
    g = _gelu_tanh(y)
    return g * _sigmoid(_mm(g, wglu_ref, wglu_lo) + bglu_ref[...])


def _s5_seq_body(*refs, tl, precise, carry_in):
    it = iter(refs)
    u_ref = next(it)
    h0r_ref, h0i_ref = (next(it), next(it)) if carry_in else (None, None)
    (perm_ref, permt_ref, bmat_ref, cmat_ref, a8re_ref, a8im_ref, apre_ref, apim_ref, a16re_ref, a16im_ref,
     d_ref, wglu_ref, bglu_ref) = (next(it) for _ in range(13))
    bmat_lo, cmat_lo, wglu_lo = (next(it), next(it), next(it)) if precise else (None, None, None)
    y_ref, hre_ref, him_ref, pre_r_ref, pre_i_ref, cre_scr, cim_scr = it
    li = pl.program_id(1)

    @pl.when(li == 0)
    def _():
        if carry_in:
            cre_scr[...] = h0r_ref[...]
            cim_scr[...] = h0i_ref[...]
        else:
            cre_scr[...] = jnp.zeros_like(cre_scr)
            cim_scr[...] = jnp.zeros_like(cim_scr)

    are = a8re_ref[...]
    aim = a8im_ref[...]
    row8 = lax.broadcasted_iota(jnp.int32, (SUBLANES, SSM_N), 0)
    nchunk = tl // S5_CHUNK

    for c in range(nchunk):
        if c == nchunk - 1:
            pre_r_ref[...] = cre_scr[...]
            pre_i_ref[...] = cim_scr[...]
        u = u_ref[c * S5_CHUNK:(c + 1) * S5_CHUNK, :]
        u_hi = u.astype(BF16)
        up = _dot(perm_ref[...], u_hi).astype(BF16)
        up_lo = (_dot(perm_ref[...], (u - u_hi.astype(F32)).astype(BF16)).astype(BF16) if precise else None)
        bu_re, bu_im = _s5_bu(up, up_lo, bmat_ref, bmat_lo)
        hr = [bu_re[0:SUBLANES, :]]
        hi = [bu_im[0:SUBLANES, :]]
        for j in range(1, S5_J):
            br = bu_re[j * SUBLANES:(j + 1) * SUBLANES, :]
            bi = bu_im[j * SUBLANES:(j + 1) * SUBLANES, :]
            hr.append(are * hr[-1] - aim * hi[-1] + br)
            hi.append(are * hi[-1] + aim * hr[-2] + bi)
        er, ei = hr[-1], hi[-1]
        for k, d in enumerate((1, 2, 4)):
            mr = jnp.broadcast_to(a16re_ref[d - 1:d, :], (SUBLANES, SSM_N))
            mi = jnp.broadcast_to(a16im_ref[d - 1:d, :], (SUBLANES, SSM_N))
            sr = jnp.where(row8 >= d, pltpu.roll(er, d, 0), 0.0)
            si = jnp.where(row8 >= d, pltpu.roll(ei, d, 0), 0.0)
            er, ei = er + mr * sr - mi * si, ei + mr * si + mi * sr
        h0r = jnp.broadcast_to(cre_scr[0:1, :], (SUBLANES, SSM_N))
        h0i = jnp.broadcast_to(cim_scr[0:1, :], (SUBLANES, SSM_N))
        p16r, p16i = a16re_ref[...], a16im_ref[...]
        er, ei = er + p16r * h0r - p16i * h0i, ei + p16r * h0i + p16i * h0r
        cinr = jnp.where(row8 == 0, h0r, pltpu.roll(er, 1, 0))
        cini = jnp.where(row8 == 0, h0i, pltpu.roll(ei, 1, 0))
        cre_scr[...] = jnp.broadcast_to(er[SUBLANES - 1:SUBLANES, :], (SUBLANES, SSM_N))
        cim_scr[...] = jnp.broadcast_to(ei[SUBLANES - 1:SUBLANES, :], (SUBLANES, SSM_N))
        fr, fi = [], []
        for j in range(S5_J):
            pr = apre_ref[j * SUBLANES:(j + 1) * SUBLANES, :]
            pi = apim_ref[j * SUBLANES:(j + 1) * SUBLANES, :]
            fr.append(hr[j] + pr * cinr - pi * cini)
            fi.append(hi[j] + pr * cini + pi * cinr)
        yp = _s5_cy(jnp.concatenate(fr, axis=0), jnp.concatenate(fi, axis=0), cmat_ref, cmat_lo)
        y_hi, y_mid, y_lo = _split3(yp)
        pt = permt_ref[...]
        y = _dot(pt, y_hi) + _dot(pt, y_mid) + _dot(pt, y_lo)
        y_ref[c * S5_CHUNK:(c + 1) * S5_CHUNK, :] = _s5_epilogue(y, u, d_ref, wglu_ref, wglu_lo, bglu_ref)

    hre_ref[...] = cre_scr[...]
    him_ref[...] = cim_scr[...]


def _s5_seq(u, s5w, bsz, seq, tl=512, precise=False, h0=None):
    nt = seq // tl
    consts = list(s5w["tabs"]) + [s5w["d"], s5w["wglu"], s5w["bglu"]]
    if precise:
        consts += [s5w["bmat_lo"], s5w["cmat_lo"], s5w["wglu_lo"]]
    state_spec = pl.BlockSpec((None, SUBLANES, SSM_N), lambda b, i: (b, 0, 0))
    state_shape = jax.ShapeDtypeStruct((bsz, SUBLANES, SSM_N), F32)
    carry = [] if h0 is None else list(h0)
    return pl.pallas_call(
        functools.partial(_s5_seq_body, tl=tl, precise=precise, carry_in=h0 is not None),
        grid=(bsz, nt),
        in_specs=[pl.BlockSpec((tl, SSM_DIM), lambda b, i: (b * nt + i, 0))]
        + [state_spec] * len(carry) + [_full(c.shape) for c in consts],
        out_specs=[pl.BlockSpec((tl, SSM_DIM), lambda b, i: (b * nt + i, 0))] + [state_spec] * 4,
        out_shape=[jax.ShapeDtypeStruct((bsz * seq, SSM_DIM), F32)] + [state_shape] * 4,
        scratch_shapes=[pltpu.VMEM((SUBLANES, SSM_N), F32), pltpu.VMEM((SUBLANES, SSM_N), F32)],
        compiler_params=_cparams(("parallel", "arbitrary")),
        name="s5_seq",
    )(u, *carry, *consts)


def _s5_step_body(u_ref, h0r_ref, h0i_ref, bmat_ref, cmat_ref, a8re_ref, a8im_ref,
                  d_ref, wglu_ref, bglu_ref, bmat_lo, cmat_lo, wglu_lo, y_ref, hre_ref, him_ref):
    u = u_ref[...]
    u_hi = u.astype(BF16)
    bu_re, bu_im = _s5_bu(u_hi, (u - u_hi.astype(F32)).astype(BF16), bmat_ref, bmat_lo)
    are = a8re_ref[0:1, :]
    aim = a8im_ref[0:1, :]
    h0r, h0i = h0r_ref[...], h0i_ref[...]
    hr = are * h0r - aim * h0i + bu_re
    hi = are * h0i + aim * h0r + bu_im
    hre_ref[...] = hr
    him_ref[...] = hi
    y = _s5_cy(hr, hi, cmat_ref, cmat_lo)
    y_ref[...] = _s5_epilogue(y, u, d_ref, wglu_ref, wglu_lo, bglu_ref)


def _s5_step(u, h0r, h0i, s5w):
    _, _, bmat, cmat, a8re, a8im = s5w["tabs"][:6]
    n = u.shape[0]
    return pl.pallas_call(
        _s5_step_body,
        out_shape=[
            jax.ShapeDtypeStruct((n, SSM_DIM), F32),
            jax.ShapeDtypeStruct((n, SSM_N), F32),
            jax.ShapeDtypeStruct((n, SSM_N), F32),
        ],
        compiler_params=pltpu.CompilerParams(vmem_limit_bytes=VMEM_LIMIT),
        name="s5_step",
    )(u, h0r, h0i, bmat, cmat, a8re, a8im, s5w["d"], s5w["wglu"], s5w["bglu"],
      s5w["bmat_lo"], s5w["cmat_lo"], s5w["wglu_lo"])


def _s5_prepare(a_re, a_im, log_dt, b_re, b_im, c_re, c_im, d_skip, w_glu, b_glu):
    a8re, a8im, apre, apim, a16re, a16im, bbre, bbim = _s5_tables(a_re, a_im, log_dt, b_re, b_im)
    bre, bim = _block_diag_b(bbre), _block_diag_b(bbim)
    cre, cim = _block_diag_c(c_re), _block_diag_c(c_im)
    us = lambda c: slice(c * S5_CU, (c + 1) * S5_CU)
    ss = lambda c: slice(c * S5_CS, (c + 1) * S5_CS)
    bmat, bmat_lo = _hilo(jnp.stack(
        [jnp.concatenate([bre[us(c), ss(c)], bim[us(c), ss(c)]], axis=1) for c in range(S5_CL)]))
    cmat, cmat_lo = _hilo(jnp.stack(
        [jnp.concatenate([cre[ss(c), us(c)], -cim[ss(c), us(c)]], axis=0) for c in range(S5_CL)]))
    wglu, wglu_lo = _hilo(w_glu)
    perm = _s5_perm()
    tabs = (jnp.asarray(perm, BF16), jnp.asarray(perm.T, BF16), bmat, cmat, a8re, a8im, apre, apim, a16re, a16im)
    return dict(tabs=tabs, d=d_skip.reshape(1, SSM_DIM), wglu=wglu, bglu=b_glu.reshape(1, SSM_DIM),
                bmat_lo=bmat_lo, cmat_lo=cmat_lo, wglu_lo=wglu_lo)


TAIL = 128
MTAIL = 16

MERGE_W = ("w2", "wbs", "wba", "wbc", "wo")


def _merge_weights(w2, w_br_ssm, w_br_attn, w_br_conv, w_out, conv_w):
    mw = {"cw": conv_w}
    for name, w in zip(MERGE_W, (w2, w_br_ssm, w_br_attn, w_br_conv, w_out)):
        mw[name], mw[name + "_lo"] = _hilo(w)
    return mw


def _merge_wlist(mw, precise):
    return [mw["cw"]] + [mw[n] for n in MERGE_W] + ([mw[n + "_lo"] for n in MERGE_W] if precise else [])


def _merge_core(x, g_ref, wrefs, precise, up0, up1, ys_ref, ya_ref):
    cw_ref, w2_ref, wbs_ref, wba_ref, wbc_ref, wo_ref = wrefs[:6]
    w2_lo, wbs_lo, wba_lo, wbc_lo, wo_lo = wrefs[6:] if precise else (None,) * 5
    zc = _mm(_rms(x, g_ref[...]), w2_ref, w2_lo)
    conv_b = zc[:, 0:CONV_DIM]
    uc = zc[:, CONV_DIM:2 * CONV_DIM] * zc[:, 2 * CONV_DIM:3 * CONV_DIM]
    yc = conv_b * (cw_ref[0:1, :] * up0(uc) + cw_ref[1:2, :] * up1(uc) + cw_ref[2:3, :] * uc)
    g0 = 3 * CONV_DIM
    g_ssm = _sigmoid(zc[:, g0:g0 + D_MODEL])
    g_attn = _sigmoid(zc[:, g0 + D_MODEL:g0 + 2 * D_MODEL])
    g_conv = _sigmoid(zc[:, g0 + 2 * D_MODEL:g0 + 3 * D_MODEL])
    merged = (g_ssm * _mm(ys_ref[...], wbs_ref, wbs_lo)
              + g_attn * _mm(ya_ref[...], wba_ref, wba_lo)
              + g_conv * _mm(yc, wbc_ref, wbc_lo))
    return x + _mm(merged, wo_ref, wo_lo), uc


def _merge_seq_body(*refs, tm, precise, carry_in):
    it = iter(refs)
    x_ref, g_ref, ys_ref, ya_ref = (next(it) for _ in range(4))
    cin_ref = next(it) if carry_in else None
    wrefs = [next(it) for _ in range(11 if precise else 6)]
    o_ref, cl_ref, pre_ref, stage = it
    ti = pl.program_id(1)

    @pl.when(ti == 0)
    def _():
        stage[0:SUBLANES, :] = cin_ref[...] if carry_in else jnp.zeros((SUBLANES, CONV_DIM), F32)

    def up0(uc):
        stage[SUBLANES:SUBLANES + tm, :] = uc
        return stage[SUBLANES - 2:SUBLANES - 2 + tm, :]

    def up1(uc):
        return stage[SUBLANES - 1:SUBLANES - 1 + tm, :]

    out, uc = _merge_core(x_ref[...], g_ref, wrefs, precise, up0, up1, ys_ref, ya_ref)
    o_ref[...] = out
    last = uc[tm - SUBLANES:tm, :]
    stage[0:SUBLANES, :] = last
    cl_ref[...] = last
    pre_ref[...] = uc[max(tm - MTAIL, SUBLANES) - SUBLANES:max(tm - MTAIL, SUBLANES), :]


def _merge_seq(x, g, ys, ya, mw, bsz, seq, tm=256, precise=False, cin=None):
    nt = seq // tm
    ws = _merge_wlist(mw, precise)
    row = lambda w: pl.BlockSpec((tm, w), lambda b, i: (b * nt + i, 0))
    cspec = pl.BlockSpec((None, SUBLANES, CONV_DIM), lambda b, i: (b, 0, 0))
    carry = [] if cin is None else [cin]
    return pl.pallas_call(
        functools.partial(_merge_seq_body, tm=tm, precise=precise, carry_in=cin is not None),
        grid=(bsz, nt),
        in_specs=[row(D_MODEL), _full(g.shape), row(SSM_DIM), row(ATTN_DIM)] + [cspec] * len(carry)
        + [_full1(w.shape) for w in ws],
        out_specs=[row(D_MODEL), cspec, cspec],
        out_shape=[jax.ShapeDtypeStruct((bsz * seq, D_MODEL), F32)]
        + [jax.ShapeDtypeStruct((bsz, SUBLANES, CONV_DIM), F32)] * 2,
        scratch_shapes=[pltpu.VMEM((tm + SUBLANES, CONV_DIM), F32)],
        compiler_params=_cparams(("parallel", "arbitrary")),
        name="merge_seq",
    )(x, g, ys, ya, *carry, *ws)


def _merge_step_body(x_ref, g_ref, ys_ref, ya_ref, p0_ref, p1_ref, *refs):
    wrefs, (o_ref, uc_ref) = refs[:11], refs[11:]
    out, uc = _merge_core(x_ref[...], g_ref, wrefs, True, lambda _: p0_ref[...], lambda _: p1_ref[...],
                          ys_ref, ya_ref)
    o_ref[...] = out
    uc_ref[...] = uc


def _merge_step(x, g, ys, ya, prev0, prev1, mw):
    n = x.shape[0]
    return pl.pallas_call(
        _merge_step_body,
        out_shape=[jax.ShapeDtypeStruct((n, D_MODEL), F32), jax.ShapeDtypeStruct((n, CONV_DIM), F32)],
        compiler_params=pltpu.CompilerParams(vmem_limit_bytes=VMEM_LIMIT),
        name="merge_step",
    )(x, g, ys, ya, prev0, prev1, *_merge_wlist(mw, True))


ROUTE_E0 = SUBLANES


def _route_math(lg, le):
    gmax = functools.reduce(jnp.maximum, lg)
    gsum = functools.reduce(lambda a, b: a + b, [jnp.exp(v - gmax) for v in lg])
    gw = 1.0 / gsum
    gsel = jnp.full_like(gmax, N_GROUPS - 1).astype(jnp.int32)
    for k in range(N_GROUPS - 2, -1, -1):
        gsel = jnp.where(lg[k] == gmax, k, gsel)
    ls = []
    for j in range(EPG):
        v = le[j]
        for k in range(1, N_GROUPS):
            v = jnp.where(gsel == k, le[k * EPG + j], v)
        ls.append(v)
    emax = functools.reduce(jnp.maximum, ls)
    ex = [jnp.exp(v - emax) for v in ls]
    esum = functools.reduce(lambda a, b: a + b, ex)
    pe = [v / esum for v in ex]
    v1 = functools.reduce(jnp.maximum, pe)
    i1 = jnp.full_like(gsel, EPG - 1)
    for j in range(EPG - 2, -1, -1):
        i1 = jnp.where(pe[j] == v1, j, i1)
    pe2 = [jnp.where(i1 == j, -1.0, pe[j]) for j in range(EPG)]
    v2 = functools.reduce(jnp.maximum, pe2)
    i2 = jnp.full_like(gsel, EPG - 1)
    for j in range(EPG - 2, -1, -1):
        i2 = jnp.where(pe2[j] == v2, j, i2)
    tot = v1 + v2
    w1 = v1 / tot * gw
    w2 = v2 / tot * gw
    return gsel * EPG + i1, gsel * EPG + i2, w1, w2


def _moe_route(h, wr_ref, wrt_ref, br_ref, brt_ref):
    tm = h.shape[0]
    h_hi = h.astype(BF16)
    h_lo = (h - h_hi.astype(F32)).astype(BF16)
    if tm % LANES:
        logits = _dot(h_hi, wr_ref[0]) + (_dot(h_lo, wr_ref[0]) + _dot(h_hi, wr_ref[1])) + br_ref[...]
        lg = [logits[:, k:k + 1] for k in range(N_GROUPS)]
        le = [logits[:, ROUTE_E0 + e:ROUTE_E0 + e + 1] for e in range(N_EXPERTS)]
        e1, e2, w1, w2 = _route_math(lg, le)
        lane = lax.broadcasted_iota(jnp.int32, (tm, LANES), 1)
        return jnp.where(lane == e1, w1, 0.0) + jnp.where(lane == e2, w2, 0.0)
    lt = _dot_nt(wrt_ref[0], h_hi) + (_dot_nt(wrt_ref[0], h_lo) + _dot_nt(wrt_ref[1], h_hi)) + brt_ref[...]
    lg = [lt[k:k + 1, :] for k in range(N_GROUPS)]
    le = [lt[ROUTE_E0 + e:ROUTE_E0 + e + 1, :] for e in range(N_EXPERTS)]
    e1, e2, w1, w2 = _route_math(lg, le)
    rows = [jnp.where(e1 == e, w1, 0.0) + jnp.where(e2 == e, w2, 0.0) for e in range(N_EXPERTS)]
    comb_t = jnp.concatenate(rows + [jnp.zeros((LANES - N_EXPERTS, tm), F32)], axis=0)
    return comb_t.T


def _moe_body(x_ref, g_ref, wr_ref, wrt_ref, br_ref, brt_ref, wgu_ref, wd_ref, gf_ref, o_ref,
              hb_scr, comb_scr, acc_scr, *, final):
    e = pl.program_id(1)

    @pl.when(e == 0)
    def _():
        h = _rms(x_ref[...], g_ref[...])
        hb_scr[...] = h.astype(BF16)
        comb_scr[...] = _moe_route(h, wr_ref, wrt_ref, br_ref, brt_ref)
        acc_scr[...] = jnp.zeros_like(acc_scr)

    hgu = _dot(hb_scr[...], wgu_ref[0])
    hg = hgu[:, :D_FF]
    lane = lax.broadcasted_iota(jnp.int32, comb_scr.shape, 1)
    ce = jnp.sum(jnp.where(lane == e, comb_scr[...], 0.0), axis=1, keepdims=True)
    act = hg * _sigmoid(hg) * hgu[:, D_FF:] * ce
    acc_scr[...] += _dot(act.astype(BF16), wd_ref[0])

    @pl.when(e == N_EXPERTS - 1)
    def _():
        y = x_ref[...] + acc_scr[...]
        o_ref[...] = _rms(y, gf_ref[...]) if final else y


def _moe(x, g, mo, gf, tm, final=False):
    n = x.shape[0]
    wr, wrt, br, brt, wgu, wd = (mo[k] for k in ("wr", "wrt", "br", "brt", "wgu", "wd"))
    return pl.pallas_call(
        functools.partial(_moe_body, final=final),
        grid=(n // tm, N_EXPERTS),
        in_specs=[
            pl.BlockSpec((tm, D_MODEL), lambda i, e: (i, 0)),
            _full(g.shape), _full(wr.shape), _full(wrt.shape), _full(br.shape), _full(brt.shape),
            pl.BlockSpec((1, D_MODEL, 2 * D_FF), lambda i, e: (e, 0, 0)),
            pl.BlockSpec((1, D_FF, D_MODEL), lambda i, e: (e, 0, 0)),
            _full(gf.shape),
        ],
        out_specs=pl.BlockSpec((tm, D_MODEL), lambda i, e: (i, 0)),
        out_shape=jax.ShapeDtypeStruct((n, D_MODEL), F32),
        scratch_shapes=[pltpu.VMEM((tm, D_MODEL), BF16), pltpu.VMEM((tm, LANES), F32),
                        pltpu.VMEM((tm, D_MODEL), F32)],
        compiler_params=_cparams(("parallel", "arbitrary")),
        name="moe",
    )(x, g, wr, wrt, br, brt, wgu, wd, gf)


def _moe_prepare(w_rg, b_rg, w_re, b_re, w_gate, w_up, w_down):
    wr = jnp.zeros((D_MODEL, LANES), F32)
    wr = wr.at[:, :N_GROUPS].set(w_rg).at[:, ROUTE_E0:ROUTE_E0 + N_EXPERTS].set(w_re)
    br = jnp.zeros((1, LANES), F32)
    br = br.at[0, :N_GROUPS].set(b_rg).at[0, ROUTE_E0:ROUTE_E0 + N_EXPERTS].set(b_re)
    wr2 = jnp.stack(_hilo(wr))
    return dict(wr=wr2, wrt=wr2.transpose(0, 2, 1), br=br, brt=br.reshape(LANES, 1),
                wgu=jnp.concatenate([w_gate, w_up], axis=-1).astype(BF16),
                wd=w_down.astype(BF16))


CMP_SUB = 256
CMP_W = 4 * HEAD_DIM
PAGES_PER_STEP = 64


def _cmp_perm():
    p = np.zeros((CMP_SUB, CMP_SUB), np.float32)
    for i in range(CMP_SUB):
        p[i, CMP_BLOCK * (i % SUBLANES) + i // SUBLANES] = 1.0
    return p


def _cmp_prepare(pe, w1, w2):
    eye4 = jnp.eye(4, dtype=F32)
    sel = jnp.array([0, 0, 1, 1])
    w1r = w1.reshape(2, CMP_BLOCK, HEAD_DIM, CMP_HIDDEN)[sel]
    w1bd = (w1r.transpose(1, 0, 2, 3)[:, :, :, None, :] * eye4[None, :, None, :, None])
    w1bd = w1bd.reshape(CMP_BLOCK, CMP_W, 4 * CMP_HIDDEN).astype(BF16)
    w2r = w2[sel]
    w2bd = (w2r[:, :, None, :] * eye4[:, None, :, None]).reshape(4 * CMP_HIDDEN, CMP_W).astype(BF16)
    pe4 = pe[sel].transpose(1, 0, 2).reshape(CMP_BLOCK, CMP_W)
    pe_exp = jnp.repeat(pe4, SUBLANES, axis=0)
    return jnp.asarray(_cmp_perm(), BF16), pe_exp, w1bd, w2bd


def _compress_rows(get_sub, nsub, perm_ref, pe_ref, w1_ref, w2_ref, stage, transposed=False):
    mm = _dot_nt if transposed else _dot
    for t in range(nsub):
        xp = mm(perm_ref[...], get_sub(t).astype(BF16))
        stage[t] = xp + pe_ref[...]
    acc = jnp.zeros((nsub * SUBLANES, 4 * CMP_HIDDEN), F32)
    for r in range(CMP_BLOCK):
        a = stage[:, r * SUBLANES:(r + 1) * SUBLANES, :].reshape(nsub * SUBLANES, CMP_W)
        acc = acc + _dot(a.astype(BF16), w1_ref[r])
    return _dot(_gelu_tanh(acc).astype(BF16), w2_ref[...])


def _compress_seq_body(x_ref, perm_ref, pe_ref, w1_ref, w2_ref, o_ref, stage, *, nsub):
    o_ref[...] = _compress_rows(lambda t: x_ref[:, t * CMP_SUB:(t + 1) * CMP_SUB], nsub,
                                perm_ref, pe_ref, w1_ref, w2_ref, stage, transposed=True)


def _compress_seq(kvp_t, layer, cw, bsz, seq):
    nsub = seq // CMP_SUB
    nblk = seq // CMP_BLOCK
    return pl.pallas_call(
        functools.partial(_compress_seq_body, nsub=nsub),
        grid=(bsz,),
        in_specs=[pl.BlockSpec((None, None, CMP_W, seq), lambda b: (layer, b, 0, 0))]
        + [_full(c.shape) for c in cw],
        out_specs=pl.BlockSpec((None, nblk, CMP_W), lambda b: (b, 0, 0)),
        out_shape=jax.ShapeDtypeStruct((bsz, nblk, CMP_W), F32),
        scratch_shapes=[pltpu.VMEM((nsub, CMP_SUB, CMP_W), F32)],
        compiler_params=_cparams(("parallel",)),
        name="compress_seq",
    )(kvp_t, *cw)


def _compress_pages_body(pt_ref, *refs):
    pages = refs[:PAGES_PER_STEP]
    perm_ref, pe_ref, w1_ref, w2_ref, o_ref, stage = refs[PAGES_PER_STEP:]
    per = CMP_SUB // PAGE

    def get_sub(t):
        return jnp.concatenate([pages[per * t + k][...] for k in range(per)], axis=1)

    o_ref[...] = _compress_rows(get_sub, PAGES_PER_STEP // per, perm_ref, pe_ref, w1_ref, w2_ref, stage,
                                transposed=True)


def _compress_pages(cache_t, page_table, layer, cw):
    bsz, n_pages = page_table.shape
    steps = n_pages // PAGES_PER_STEP
    nsub = PAGES_PER_STEP * PAGE // CMP_SUB
    blk_per_step = PAGES_PER_STEP * PAGE // CMP_BLOCK

    def page_spec(k):
        return pl.BlockSpec((None, None, CMP_W, PAGE),
                            lambda b, i, pt: (layer, pt[b * n_pages + i * PAGES_PER_STEP + k], 0, 0))

    grid_spec = pltpu.PrefetchScalarGridSpec(
        num_scalar_prefetch=1,
        grid=(bsz, steps),
        in_specs=[page_spec(k) for k in range(PAGES_PER_STEP)]
        + [pl.BlockSpec(c.shape, lambda b, i, pt, nd=c.ndim: (0,) * nd) for c in cw],
        out_specs=pl.BlockSpec((None, blk_per_step, CMP_W), lambda b, i, pt: (b, i, 0)),
        scratch_shapes=[pltpu.VMEM((nsub, CMP_SUB, CMP_W), F32)],
    )
    return pl.pallas_call(
        _compress_pages_body,
        grid_spec=grid_spec,
        out_shape=jax.ShapeDtypeStruct((bsz, n_pages * PAGE // CMP_BLOCK, CMP_W), F32),
        compiler_params=_cparams(("parallel", "arbitrary")),
        name="compress_pages",
    )(page_table.reshape(-1), *([cache_t] * PAGES_PER_STEP), *cw)


TQ = 128
SLC_CHUNK = 1024
QL = GRP * TQ
N_SELBLK_SEQ = 32
AUG_MASK = HEAD_DIM
AUG_POS = HEAD_DIM + 32
POS_SPLIT = 128


def _alibi_slopes():
    return [2.0 ** (-8.0 * (h + 1) / N_HEADS) for h in range(N_HEADS)]


def _nsa_tables(seq):
    pos = np.arange(seq)
    tbl = np.zeros((seq, HEAD_DIM), np.float32)
    tbl[pos, pos // SEL_BLOCK] = 1.0
    tbl[:, 32] = (pos // POS_SPLIT) * POS_SPLIT
    tbl[:, 33] = pos % POS_SPLIT
    ncmp = seq // CMP_BLOCK
    order = np.concatenate([np.arange(0, ncmp, 2), np.arange(1, ncmp, 2)])
    cend = (order + 1) * CMP_BLOCK - 1
    ctbl = np.zeros((ncmp, HEAD_DIM), np.float32)
    ctbl[:, 32] = (cend // POS_SPLIT) * POS_SPLIT
    ctbl[:, 33] = cend % POS_SPLIT
    pm = np.zeros((ncmp, ncmp), np.float32)
    pm[np.arange(ncmp), order] = 1.0
    slope = np.zeros((KV_HEADS, 16, QL), np.float32)
    sl = _alibi_slopes()
    for g in range(KV_HEADS):
        for m in range(GRP):
            slope[g, 0:2, m * TQ:(m + 1) * TQ] = sl[g * GRP + m]
    return (jnp.asarray(tbl), jnp.asarray(ctbl), jnp.asarray(cend.reshape(ncmp, 1).astype(np.int32)),
            jnp.asarray(pm, BF16), jnp.asarray(slope, BF16))


def _flash_chunk(kaug, vt, qa, valid, state):
    m, l, acc = state
    s = _dot(kaug, qa)
    if valid is not None:
        s = jnp.where(valid, s, NEG)
    mn = jnp.maximum(m, jnp.max(s, axis=0, keepdims=True))
    alpha = jnp.exp(m - mn)
    p = jnp.exp(s - mn)
    l = alpha * l + jnp.sum(p, axis=0, keepdims=True)
    acc = alpha * acc + _dot(vt, p.astype(BF16))
    return mn, l, acc


def _flash_init():
    return (jnp.full((1, QL), M_INIT, F32), jnp.zeros((1, QL), F32), jnp.zeros((HEAD_DIM, QL), F32))


def _nsa_seq_body(q_ref, kvp_ref, kvw_ref, gt_ref, kc_ref, tbl_ref, ctbl_ref, cend_ref, pm_ref,
                  slope_ref, y_ref, kslc, kwin, vslc, vwin, qaug):
    i = pl.program_id(1)
    l0 = i * TQ

    @pl.when(i == 0)
    def _():
        kslc[...] = jnp.zeros_like(kslc)
        kwin[...] = jnp.zeros_like(kwin)
        vslc[...] = jnp.zeros_like(vslc)
        vwin[...] = jnp.zeros_like(vwin)

    q = q_ref[...]
    tbl = tbl_ref[...]
    lane64 = lax.broadcasted_iota(jnp.int32, (TQ, HEAD_DIM), 1)
    tblw = jnp.where(lane64 < 32, 0.0, tbl)
    rows = pl.ds(pl.multiple_of(l0, TQ), TQ)
    ks_all = kvp_ref[0:KV_DIM, :].T
    kw_all = kvw_ref[0:KV_DIM, :].T
    for g in range(KV_HEADS):
        ks = ks_all[:, g * HEAD_DIM:(g + 1) * HEAD_DIM]
        kslc[g, rows, :] = jnp.concatenate([ks, tbl], axis=1).astype(BF16)
        kw = kw_all[:, g * HEAD_DIM:(g + 1) * HEAD_DIM]
        kwin[g, rows, :] = jnp.concatenate([kw, tblw], axis=1).astype(BF16)
    vslc[i] = kvp_ref[KV_DIM:2 * KV_DIM, :].astype(BF16)
    vwin[i] = kvw_ref[KV_DIM:2 * KV_DIM, :].astype(BF16)

    scale = HEAD_DIM ** -0.5
    qt = [(q[:, j * LANES:(j + 1) * LANES] * scale).T for j in range(ATTN_DIM // LANES)]
    for g in range(KV_HEADS):
        heads = []
        for m in range(GRP):
            h = g * GRP + m
            heads.append(qt[h // 2][(h % 2) * HEAD_DIM:(h % 2 + 1) * HEAD_DIM, :])
        qaug[g, 0:HEAD_DIM, :] = jnp.concatenate(heads, axis=1).astype(BF16)
        qaug[g, AUG_MASK:AUG_POS, :] = jnp.zeros((AUG_POS - AUG_MASK, QL), BF16)
        qaug[g, AUG_POS:AUG_POS + 16, :] = slope_ref[g]
        qaug[g, AUG_POS + 16:, :] = jnp.zeros((2 * HEAD_DIM - AUG_POS - 16, QL), BF16)

    ncmp = kc_ref.shape[0]
    nsel = ncmp // 2
    kcp = _dot(pm_ref[...], kc_ref[...].astype(BF16))
    vct = jnp.concatenate([kcp[:, 2 * KV_DIM - KV_DIM:2 * KV_DIM],
                           jnp.zeros((LANES - ncmp, KV_DIM), F32)], axis=0).T
    lpos = l0 + lax.broadcasted_iota(jnp.int32, (1, QL), 1) % TQ
    valid_c = cend_ref[...] <= lpos
    lq = l0 + lax.broadcasted_iota(jnp.int32, (nsel, TQ), 1)
    blk = lax.broadcasted_iota(jnp.int32, (nsel, TQ), 0)
    cur = lq // SEL_BLOCK
    forced = (blk == 0) | (blk == cur) | (blk == cur - 1)
    causal_blk = blk * SEL_BLOCK <= lq
    o_cmp = []
    for g in range(KV_HEADS):
        kca = jnp.concatenate([kcp[:, g * HEAD_DIM:(g + 1) * HEAD_DIM], ctbl_ref[...]], axis=1).astype(BF16)
        s = jnp.where(valid_c, _dot(kca, qaug[g]), NEG)
        mx = jnp.max(s, axis=0, keepdims=True)
        e = jnp.where(valid_c, jnp.exp(s - mx), 0.0)
        p = e * (1.0 / jnp.maximum(jnp.sum(e, axis=0, keepdims=True), 1.0))
        vt_g = vct[g * HEAD_DIM:(g + 1) * HEAD_DIM, 0:ncmp].astype(BF16)
        o_cmp.append(_dot(vt_g, p.astype(BF16)))
        psum = p[:, 0:TQ]
        for m in range(1, GRP):
            psum = psum + p[:, m * TQ:(m + 1) * TQ]
        imp = psum[0:nsel, :] + psum[nsel:, :]
        score = jnp.where(causal_blk, imp + jnp.where(forced, FORCE_BONUS, 0.0), NEG)
        sel = jnp.zeros((nsel, TQ), jnp.bool_)
        for _ in range(N_SELECT):
            best = jnp.max(score, axis=0, keepdims=True)
            idx = jnp.min(jnp.where(score == best, blk, nsel), axis=0, keepdims=True)
            hit = (blk == idx) & (best > 0.5 * NEG)
            sel = sel | hit
            score = jnp.where(hit, 2.0 * NEG, score)
        mb = jnp.where(sel, 0.0, NEG)
        qaug[g, AUG_MASK:AUG_POS, :] = jnp.concatenate([mb] * GRP, axis=1).astype(BF16)

    qpos = l0 + lax.broadcasted_iota(jnp.int32, (1, QL), 1) % TQ
    per = SLC_CHUNK // TQ

    def slc_chunk(cb, st, valid):
        k0 = pl.multiple_of(cb * SLC_CHUNK, SLC_CHUNK)
        out = []
        for g in range(KV_HEADS):
            vt = jnp.concatenate([vslc[cb * per + k, g * HEAD_DIM:(g + 1) * HEAD_DIM, :] for k in range(per)], axis=1)
            out.append(_flash_chunk(kslc[g, pl.ds(k0, SLC_CHUNK), :], vt, qaug[g], valid, st[g]))
        return tuple(out)

    cbd = i // per
    st = lax.fori_loop(0, cbd, lambda cb, s: slc_chunk(cb, s, None), tuple(_flash_init() for _ in range(KV_HEADS)))
    kpos = cbd * SLC_CHUNK + lax.broadcasted_iota(jnp.int32, (SLC_CHUNK, 1), 0)
    st = slc_chunk(cbd, st, kpos <= qpos)
    o_slc = [acc * (1.0 / l) for (m, l, acc) in st]

    nwc = WINDOW // TQ + 1
    cw0 = jnp.maximum(i - (nwc - 1), 0)
    kw0 = pl.multiple_of(cw0 * TQ, TQ)
    dist = qpos - (kw0 + lax.broadcasted_iota(jnp.int32, (nwc * TQ, 1), 0))
    vis = (dist >= 0) & (dist < WINDOW)
    o_win = []
    for g in range(KV_HEADS):
        vt = jnp.concatenate([vwin[cw0 + k, g * HEAD_DIM:(g + 1) * HEAD_DIM, :] for k in range(nwc)], axis=1)
        m, l, acc = _flash_chunk(kwin[g, pl.ds(kw0, nwc * TQ), :], vt, qaug[g], vis, _flash_init())
        o_win.append(acc * (1.0 / l))

    sg = _sigmoid(gt_ref[...].T[0:3 * N_HEADS, :])
    outs = []
    for g in range(KV_HEADS):
        def gate_row(br):
            return jnp.concatenate([sg[br * N_HEADS + g * GRP + m:br * N_HEADS + g * GRP + m + 1, :]
                                    for m in range(GRP)], axis=1)
        outs.append(gate_row(0) * o_cmp[g] + gate_row(1) * o_slc[g] + gate_row(2) * o_win[g])
    for j in range(ATTN_DIM // LANES):
        rows = []
        for h in (2 * j, 2 * j + 1):
            g, m = divmod(h, GRP)
            rows.append(outs[g][:, m * TQ:(m + 1) * TQ])
        y_ref[:, j * LANES:(j + 1) * LANES] = jnp.concatenate(rows, axis=0).T


def _nsa_seq(q, kvp_t, kvw_t, layer, gt, kc, bsz, seq):
    assert seq % SLC_CHUNK == 0 and seq >= WINDOW + TQ
    assert seq // SEL_BLOCK == AUG_POS - AUG_MASK
    nt = seq // TQ
    tbl, ctbl, cend, pm, slope = _nsa_tables(seq)
    ncmp = seq // CMP_BLOCK
    row = lambda w: pl.BlockSpec((TQ, w), lambda b, i: (b * nt + i, 0))
    return pl.pallas_call(
        _nsa_seq_body,
        grid=(bsz, nt),
        in_specs=[row(ATTN_DIM),
                  pl.BlockSpec((None, None, 2 * KV_DIM, TQ), lambda b, i: (layer, b, 1, i)),
                  pl.BlockSpec((None, None, 2 * KV_DIM, TQ), lambda b, i: (layer, b, 0, i)),
                  row(GATE_PAD),
                  pl.BlockSpec((None, ncmp, CMP_W), lambda b, i: (b, 0, 0)),
                  pl.BlockSpec((TQ, HEAD_DIM), lambda b, i: (i, 0)),
                  _full(ctbl.shape), _full(cend.shape), _full(pm.shape), _full(slope.shape)],
        out_specs=row(ATTN_DIM),
        out_shape=jax.ShapeDtypeStruct((bsz * seq, ATTN_DIM), F32),
        scratch_shapes=[pltpu.VMEM((KV_HEADS, seq, 2 * HEAD_DIM), BF16),
                        pltpu.VMEM((KV_HEADS, seq, 2 * HEAD_DIM), BF16),
                        pltpu.VMEM((nt, KV_DIM, TQ), BF16),
                        pltpu.VMEM((nt, KV_DIM, TQ), BF16),
                        pltpu.VMEM((KV_HEADS, 2 * HEAD_DIM, QL), BF16)],
        compiler_params=_cparams(("parallel", "arbitrary")),
        name="nsa_seq",
    )(q, kvp_t, kvw_t, gt, kc, tbl, ctbl, cend, pm, slope)


N_PICK = N_SELECT - 1


def _q_rows(q, g, scale):
    rows = [q[:, (g * GRP + m) * HEAD_DIM:(g * GRP + m + 1) * HEAD_DIM] for m in range(GRP)]
    return jnp.concatenate(rows + [jnp.zeros((SUBLANES - GRP, HEAD_DIM), F32)], axis=0) * scale


def _slope_col(g):
    sl = _alibi_slopes()
    row = lax.broadcasted_iota(jnp.int32, (SUBLANES, 1), 0)
    col = jnp.zeros((SUBLANES, 1), F32)
    for m in range(GRP):
        col = jnp.where(row == m, sl[g * GRP + m], col)
    return col


def _nsa_step_cmp_body(q_ref, kc_ref, pair_ref, o_ref, idx_ref, *, past_len, nseq):
    ncmp = kc_ref.shape[1]
    nsel = ncmp // 2
    cur = past_len // SEL_BLOCK
    scale = HEAD_DIM ** -0.5
    cend = (lax.broadcasted_iota(jnp.int32, (1, ncmp), 1) + 1) * CMP_BLOCK - 1
    dist = (past_len - cend).astype(F32)
    lane = lax.broadcasted_iota(jnp.int32, (1, nsel), 1)
    forced = (lane == 0) | (lane == cur) | (lane == cur - 1)
    lane_o = lax.broadcasted_iota(jnp.int32, (1, LANES), 1)
    pair = pair_ref[...]
    rows = []
    for bl in range(nseq):
        q = q_ref[bl]
        kc = kc_ref[bl]
        for g in range(KV_HEADS):
            q8 = _q_rows(q, g, scale).astype(BF16)
            s = _dot_nt(q8, kc[:, g * HEAD_DIM:(g + 1) * HEAD_DIM].astype(BF16)) - _slope_col(g) * dist
            mx = jnp.max(s, axis=1, keepdims=True)
            e = jnp.exp(s - mx)
            p = e / jnp.maximum(jnp.sum(e, axis=1, keepdims=True), 1.0)
            vg = kc[:, KV_DIM + g * HEAD_DIM:KV_DIM + (g + 1) * HEAD_DIM]
            o_ref[bl, g] = _dot(p.astype(BF16), vg.astype(BF16))
            psum = jnp.sum(p[0:GRP, :], axis=0, keepdims=True)
            p_hi, p_mid, p_lo = _split3(jnp.broadcast_to(psum, (SUBLANES, ncmp)))
            imp = (_dot(p_hi, pair) + _dot(p_mid, pair) + _dot(p_lo, pair))[0:1, :]
            rows.append(imp + jnp.where(forced, FORCE_BONUS, 0.0))
    score = jnp.concatenate(rows, axis=0)
    picks = []
    for t in range(N_PICK):
        best = jnp.max(score, axis=1, keepdims=True)
        idx = jnp.min(jnp.where(score == best, lane, nsel), axis=1, keepdims=True)
        score = jnp.where(lane == idx, NEG, score)
        picks.append(idx)
    for bl in range(nseq):
        idx_out = jnp.zeros((1, LANES), jnp.int32)
        for g in range(KV_HEADS):
            r = bl * KV_HEADS + g
            for t in range(N_PICK):
                idx_out = jnp.where(lane_o == g * N_SELECT + t, picks[t][r:r + 1, :], idx_out)
        idx_ref[bl] = idx_out


def _nsa_step_attn_body(sel_ref, pt_ref, *refs, past_len):
    nb = KV_HEADS * N_PICK
    blocks = refs[:nb]
    win_ref, q_ref, gt_ref, kvp_ref, kvw_ref, ocmp_ref, y_ref = refs[nb:]
    b = pl.program_id(0)
    q = q_ref[...]
    kvp = kvp_ref[...]
    kvw = kvw_ref[...]
    sg = _sigmoid(gt_ref[...])
    w = win_ref[...]
    nwin = w.shape[1]
    scale = HEAD_DIM ** -0.5
    pl_ = lax.broadcasted_iota(jnp.int32, (1, PAGE), 1)
    wl = lax.broadcasted_iota(jnp.int32, (1, nwin), 1)
    per_page = PAGE // SEL_BLOCK
    for g in range(KV_HEADS):
        q8 = _q_rows(q, g, scale)
        q8b = q8.astype(BF16)
        sc = _slope_col(g)
        hs = slice(g * HEAD_DIM, (g + 1) * HEAD_DIM)
        vs_ = slice(KV_DIM + g * HEAD_DIM, KV_DIM + (g + 1) * HEAD_DIM)
        k_new = kvp[:, 2 * KV_DIM + g * HEAD_DIM:2 * KV_DIM + (g + 1) * HEAD_DIM]
        v_new = kvp[:, 3 * KV_DIM + g * HEAD_DIM:3 * KV_DIM + (g + 1) * HEAD_DIM]
        s_cur = jnp.sum(q8 * k_new, axis=1, keepdims=True)
        ss, vs = [], []
        for t in range(N_PICK):
            blk = blocks[g * N_PICK + t][...]
            s_idx = sel_ref[b * KV_HEADS * N_SELECT + g * N_SELECT + t]
            kpos = (s_idx // per_page) * PAGE + pl_
            inblk = pl_ // SEL_BLOCK == s_idx % per_page
            st = _dot(q8b, blk[hs, :].astype(BF16)) - sc * (past_len - kpos).astype(F32)
            ss.append(jnp.where(inblk, st, NEG))
            vs.append(blk[vs_, :].astype(BF16))
        mx = functools.reduce(jnp.maximum, [jnp.max(s, axis=1, keepdims=True) for s in ss] + [s_cur])
        es = [jnp.exp(s - mx) for s in ss]
        e_cur = jnp.exp(s_cur - mx)
        den = functools.reduce(lambda a, c: a + c, [jnp.sum(e, axis=1, keepdims=True) for e in es] + [e_cur])
        acc = e_cur * v_new
        for e, v in zip(es, vs):
            acc = acc + _dot_nt(e.astype(BF16), v)
        o_slc = acc / den
        kw_new = kvw[:, g * HEAD_DIM:(g + 1) * HEAD_DIM]
        vw_new = kvw[:, KV_DIM + g * HEAD_DIM:KV_DIM + (g + 1) * HEAD_DIM]
        dw = (nwin - wl).astype(F32)
        s_w = jnp.where(nwin - wl < WINDOW, _dot(q8b, w[hs, :].astype(BF16)) - sc * dw, NEG)
        s_wc = jnp.sum(q8 * kw_new, axis=1, keepdims=True)
        mw = jnp.maximum(jnp.max(s_w, axis=1, keepdims=True), s_wc)
        e_w = jnp.exp(s_w - mw)
        e_wc = jnp.exp(s_wc - mw)
        o_win = (_dot_nt(e_w.astype(BF16), w[vs_, :].astype(BF16))
                 + e_wc * vw_new) / (jnp.sum(e_w, axis=1, keepdims=True) + e_wc)

        def gate_col(br):
            cols = [sg[:, br * N_HEADS + g * GRP + m:br * N_HEADS + g * GRP + m + 1] for m in range(GRP)]
            return jnp.concatenate(cols + [jnp.zeros((SUBLANES - GRP, 1), F32)], axis=0)

        y_ref[g] = gate_col(0) * ocmp_ref[g] + gate_col(1) * o_slc + gate_col(2) * o_win


def _nsa_step(q, kvp, kvw, gt, kc, cache, page_table, cache_win, layer):
    n, n_pages = page_table.shape
    past_len = n_pages * PAGE
    ncmp = kc.shape[1]
    nsel = ncmp // 2
    pair = np.zeros((ncmp, nsel), np.float32)
    pair[np.arange(ncmp), np.arange(ncmp) // 2] = 1.0
    r3 = lambda a: a.reshape(n, 1, a.shape[-1])
    row3 = lambda w: pl.BlockSpec((None, 1, w), lambda b, *_: (b, 0, 0))
    nseq = math.gcd(n, SUBLANES // KV_HEADS)
    o_cmp, idx = pl.pallas_call(
        functools.partial(_nsa_step_cmp_body, past_len=past_len, nseq=nseq),
        grid=(n // nseq,),
        in_specs=[pl.BlockSpec((nseq, 1, ATTN_DIM), lambda b: (b, 0, 0)),
                  pl.BlockSpec((nseq, ncmp, CMP_W), lambda b: (b, 0, 0)), _full(pair.shape)],
        out_specs=[pl.BlockSpec((nseq, KV_HEADS, SUBLANES, HEAD_DIM), lambda b: (b, 0, 0, 0)),
                   pl.BlockSpec((nseq, 1, LANES), lambda b: (b, 0, 0))],
        out_shape=[jax.ShapeDtypeStruct((n, KV_HEADS, SUBLANES, HEAD_DIM), F32),
                   jax.ShapeDtypeStruct((n, 1, LANES), jnp.int32)],
        compiler_params=_cparams(("parallel",)),
        name="nsa_step_cmp",
    )(r3(q), kc, jnp.asarray(pair, BF16))
    sel = idx[:, 0, :KV_HEADS * N_SELECT].reshape(-1)

    def blk_spec(j):
        g, t = divmod(j, N_PICK)

        def imap(b, sel_ref, pt_ref):
            s = sel_ref[b * KV_HEADS * N_SELECT + g * N_SELECT + t]
            return (layer, pt_ref[b * n_pages + s // (PAGE // SEL_BLOCK)], 1, 0)

        return pl.BlockSpec((None, None, 2 * KV_DIM, PAGE), imap)

    nb = KV_HEADS * N_PICK
    grid_spec = pltpu.PrefetchScalarGridSpec(
        num_scalar_prefetch=2,
        grid=(n,),
        in_specs=[blk_spec(j) for j in range(nb)]
        + [pl.BlockSpec((None, None, CMP_W, cache_win.shape[3]), lambda b, *_: (layer, b, 0, 0)),
           row3(ATTN_DIM), row3(GATE_PAD), row3(4 * KV_DIM), row3(2 * KV_DIM),
           pl.BlockSpec((None, KV_HEADS, SUBLANES, HEAD_DIM), lambda b, *_: (b, 0, 0, 0))],
        out_specs=pl.BlockSpec((None, KV_HEADS, SUBLANES, HEAD_DIM), lambda b, *_: (b, 0, 0, 0)),
    )
    y = pl.pallas_call(
        functools.partial(_nsa_step_attn_body, past_len=past_len),
        grid_spec=grid_spec,
        out_shape=jax.ShapeDtypeStruct((n, KV_HEADS, SUBLANES, HEAD_DIM), F32),
        compiler_params=_cparams(("arbitrary",)),
        name="nsa_step_attn",
    )(sel, page_table.reshape(-1), *([cache] * nb), cache_win, r3(q), r3(gt), r3(kvp), r3(kvw), o_cmp)
    return y[:, :, :GRP, :].reshape(n, ATTN_DIM)


def _patch_tail_body(t_ref, big_ref, o_ref):
    del big_ref
    o_ref[...] = t_ref[...]


def _patch_tail(x, xt, bsz, seq):
    nt = seq // MTAIL
    return pl.pallas_call(
        _patch_tail_body,
        grid=(bsz,),
        in_specs=[pl.BlockSpec((MTAIL, D_MODEL), lambda b: (b, 0)), pl.BlockSpec(memory_space=pl.ANY)],
        out_specs=pl.BlockSpec((MTAIL, D_MODEL), lambda b: (b * nt + nt - 1, 0)),
        out_shape=jax.ShapeDtypeStruct(x.shape, x.dtype),
        input_output_aliases={1: 0},
        compiler_params=_cparams(("parallel",)),
        name="patch_tail",
    )(xt, x)


def kernel(x_prompt, x_sample, cache_kv, page_table, cache_win, state_ssm, state_conv, norm_attn_g, w_in,
           ssm_a_re, ssm_a_im, ssm_log_dt, ssm_b_re, ssm_b_im, ssm_c_re, ssm_c_im, ssm_d, ssm_w_glu, ssm_b_glu,
           cmp_pe, cmp_w1, cmp_w2, conv_w, w_br_ssm, w_br_attn, w_br_conv, w_out, norm_ffn_g,
           w_router_group, b_router_group, w_router_expert, b_router_expert, moe_w_gate, moe_w_up, moe_w_down,
           norm_final_g):
    bp, lp, _ = x_prompt.shape
    bs = x_sample.shape[0]
    depth = w_in.shape[0]
    n_pool = cache_kv.shape[1]
    nwin = cache_win.shape[2]
    xp = x_prompt.reshape(bp * lp, D_MODEL)
    xs = x_sample.reshape(bs, D_MODEL)
    cache = cache_kv.transpose(0, 1, 3, 4, 5, 2).reshape(depth, n_pool, 4 * KV_DIM, PAGE)
    cwin = cache_win.transpose(0, 1, 3, 4, 5, 2).reshape(depth, bs, 2 * KV_DIM, nwin)
    gf = norm_final_g.reshape(1, D_MODEL)
    kv_s, win_s, ssm_p, ssm_s, conv_p, conv_s = ([] for _ in range(6))
    kv_bufs = None

    def ssm_state(hre, him, n):
        return jnp.stack([hre.reshape(n, SSM_GROUPS, SSM_STATE), him.reshape(n, SSM_GROUPS, SSM_STATE)], axis=-1)

    for l in range(depth):
        w = w_in[l]
        w1, w1_lo = _hilo(jnp.concatenate(
            [w[:, :C_GATE], jnp.pad(w[:, C_GATE:C_CONV], ((0, 0), (0, GATE_PAD - 3 * N_HEADS)))], axis=1))
        g_attn = norm_attn_g[l].reshape(1, D_MODEL)
        g_ffn = norm_ffn_g[l].reshape(1, D_MODEL)
        s5w = _s5_prepare(ssm_a_re[l], ssm_a_im[l], ssm_log_dt[l], ssm_b_re[l], ssm_b_im[l],
                          ssm_c_re[l], ssm_c_im[l], ssm_d[l], ssm_w_glu[l], ssm_b_glu[l])
        cw = _cmp_prepare(cmp_pe[l], cmp_w1[l], cmp_w2[l])
        mw = _merge_weights(w[:, C_CONV:], w_br_ssm[l], w_br_attn[l], w_br_conv[l], w_out[l], conv_w[l])
        mo = _moe_prepare(w_router_group[l], b_router_group[l], w_router_expert[l], b_router_expert[l],
                          moe_w_gate[l], moe_w_up[l], moe_w_down[l])
        final = l == depth - 1

        u, q, gt, *kv_bufs = _inproj_seq(xp, g_attn, w1, bp, lp, l, depth, kv_bufs)
        y_ssm, hre, him, pre_r, pre_i = _s5_seq(u, s5w, bp, lp)
        kc = _compress_seq(kv_bufs[0], l, cw, bp, lp)
        y_attn = _nsa_seq(q, kv_bufs[0], kv_bufs[1], l, gt, kc, bp, lp)
        x1, cl, cpre = _merge_seq(xp, g_attn, y_ssm, y_attn, mw, bp, lp)
        x2 = _moe(x1, g_ffn, mo, gf, 1024, final=final)
        if not final and lp >= 2 * TAIL:
            def tail(a, n, per=lp):
                return a.reshape(bp, per, a.shape[-1])[:, per - n:].reshape(bp * n, a.shape[-1])

            ut = _inproj(tail(xp, TAIL), g_attn, w1, TAIL, w1_lo)[0]
            yst = _s5_seq(ut, s5w, bp, TAIL, tl=TAIL, precise=True, h0=(pre_r, pre_i))[0]
            x1t = _merge_seq(tail(xp, MTAIL), g_attn, tail(yst, MTAIL, TAIL), tail(y_attn, MTAIL), mw, bp, MTAIL,
                             tm=MTAIL, precise=True, cin=cpre)[0]
            x2t = _moe(x1t, g_ffn, mo, gf, bp * MTAIL, final=False)
            x2 = _patch_tail(x2, x2t, bp, lp)
        xp = x2
        ssm_p.append(ssm_state(hre[:, 0], him[:, 0], bp))
        conv_p.append(cl[:, SUBLANES - 2:])

        u, q, kvp, kvw, gt = _inproj(xs, g_attn, w1, bs, w1_lo)
        st = state_ssm[l]
        y_ssm, hre, him = _s5_step(u, st[..., 0].reshape(bs, SSM_N), st[..., 1].reshape(bs, SSM_N), s5w)
        kc = _compress_pages(cache, page_table, l, cw)
        y_attn = _nsa_step(q, kvp, kvw, gt, kc, cache, page_table, cwin, l)
        prev = state_conv[l]
        x1, uc = _merge_step(xs, g_attn, y_ssm, y_attn, prev[:, 0], prev[:, 1], mw)
        xs = _moe(x1, g_ffn, mo, gf, bs, final=final)
        kv_s.append(kvp.reshape(bs, 1, 4, KV_HEADS, HEAD_DIM))
        win_s.append(jnp.concatenate([cache_win[l][:, 1:], kvw.reshape(bs, 1, 2, KV_HEADS, HEAD_DIM)], axis=1))
        ssm_s.append(ssm_state(hre, him, bs))
        conv_s.append(jnp.stack([prev[:, 1], uc], axis=1))

    kvp_t, kvw_t = kv_bufs
    nw = min(WINDOW, lp)
    kv_prompt = kvp_t.reshape(depth, bp, 4, KV_HEADS, HEAD_DIM, lp).transpose(0, 1, 5, 2, 3, 4)
    win_prompt = kvw_t[..., lp - nw:].reshape(depth, bp, 2, KV_HEADS, HEAD_DIM, nw).transpose(0, 1, 5, 2, 3, 4)
    return (xp.reshape(bp, lp, D_MODEL), xs.reshape(bs, 1, D_MODEL),
            kv_prompt, jnp.stack(kv_s), win_prompt, jnp.stack(win_s),
            jnp.stack(ssm_p), jnp.stack(ssm_s), jnp.stack(conv_p), jnp.stack(conv_s))
```
